```python
import math
import jax, jax.numpy as jnp
from jax import lax
import numpy as np

D_MODEL = 1024
BATCH = 4
SEQ = 4096
DEPTH = 2
DEC_BATCH = 32
DEC_SEQ = 1
PAST_LEN = 8192
PAGE_SIZE = 128

H_SB = 4
D_SB = 64
H_DF = 4
D_DF = 64
C_CONV = D_MODEL // 4
CONV_W = 31
N_BUCKETS = 32
MAX_DIST = 128
N_EXPERTS = 64
TOP_K = 8
N_GROUPS = 8
TOPK_GROUPS = 4
D_EXPERT = D_MODEL // 4
D_SHARED = D_MODEL // 4
ROUTE_SCALE = 2.5
MOE_BLOCK = 128
Q_BLOCK = 128
LN_EPS = 1e-5
ALPHA = (2 * DEPTH) ** 0.25
BETA = (8 * DEPTH) ** -0.25

W_SB = H_SB * D_SB
W_DF_QK = H_DF * 2 * D_DF
W_DF_V = H_DF * 2 * D_DF
IN_SIZES = (W_SB, W_SB, W_SB, W_DF_QK, W_DF_QK, W_DF_V, 2 * C_CONV, 3 * D_MODEL)
IN_WIDTH = 3 * W_SB + 2 * W_DF_QK + W_DF_V + 2 * C_CONV + 3 * D_MODEL

kernel_name = "hybrid_sb_diff_conformer_moe_step"

F32 = jnp.float32


def layer_norm(x, g, b):
    xf = x.astype(F32)
    mu = jnp.mean(xf, axis=-1, keepdims=True)
    var = jnp.mean(jnp.square(xf - mu), axis=-1, keepdims=True)
    y = (xf - mu) * lax.rsqrt(var + LN_EPS) * g.astype(F32) + b.astype(F32)
    return y.astype(x.dtype)


def rms_norm(x, g):
    xf = x.astype(F32)
    y = xf * lax.rsqrt(jnp.mean(jnp.square(xf), axis=-1, keepdims=True) + LN_EPS) * g.astype(F32)
    return y.astype(x.dtype)


def rel_bias(table, q_pos, k_pos):
    n = jnp.maximum(q_pos[:, None] - k_pos[None, :], 0)
    max_exact = N_BUCKETS // 2
    nf = jnp.maximum(n, 1).astype(F32)
    large = max_exact + (jnp.log(nf / max_exact) / math.log(MAX_DIST / max_exact)
                         * (N_BUCKETS - max_exact)).astype(jnp.int32)
    large = jnp.minimum(large, N_BUCKETS - 1)
    bucket = jnp.where(n < max_exact, n, large)
    return jnp.transpose(table[bucket].astype(F32), (2, 0, 1))


def sb_core(q, k, v, q_pos, k_pos):
    z = jnp.einsum('bqhd,bkhd->bhqk', q, k).astype(F32) * (D_SB ** -0.5)
    valid = k_pos[None, :] < q_pos[:, None]
    log_beta = jax.nn.log_sigmoid(z)
    log_keep = jnp.where(valid, jax.nn.log_sigmoid(-z), 0.0)
    suffix = lax.cumsum(log_keep, axis=3, reverse=True) - log_keep
    w = jnp.where(valid, jnp.exp(log_beta + suffix), 0.0)
    return jnp.einsum('bhqk,bkhd->bqhd', w.astype(v.dtype), v)


def diff_core(q, k, v, q_pos, k_pos, table, lam):
    s = jnp.einsum('bqhmd,bkhmd->bhmqk', q, k).astype(F32) * (D_DF ** -0.5)
    s = s + rel_bias(table, q_pos, k_pos)[None, :, None]
    causal = k_pos[None, :] <= q_pos[:, None]
    p = jax.nn.softmax(jnp.where(causal, s, -jnp.inf), axis=-1)
    a = p[:, :, 0] - lam * p[:, :, 1]
    return jnp.einsum('bhqk,bkhe->bqhe', a.astype(v.dtype), v)


def over_query_blocks(fn, q, q_pos):
    b, t = q.shape[0], q.shape[1]
    nb = t // Q_BLOCK
    qb = jnp.moveaxis(q.reshape((b, nb, Q_BLOCK) + q.shape[2:]), 1, 0)
    pb = q_pos.reshape(nb, Q_BLOCK)
    out = lax.map(lambda a: fn(a[0], a[1]), (qb, pb))
    out = jnp.moveaxis(out, 0, 1)
    return out.reshape((b, t) + out.shape[3:])


def conv_branch(u, state, conv_w, conv_b, g, b):
    full = jnp.concatenate([state, u], axis=1)
    y = lax.conv_general_dilated(full, conv_w[:, None, :], window_strides=(1,), padding='VALID',
                                 dimension_numbers=('NWC', 'WIO', 'NWC'),
                                 feature_group_count=C_CONV)
    y = layer_norm(y + conv_b, g, b)
    return jax.nn.silu(y), full[:, -(CONV_W - 1):]


def token_mixers(x, past, lam_init, p):
    (w_in, rel_table, lq1, lk1, lq2, lk2, subln_g, conv_w, conv_b, cln_g, cln_b,
     w_sb_out, w_df_out, w_conv_out, w_o) = p
    b, t, _ = x.shape
    h = x @ w_in
    splits = [int(i) for i in np.cumsum(IN_SIZES)[:-1]]
    q_sb, k_sb, v_sb, q_df, k_df, v_df, glu, gates = jnp.split(h, splits, axis=-1)
    q_sb = q_sb.reshape(b, t, H_SB, D_SB)
    k_sb = k_sb.reshape(b, t, H_SB, D_SB)
    v_sb = v_sb.reshape(b, t, H_SB, D_SB)
    q_df = q_df.reshape(b, t, H_DF, 2, D_DF)
    k_df = k_df.reshape(b, t, H_DF, 2, D_DF)
    v_df = v_df.reshape(b, t, H_DF, 2 * D_DF)
    u = glu[..., :C_CONV] * jax.nn.sigmoid(glu[..., C_CONV:])
    if past is None:
        p0 = 0
        ksb, vsb, kdf, vdf = k_sb, v_sb, k_df, v_df
        conv_state = jnp.zeros((b, CONV_W - 1, C_CONV), u.dtype)
    else:
        pk_sb, pv_sb, pk_df, pv_df, conv_state = past
        p0 = pk_sb.shape[1]
        ksb = jnp.concatenate([pk_sb, k_sb], axis=1)
        vsb = jnp.concatenate([pv_sb, v_sb], axis=1)
        kdf = jnp.concatenate([pk_df, k_df], axis=1)
        vdf = jnp.concatenate([pv_df, v_df], axis=1)
    q_pos = p0 + jnp.arange(t, dtype=jnp.int32)
    k_pos = jnp.arange(p0 + t, dtype=jnp.int32)
    lam = (jnp.exp(jnp.sum(lq1.astype(F32) * lk1.astype(F32)))
           - jnp.exp(jnp.sum(lq2.astype(F32) * lk2.astype(F32))) + lam_init)
    sb_fn = lambda qq, qp: sb_core(qq, ksb, vsb, qp, k_pos)
    df_fn = lambda qq, qp: diff_core(qq, kdf, vdf, qp, k_pos, rel_table, lam)
    if past is None:
        o_sb = over_query_blocks(sb_fn, q_sb, q_pos)
        o_df = over_query_blocks(df_fn, q_df, q_pos)
    else:
        o_sb = sb_fn(q_sb, q_pos)
        o_df = df_fn(q_df, q_pos)
    o_df = rms_norm(o_df, subln_g) * (1.0 - lam_init)
    o_c, new_conv = conv_branch(u, conv_state, conv_w, conv_b, cln_g, cln_b)
    g_sb, g_df, g_c = jnp.split(jax.nn.sigmoid(gates), 3, axis=-1)
    merged = (g_sb * (o_sb.reshape(b, t, W_SB) @ w_sb_out)
              + g_df * (o_df.reshape(b, t, W_DF_V) @ w_df_out)
              + g_c * (o_c @ w_conv_out))
    return merged @ w_o, (k_sb, v_sb, k_df, v_df, new_conv)


def route(x2, router_w, router_bias):
    scores = jax.nn.sigmoid(jnp.dot(x2.astype(F32), router_w.astype(F32)))
    biased = scores + router_bias.astype(F32)
    n = x2.shape[0]
    grp = biased.reshape(n, N_GROUPS, N_EXPERTS // N_GROUPS)
    grp_score = lax.top_k(grp, 2)[0].sum(-1)
    _, grp_idx = lax.top_k(grp_score, TOPK_GROUPS)
    grp_mask = jax.nn.one_hot(grp_idx, N_GROUPS, dtype=F32).sum(1) > 0
    expert_mask = jnp.repeat(grp_mask, N_EXPERTS // N_GROUPS, axis=1)
    _, top_idx = lax.top_k(jnp.where(expert_mask, biased, -jnp.inf), TOP_K)
    top_w = jnp.take_along_axis(scores, top_idx, axis=1)
    top_w = top_w / jnp.sum(top_w, axis=-1, keepdims=True) * ROUTE_SCALE
    return top_idx, top_w


def routed_experts(x2, top_idx, top_w, w_gate, w_up, w_down):
    n, d = x2.shape
    m = n * TOP_K
    e_flat = top_idx.reshape(m)
    tok_flat = jnp.arange(m, dtype=jnp.int32) // TOP_K
    w_flat = top_w.reshape(m)
    order = jnp.argsort(e_flat)
    e_sorted, tok_sorted, w_sorted = e_flat[order], tok_flat[order], w_flat[order]
    counts = jnp.bincount(e_flat, length=N_EXPERTS)
    start = jnp.cumsum(counts) - counts
    padded = (counts + MOE_BLOCK - 1) // MOE_BLOCK * MOE_BLOCK
    pend = jnp.cumsum(padded)
    pstart = pend - padded
    dest = pstart[e_sorted] + jnp.arange(m) - start[e_sorted]
    n_blocks = -(-m // MOE_BLOCK) + N_EXPERTS
    rows = n_blocks * MOE_BLOCK
    row_tok = jnp.full((rows,), n, jnp.int32).at[dest].set(tok_sorted)
    row_w = jnp.zeros((rows,), F32).at[dest].set(w_sorted)
    blk_expert = jnp.minimum(jnp.searchsorted(pend, jnp.arange(n_blocks) * MOE_BLOCK, side='right'),
                             N_EXPERTS - 1)
    x_pad = jnp.concatenate([x2, jnp.zeros((1, d), x2.dtype)], axis=0)
    xb = x_pad[row_tok].reshape(n_blocks, MOE_BLOCK, d)

    def expert_block(args):
        xblk, e = args
        hdn = jax.nn.silu(xblk @ w_gate[e]) * (xblk @ w_up[e])
        return hdn @ w_down[e]

    out = lax.map(expert_block, (xb, blk_expert)).reshape(rows, d)
    y = jnp.zeros((n + 1, d), F32).at[row_tok].add(out.astype(F32) * row_w[:, None])
    return y[:n].astype(x2.dtype)


def moe(x, f):
    router_w, router_bias, w_gate, w_up, w_down, sh_gate, sh_up, sh_down = f
    b, t, d = x.shape
    x2 = x.reshape(b * t, d)
    top_idx, top_w = route(x2, router_w, router_bias)
    y = routed_experts(x2, top_idx, top_w, w_gate, w_up, w_down)
    y = y + (jax.nn.silu(x2 @ sh_gate) * (x2 @ sh_up)) @ sh_down
    return y.reshape(b, t, d)


def gather_pages(cache_l, page_table):
    g = cache_l[page_table]
    return g.reshape((page_table.shape[0], page_table.shape[1] * cache_l.shape[1]) + cache_l.shape[2:])


def setup_inputs(seed: int = 0) -> dict:
    key = jax.random.key(seed)
    ks = iter(jax.random.split(key, 48))

    def nrm(shape, scale=1.0):
        return jax.random.normal(next(ks), shape, F32) * scale

    def gain(shape):
        return 1.0 + nrm(shape, 0.02)

    n_pages = PAST_LEN // PAGE_SIZE
    n_used = DEC_BATCH * n_pages
    n_phys = n_used + (n_used + 3) // 4
    page_table = jax.random.permutation(next(ks), n_phys)[:n_used].reshape(DEC_BATCH, n_pages).astype(jnp.int32)
    return {
        "x_prompt": nrm((BATCH, SEQ, D_MODEL)),
        "x_sample": nrm((DEC_BATCH, DEC_SEQ, D_MODEL)),
        "cache_sb_k": nrm((DEPTH, n_phys, PAGE_SIZE, H_SB, D_SB)),
        "cache_sb_v": nrm((DEPTH, n_phys, PAGE_SIZE, H_SB, D_SB)),
        "cache_df_k": nrm((DEPTH, n_phys, PAGE_SIZE, H_DF, 2, D_DF)),
        "cache_df_v": nrm((DEPTH, n_phys, PAGE_SIZE, H_DF, 2 * D_DF)),
        "state_conv": nrm((DEPTH, DEC_BATCH, CONV_W - 1, C_CONV)),
        "page_table": page_table,
        "w_in": nrm((DEPTH, D_MODEL, IN_WIDTH), D_MODEL ** -0.5),
        "rel_bias_table": nrm((N_BUCKETS, H_DF), 0.5),
        "lam_q1": nrm((DEPTH, D_DF), 0.1),
        "lam_k1": nrm((DEPTH, D_DF), 0.1),
        "lam_q2": nrm((DEPTH, D_DF), 0.1),
        "lam_k2": nrm((DEPTH, D_DF), 0.1),
        "subln_g": gain((DEPTH, 2 * D_DF)),
        "conv_w": nrm((DEPTH, CONV_W, C_CONV), CONV_W ** -0.5),
        "conv_b": nrm((DEPTH, C_CONV), 0.02),
        "conv_ln_g": gain((DEPTH, C_CONV)),
        "conv_ln_b": nrm((DEPTH, C_CONV), 0.02),
        "w_sb_out": nrm((DEPTH, W_SB, D_MODEL), W_SB ** -0.5),
        "w_df_out": nrm((DEPTH, W_DF_V, D_MODEL), W_DF_V ** -0.5),
        "w_conv_out": nrm((DEPTH, C_CONV, D_MODEL), C_CONV ** -0.5),
        "w_o": nrm((DEPTH, D_MODEL, D_MODEL), BETA * D_MODEL ** -0.5),
        "ln1_g": gain((DEPTH, D_MODEL)),
        "ln1_b": nrm((DEPTH, D_MODEL), 0.02),
        "router_w": nrm((DEPTH, D_MODEL, N_EXPERTS), D_MODEL ** -0.5),
        "router_bias": nrm((DEPTH, N_EXPERTS), 0.01),
        "w_gate": nrm((DEPTH, N_EXPERTS, D_MODEL, D_EXPERT), D_MODEL ** -0.5),
        "w_up": nrm((DEPTH, N_EXPERTS, D_MODEL, D_EXPERT), D_MODEL ** -0.5),
        "w_down": nrm((DEPTH, N_EXPERTS, D_EXPERT, D_MODEL), BETA * D_EXPERT ** -0.5),
        "sh_gate": nrm((DEPTH, D_MODEL, D_SHARED), D_MODEL ** -0.5),
        "sh_up": nrm((DEPTH, D_MODEL, D_SHARED), D_MODEL ** -0.5),
        "sh_down": nrm((DEPTH, D_SHARED, D_MODEL), BETA * D_SHARED ** -0.5),
        "ln2_g": gain((DEPTH, D_MODEL)),
        "ln2_b": nrm((DEPTH, D_MODEL), 0.02),
    }


def reference(x_prompt, x_sample, cache_sb_k, cache_sb_v, cache_df_k, cache_df_v, state_conv,
              page_table, w_in, rel_bias_table, lam_q1, lam_k1, lam_q2, lam_k2, subln_g,
              conv_w, conv_b, conv_ln_g, conv_ln_b, w_sb_out, w_df_out, w_conv_out, w_o,
              ln1_g, ln1_b, router_w, router_bias, w_gate, w_up, w_down, sh_gate, sh_up,
              sh_down, ln2_g, ln2_b):
    xp, xs = x_prompt, x_sample
    new_p, new_s = [], []
    for l in range(DEPTH):
        lam_init = 0.8 - 0.6 * math.exp(-0.3 * l)
        p = (w_in[l], rel_bias_table, lam_q1[l], lam_k1[l], lam_q2[l], lam_k2[l], subln_g[l],
             conv_w[l], conv_b[l], conv_ln_g[l], conv_ln_b[l], w_sb_out[l], w_df_out[l],
             w_conv_out[l], w_o[l])
        f = (router_w[l], router_bias[l], w_gate[l], w_up[l], w_down[l], sh_gate[l], sh_up[l], sh_down[l])
        m, rows = token_mixers(xp, None, lam_init, p)
        xp = layer_norm(ALPHA * xp + m, ln1_g[l], ln1_b[l])
        xp = layer_norm(ALPHA * xp + moe(xp, f), ln2_g[l], ln2_b[l])
        new_p.append(rows)
        past = (gather_pages(cache_sb_k[l], page_table), gather_pages(cache_sb_v[l], page_table),
                gather_pages(cache_df_k[l], page_table), gather_pages(cache_df_v[l], page_table),
                state_conv[l])
        m, rows = token_mixers(xs, past, lam_init, p)
        xs = layer_norm(ALPHA * xs + m, ln1_g[l], ln1_b[l])
        xs = layer_norm(ALPHA * xs + moe(xs, f), ln2_g[l], ln2_b[l])
        new_s.append(rows)
    p_sb_k = jnp.stack([r[0] for r in new_p])
    p_sb_v = jnp.stack([r[1] for r in new_p])
    p_df_k = jnp.stack([r[2] for r in new_p])
    p_df_v = jnp.stack([r[3] for r in new_p])
    p_conv = jnp.stack([r[4] for r in new_p])
    s_sb_k = jnp.stack([r[0] for r in new_s])
    s_sb_v = jnp.stack([r[1] for r in new_s])
    s_df_k = jnp.stack([r[2] for r in new_s])
    s_df_v = jnp.stack([r[3] for r in new_s])
    s_conv = jnp.stack([r[4] for r in new_s])
    return (xp, xs, p_sb_k, p_sb_v, p_df_k, p_df_v, p_conv, s_sb_k, s_sb_v, s_df_k, s_df_v, s_conv)
```

```python
import functools
import math

import jax
import jax.numpy as jnp
from jax import lax
from jax.experimental import pallas as pl
from jax.experimental.pallas import tpu as pltpu

F32 = jnp.float32
BF16 = jnp.bfloat16

D_MODEL = 1024
H_SB, D_SB = 4, 64
H_DF, D_DF = 4, 64
C_CONV = D_MODEL // 4
CONV_W = 31
N_BUCKETS, MAX_DIST = 32, 128
N_EXPERTS, TOP_K, N_GROUPS, TOPK_GROUPS = 64, 8, 8, 4
GROUP_SIZE = N_EXPERTS // N_GROUPS
D_EXPERT = D_MODEL // 4
D_SHARED = D_MODEL // 4
ROUTE_SCALE = 2.5
LN_EPS = 1e-5
PAGE_SIZE = 128
W_SB = H_SB * D_SB
W_DF = H_DF * 2 * D_DF
IN_SIZES = (W_SB, W_SB, W_SB, W_DF, W_DF, W_DF, 2 * C_CONV, 3 * D_MODEL)
IN_WIDTH = sum(IN_SIZES)

LANES = 128
SUBLANES = 8
VMEM_LIMIT = 56 * 1024 * 1024
NEG_INF = float("-inf")
NT_DIMS = (((1,), (1,)), ((), ()))


def _params(semantics, vmem=VMEM_LIMIT):
    return pltpu.CompilerParams(dimension_semantics=semantics, vmem_limit_bytes=vmem)


def _dot(a, b):
    return jnp.dot(a, b, preferred_element_type=F32)


def _dot_nt(a, b):
    return lax.dot_general(a, b, NT_DIMS, preferred_element_type=F32)


def _layer_norm(y, g, b):
    mu = jnp.mean(y, axis=-1, keepdims=True)
    d = y - mu
    var = jnp.mean(d * d, axis=-1, keepdims=True)
    return d * lax.rsqrt(var + LN_EPS) * g + b


def _silu(x):
    return x * jax.nn.sigmoid(x)


def _inproj_kernel(x_ref, w_ref, *out_refs):
    x = x_ref[...].astype(BF16)
    off = 0
    for ref, width in zip(out_refs, IN_SIZES):
        for c in range(0, width, 512):
            cw = min(512, width - c)
            ref[:, c:c + cw] = _dot(x, w_ref[:, off + c:off + c + cw])
        off += width


def _inproj(x, w_bf16, tm):
    n = x.shape[0]
    return pl.pallas_call(
        _inproj_kernel,
        grid=(n // tm,),
        in_specs=[pl.BlockSpec((tm, D_MODEL), lambda i: (i, 0)),
                  pl.BlockSpec((D_MODEL, IN_WIDTH), lambda i: (0, 0), pipeline_mode=pl.Buffered(1))],
        out_specs=[pl.BlockSpec((tm, w), lambda i: (i, 0)) for w in IN_SIZES],
        out_shape=[jax.ShapeDtypeStruct((n, w), F32) for w in IN_SIZES],
        compiler_params=_params(("parallel",)),
        name="inproj",
    )(x, w_bf16)


def _causal_pairs(nq, descending):
    qi, kj = [], []
    for i in range(nq):
        ks = range(i, -1, -1) if descending else range(i + 1)
        for j in ks:
            qi.append(i)
            kj.append(j)
    return jnp.asarray(qi, jnp.int32), jnp.asarray(kj, jnp.int32)


def _suffix_matrix():
    r = lax.broadcasted_iota(jnp.int32, (LANES, LANES), 0)
    c = lax.broadcasted_iota(jnp.int32, (LANES, LANES), 1)
    u = (r > c).astype(BF16)
    return jnp.concatenate([u, jnp.ones((LANES, LANES), BF16)], axis=1)


def _split_dot(x, rhs_bf16, terms=2):
    out = None
    for _ in range(terms):
        part = x.astype(BF16)
        x = x - part.astype(F32)
        d = _dot(part, rhs_bf16)
        out = d if out is None else out + d
    return out


def _log_sigmoid_pair(z):
    sp = jnp.log1p(jnp.exp(-jnp.abs(z)))
    return jnp.minimum(z, 0.0) - sp, -jnp.maximum(z, 0.0) - sp


def _sb_kernel(qi_ref, kj_ref, q_ref, k_ref, v_ref, tri_ref, o_ref, carry_ref, acc_ref, *, tq):
    p = pl.program_id(1)
    qi = qi_ref[p]
    kj = kj_ref[p]

    @pl.when(kj == qi)
    def _init():
        carry_ref[...] = jnp.zeros_like(carry_ref)
        acc_ref[...] = jnp.zeros_like(acc_ref)

    def step(diag):
        if diag:
            row = lax.broadcasted_iota(jnp.int32, (tq, tq), 0)
            col = lax.broadcasted_iota(jnp.int32, (tq, tq), 1)
            valid = col < row
        for h in range(H_SB):
            hs = slice(h * D_SB, (h + 1) * D_SB)
            q = q_ref[:, hs].astype(BF16)
            k = k_ref[:, hs].astype(BF16)
            z = _dot_nt(q, k) * (D_SB ** -0.5)
            log_beta, log_keep = _log_sigmoid_pair(z)
            if diag:
                log_keep = jnp.where(valid, log_keep, 0.0)
            carry = carry_ref[h]
            acc = acc_ref[h]
            for c in reversed(range(tq // LANES)):
                sl = slice(c * LANES, (c + 1) * LANES)
                st = _split_dot(log_keep[:, sl], tri_ref[...])
                w = jnp.exp(log_beta[:, sl] + st[:, :LANES] + carry)
                if diag:
                    w = jnp.where(valid[:, sl], w, 0.0)
                acc = acc + _dot(w.astype(BF16), v_ref[sl, hs].astype(BF16))
                carry = carry + st[:, LANES:]
            carry_ref[h] = carry
            acc_ref[h] = acc

    @pl.when(kj == qi)
    def _diag():
        step(True)

    @pl.when(kj < qi)
    def _off():
        step(False)

    @pl.when(kj == 0)
    def _fin():
        for h in range(H_SB):
            o_ref[:, h * D_SB:(h + 1) * D_SB] = acc_ref[h]


def _sb_attention(q, k, v, batch, seq, tq):
    nq = seq // tq
    qi, kj = _causal_pairs(nq, descending=True)
    grid_spec = pltpu.PrefetchScalarGridSpec(
        num_scalar_prefetch=2,
        grid=(batch, int(qi.shape[0])),
        in_specs=[pl.BlockSpec((tq, W_SB), lambda b, p, qi, kj: (b * nq + qi[p], 0)),
                  pl.BlockSpec((tq, W_SB), lambda b, p, qi, kj: (b * nq + kj[p], 0)),
                  pl.BlockSpec((tq, W_SB), lambda b, p, qi, kj: (b * nq + kj[p], 0)),
                  pl.BlockSpec((LANES, 2 * LANES), lambda b, p, qi, kj: (0, 0))],
        out_specs=pl.BlockSpec((tq, W_SB), lambda b, p, qi, kj: (b * nq + qi[p], 0)),
        scratch_shapes=[pltpu.VMEM((H_SB, tq, LANES), F32), pltpu.VMEM((H_SB, tq, D_SB), F32)],
    )
    return pl.pallas_call(
        functools.partial(_sb_kernel, tq=tq),
        grid_spec=grid_spec,
        out_shape=jax.ShapeDtypeStruct((batch * seq, W_SB), F32),
        compiler_params=_params(("parallel", "arbitrary")),
        name="sb_attention",
    )(qi, kj, q, k, v, _suffix_matrix())


def _bias_of_distance(table, n):
    max_exact = N_BUCKETS // 2
    nf = jnp.maximum(n, 1).astype(F32)
    large = max_exact + (jnp.log(nf / max_exact) / math.log(MAX_DIST / max_exact)
                         * (N_BUCKETS - max_exact)).astype(jnp.int32)
    large = jnp.minimum(large, N_BUCKETS - 1)
    bucket = jnp.where(n < max_exact, n, large)
    return table[bucket].astype(F32)


def _bias_tiles(table, tq):
    assert tq >= MAX_DIST, "blocks two or more behind must all fall in the last bucket"
    r = jnp.arange(tq, dtype=jnp.int32)[:, None]
    c = jnp.arange(tq, dtype=jnp.int32)[None, :]
    tiles = [_bias_of_distance(table, jnp.maximum(t * tq + r - c, 0)) for t in range(3)]
    return jnp.transpose(jnp.stack(tiles), (0, 3, 1, 2))


def _lambda(lq1, lk1, lq2, lk2, lam_init):
    return (jnp.exp(jnp.sum(lq1 * lk1, axis=-1, keepdims=True))
            - jnp.exp(jnp.sum(lq2 * lk2, axis=-1, keepdims=True)) + lam_init)


def _sub_norm(o, g, post_scale):
    ms = jnp.mean(o * o, axis=-1, keepdims=True)
    return o * lax.rsqrt(ms + LN_EPS) * g * post_scale


def _df_kernel(qi_ref, kj_ref, q_ref, k_ref, v_ref, bias_ref, lam_ref, g_ref, o_ref,
               m_ref, l_ref, acc_ref, *, tq, lam_init):
    p = pl.program_id(1)
    qi = qi_ref[p]
    kj = kj_ref[p]

    @pl.when(kj == 0)
    def _init():
        m_ref[...] = jnp.full_like(m_ref, NEG_INF)
        l_ref[...] = jnp.zeros_like(l_ref)
        acc_ref[...] = jnp.zeros_like(acc_ref)

    def step(diag):
        tile = jnp.minimum(qi - kj, 2)
        if diag:
            row = lax.broadcasted_iota(jnp.int32, (tq, tq), 0)
            col = lax.broadcasted_iota(jnp.int32, (tq, tq), 1)
            causal = col <= row
        for h in range(H_DF):
            vh = v_ref[:, h * 2 * D_DF:(h + 1) * 2 * D_DF].astype(BF16)
            bias = bias_ref[tile, h]
            for mp in range(2):
                r = 2 * h + mp
                rs = slice(r * D_DF, (r + 1) * D_DF)
                s = _dot_nt(q_ref[:, rs].astype(BF16), k_ref[:, rs].astype(BF16)) * (D_DF ** -0.5) + bias
                if diag:
                    s = jnp.where(causal, s, NEG_INF)
                m_old = m_ref[r]
                m_new = jnp.maximum(m_old, jnp.max(s, axis=-1, keepdims=True))
                pr = jnp.exp(s - m_new)
                alpha = jnp.exp(m_old - m_new)
                l_ref[r] = alpha * l_ref[r] + jnp.sum(pr, axis=-1, keepdims=True)
                acc_ref[r] = alpha * acc_ref[r] + _dot(pr.astype(BF16), vh)
                m_ref[r] = m_new

    @pl.when(kj == qi)
    def _diag():
        step(True)

    @pl.when(kj < qi)
    def _off():
        step(False)

    @pl.when(kj == qi)
    def _fin():
        lam = _lambda(lam_ref[0:1, :], lam_ref[1:2, :], lam_ref[2:3, :], lam_ref[3:4, :], lam_init)
        for h in range(H_DF):
            o = acc_ref[2 * h] / l_ref[2 * h] - lam * (acc_ref[2 * h + 1] / l_ref[2 * h + 1])
            o_ref[:, h * 2 * D_DF:(h + 1) * 2 * D_DF] = _sub_norm(o, g_ref[...], 1.0 - lam_init)


def _df_attention(q, k, v, bias_tiles, lam_vecs, subln_g, batch, seq, tq, lam_init):
    nq = seq // tq
    qi, kj = _causal_pairs(nq, descending=False)
    blk = lambda which: (lambda b, p, qi, kj: (b * nq + (qi if which == "q" else kj)[p], 0))
    grid_spec = pltpu.PrefetchScalarGridSpec(
        num_scalar_prefetch=2,
        grid=(batch, int(qi.shape[0])),
        in_specs=[pl.BlockSpec((tq, W_DF), blk("q")),
                  pl.BlockSpec((tq, W_DF), blk("k")),
                  pl.BlockSpec((tq, W_DF), blk("k")),
                  pl.BlockSpec((3, H_DF, tq, tq), lambda b, p, qi, kj: (0, 0, 0, 0)),
                  pl.BlockSpec((4, D_DF), lambda b, p, qi, kj: (0, 0)),
                  pl.BlockSpec((1, 2 * D_DF), lambda b, p, qi, kj: (0, 0))],
        out_specs=pl.BlockSpec((tq, W_DF), blk("q")),
        scratch_shapes=[pltpu.VMEM((2 * H_DF, tq, 1), F32), pltpu.VMEM((2 * H_DF, tq, 1), F32),
                        pltpu.VMEM((2 * H_DF, tq, 2 * D_DF), F32)],
    )
    return pl.pallas_call(
        functools.partial(_df_kernel, tq=tq, lam_init=lam_init),
        grid_spec=grid_spec,
        out_shape=jax.ShapeDtypeStruct((batch * seq, W_DF), F32),
        compiler_params=_params(("parallel", "arbitrary")),
        name="df_attention",
    )(qi, kj, q, k, v, bias_tiles, lam_vecs, subln_g)


CONV_HIST = CONV_W - 1
CONV_PAD = 32
CONV_CHUNK = 64


def _bf16_round(x):
    return x.astype(BF16).astype(F32)


def _conv_kernel(glu_ref, w_ref, cb_ref, g_ref, b_ref, o_ref, st_ref, ubuf_ref, tail_ref, *, tt):
    i = pl.program_id(1)

    @pl.when(i == 0)
    def _zero_history():
        ubuf_ref[0:CONV_PAD, :] = jnp.zeros((CONV_PAD, C_CONV), F32)

    @pl.when(i > 0)
    def _carry_history():
        ubuf_ref[0:CONV_PAD, :] = ubuf_ref[tt:tt + CONV_PAD, :]

    u = glu_ref[:, :C_CONV] * jax.nn.sigmoid(glu_ref[:, C_CONV:])
    ubuf_ref[CONV_PAD:CONV_PAD + tt, :] = _bf16_round(u)
    taps = [_bf16_round(w_ref[j:j + 1, :]) for j in range(CONV_W)]
    first = CONV_PAD - CONV_HIST
    for r0 in range(0, tt, CONV_CHUNK):
        acc = jnp.zeros((CONV_CHUNK, C_CONV), F32)
        for j in range(CONV_W):
            acc = acc + ubuf_ref[first + r0 + j:first + r0 + j + CONV_CHUNK, :] * taps[j]
        y = _layer_norm(acc + cb_ref[...], g_ref[...], b_ref[...])
        o_ref[r0:r0 + CONV_CHUNK, :] = _silu(y)

    @pl.when(i == pl.num_programs(1) - 1)
    def _final_state():
        tail_ref[...] = u[tt - CONV_PAD:, :]
        st_ref[...] = tail_ref[CONV_PAD - CONV_HIST:, :]


def _conv_branch(glu, conv_w, conv_b, g, b, batch, seq, tt):
    nt = seq // tt
    vec = lambda: pl.BlockSpec((1, C_CONV), lambda bb, i: (0, 0))
    return pl.pallas_call(
        functools.partial(_conv_kernel, tt=tt),
        grid=(batch, nt),
        in_specs=[pl.BlockSpec((tt, 2 * C_CONV), lambda bb, i: (bb * nt + i, 0)),
                  pl.BlockSpec((CONV_W, C_CONV), lambda bb, i: (0, 0)),
                  vec(), vec(), vec()],
        out_specs=[pl.BlockSpec((tt, C_CONV), lambda bb, i: (bb * nt + i, 0)),
                   pl.BlockSpec((None, CONV_HIST, C_CONV), lambda bb, i: (bb, 0, 0))],
        out_shape=[jax.ShapeDtypeStruct((batch * seq, C_CONV), F32),
                   jax.ShapeDtypeStruct((batch, CONV_HIST, C_CONV), F32)],
        scratch_shapes=[pltpu.VMEM((CONV_PAD + tt, C_CONV), F32), pltpu.VMEM((CONV_PAD, C_CONV), F32)],
        compiler_params=_params(("parallel", "arbitrary")),
        name="conv_branch",
    )(glu, conv_w, conv_b, g, b)


def _conv_dec_kernel(glu_ref, st_ref, w_ref, cb_ref, g_ref, b_ref, o_ref, u_ref):
    u = glu_ref[:, :C_CONV] * jax.nn.sigmoid(glu_ref[:, C_CONV:])
    acc = u * w_ref[CONV_HIST:CONV_W, :]
    for j in range(CONV_HIST):
        acc = acc + st_ref[j] * w_ref[j:j + 1, :]
    y = _layer_norm(acc + cb_ref[...], g_ref[...], b_ref[...])
    o_ref[...] = _silu(y)
    u_ref[...] = u


def _conv_decode(glu, state_t, conv_w, conv_b, g, b):
    n = glu.shape[0]
    return pl.pallas_call(
        _conv_dec_kernel,
        out_shape=[jax.ShapeDtypeStruct((n, C_CONV), F32), jax.ShapeDtypeStruct((n, C_CONV), F32)],
        name="conv_decode",
    )(glu, state_t, conv_w, conv_b, g, b)


def _merge_kernel(x_ref, osb_ref, odf_ref, oc_ref, gate_ref, wsb_ref, wdf_ref, wc_ref, wo_ref,
                  g_ref, b_ref, o_ref, *, alpha):
    merged = (jax.nn.sigmoid(gate_ref[:, 0:D_MODEL]) * _dot(osb_ref[...].astype(BF16), wsb_ref[...])
              + jax.nn.sigmoid(gate_ref[:, D_MODEL:2 * D_MODEL]) * _dot(odf_ref[...].astype(BF16), wdf_ref[...])
              + jax.nn.sigmoid(gate_ref[:, 2 * D_MODEL:]) * _dot(oc_ref[...].astype(BF16), wc_ref[...]))
    m = _dot(merged.astype(BF16), wo_ref[...])
    o_ref[...] = _layer_norm(alpha * x_ref[...] + m, g_ref[...], b_ref[...])


def _merge(x, o_sb, o_df, o_c, gates, w_sb, w_df, w_c, w_o, g, b, tm, alpha):
    n = x.shape[0]
    rows = lambda w: pl.BlockSpec((tm, w), lambda i: (i, 0))
    full = lambda a: pl.BlockSpec(a.shape, lambda i: (0, 0))
    return pl.pallas_call(
        functools.partial(_merge_kernel, alpha=alpha),
        grid=(n // tm,),
        in_specs=[rows(D_MODEL), rows(W_SB), rows(W_DF), rows(C_CONV), rows(3 * D_MODEL),
                  full(w_sb), full(w_df), full(w_c), full(w_o), full(g), full(b)],
        out_specs=rows(D_MODEL),
        out_shape=jax.ShapeDtypeStruct((n, D_MODEL), F32),
        compiler_params=_params(("parallel",)),
        name="merge_ln",
    )(x, o_sb, o_df, o_c, gates, w_sb, w_df, w_c, w_o, g, b)


def _first_argmax(vals, ids, axis, sentinel):
    mx = jnp.max(vals, axis=axis, keepdims=True)
    ix = jnp.min(jnp.where(vals == mx, ids, sentinel), axis=axis, keepdims=True)
    return mx, ix


def _router_kernel(x_ref, rw_ref, rb_ref, idx_ref, w_ref, *, tm):
    logits = _dot_nt(rw_ref[...], x_ref[...].astype(BF16))
    scores = jax.nn.sigmoid(logits)
    biased = scores + rb_ref[...]
    sub = lax.broadcasted_iota(jnp.int32, (GROUP_SIZE, tm), 0).astype(F32)
    gscore = jnp.zeros((N_GROUPS, tm), F32)
    for g in range(N_GROUPS):
        blk = biased[g * GROUP_SIZE:(g + 1) * GROUP_SIZE, :]
        m1, i1 = _first_argmax(blk, sub, 0, float(GROUP_SIZE))
        m2 = jnp.max(jnp.where(sub == i1, NEG_INF, blk), axis=0, keepdims=True)
        gscore = jnp.where(sub == float(g), m1 + m2, gscore)
    gmask = jnp.zeros((N_GROUPS, tm), F32)
    for _ in range(TOPK_GROUPS):
        _, ig = _first_argmax(gscore, sub, 0, float(N_GROUPS))
        sel = sub == ig
        gmask = jnp.where(sel, 1.0, gmask)
        gscore = jnp.where(sel, NEG_INF, gscore)
    eid = lax.broadcasted_iota(jnp.int32, (N_EXPERTS, tm), 0).astype(F32)
    cand = jnp.concatenate(
        [jnp.where(gmask[g:g + 1, :] > 0.0, biased[g * GROUP_SIZE:(g + 1) * GROUP_SIZE, :], NEG_INF)
         for g in range(N_GROUPS)], axis=0)
    total = jnp.zeros((1, tm), F32)
    picked = []
    for k in range(TOP_K):
        _, ie = _first_argmax(cand, eid, 0, float(N_EXPERTS))
        sel = eid == ie
        wk = jnp.sum(jnp.where(sel, scores, 0.0), axis=0, keepdims=True)
        cand = jnp.where(sel, NEG_INF, cand)
        idx_ref[k:k + 1, :] = ie.astype(jnp.int32)
        picked.append(wk)
        total = total + wk
    for k in range(TOP_K):
        w_ref[k:k + 1, :] = picked[k] / total * ROUTE_SCALE


def _router(x, rw_pad, rb_col, tm):
    n = x.shape[0]
    return pl.pallas_call(
        functools.partial(_router_kernel, tm=tm),
        grid=(n // tm,),
        in_specs=[pl.BlockSpec((tm, D_MODEL), lambda i: (i, 0)),
                  pl.BlockSpec((N_EXPERTS, D_MODEL), lambda i: (0, 0)),
                  pl.BlockSpec((N_EXPERTS, 1), lambda i: (0, 0))],
        out_specs=[pl.BlockSpec((TOP_K, tm), lambda i: (0, i)),
                   pl.BlockSpec((TOP_K, tm), lambda i: (0, i))],
        out_shape=[jax.ShapeDtypeStruct((TOP_K, n), jnp.int32),
                   jax.ShapeDtypeStruct((TOP_K, n), F32)],
        compiler_params=_params(("parallel",)),
        name="router",
    )(x, rw_pad, rb_col)


def _expert_kernel(be_ref, nused_ref, xs_ref, wg_ref, wu_ref, wd_ref, o_ref):
    b = pl.program_id(0)

    @pl.when(b < nused_ref[0])
    def _compute():
        x = xs_ref[...].astype(BF16)
        hdn = _silu(_dot(x, wg_ref[...])) * _dot(x, wu_ref[...])
        o_ref[...] = _dot(hdn.astype(BF16), wd_ref[...])

    @pl.when(b >= nused_ref[0])
    def _unused():
        o_ref[...] = jnp.zeros_like(o_ref)


def _experts(xs, blk_expert, n_used, wg, wu, wd, br):
    rows = xs.shape[0]
    grid_spec = pltpu.PrefetchScalarGridSpec(
        num_scalar_prefetch=2,
        grid=(rows // br,),
        in_specs=[pl.BlockSpec((br, D_MODEL), lambda b, be, nu: (b, 0)),
                  pl.BlockSpec((None, D_MODEL, D_EXPERT), lambda b, be, nu: (be[b], 0, 0)),
                  pl.BlockSpec((None, D_MODEL, D_EXPERT), lambda b, be, nu: (be[b], 0, 0)),
                  pl.BlockSpec((None, D_EXPERT, D_MODEL), lambda b, be, nu: (be[b], 0, 0))],
        out_specs=pl.BlockSpec((br, D_MODEL), lambda b, be, nu: (b, 0)),
    )
    return pl.pallas_call(
        _expert_kernel,
        grid_spec=grid_spec,
        out_shape=jax.ShapeDtypeStruct((rows, D_MODEL), F32),
        compiler_params=_params(("arbitrary",)),
        name="experts",
    )(blk_expert, n_used, xs, wg, wu, wd)


def _dispatch_tables(top_idx_t, br):
    k, n = top_idx_t.shape
    m = k * n
    onehot = (top_idx_t[:, :, None] == jnp.arange(N_EXPERTS, dtype=jnp.int32)).astype(jnp.int32).sum(0)
    rank = jnp.cumsum(onehot, axis=0) - onehot
    counts = onehot.sum(0)
    padded = (counts + br - 1) // br * br
    pend = jnp.cumsum(padded)
    pstart = pend - padded
    dest = pstart[top_idx_t] + jnp.take_along_axis(rank, top_idx_t.T, axis=1).T
    n_blocks = -(-m // br) + N_EXPERTS
    blk_expert = jnp.minimum(jnp.searchsorted(pend, jnp.arange(n_blocks, dtype=jnp.int32) * br, side="right"),
                             N_EXPERTS - 1).astype(jnp.int32)
    n_used = (pend[-1] // br).astype(jnp.int32).reshape(1)
    return dest.astype(jnp.int32), blk_expert, n_used, n_blocks


def _ffn_out_kernel(x_ref, y_ref, sg_ref, su_ref, sd_ref, g_ref, b_ref, o_ref, *, alpha):
    x = x_ref[...]
    xb = x.astype(BF16)
    hdn = _silu(_dot(xb, sg_ref[...])) * _dot(xb, su_ref[...])
    y = y_ref[...] + _dot(hdn.astype(BF16), sd_ref[...])
    o_ref[...] = _layer_norm(alpha * x + y, g_ref[...], b_ref[...])


def _ffn_out(x, y_routed, sg, su, sd, g, b, tm, alpha):
    n = x.shape[0]
    rows = lambda: pl.BlockSpec((tm, D_MODEL), lambda i: (i, 0))
    full = lambda a: pl.BlockSpec(a.shape, lambda i: (0, 0))
    return pl.pallas_call(
        functools.partial(_ffn_out_kernel, alpha=alpha),
        grid=(n // tm,),
        in_specs=[rows(), rows(), full(sg), full(su), full(sd), full(g), full(b)],
        out_specs=rows(),
        out_shape=jax.ShapeDtypeStruct((n, D_MODEL), F32),
        compiler_params=_params(("parallel",)),
        name="ffn_out_ln",
    )(x, y_routed, sg, su, sd, g, b)


def _moe(x1, f, tm, br, alpha):
    router_w, router_b, wg, wu, wd, sg, su, sd, ln_g, ln_b = f
    n = x1.shape[0]
    top_idx_t, top_w_t = _router(x1, router_w, router_b, tm)
    dest, blk_expert, n_used, n_blocks = _dispatch_tables(top_idx_t, br)
    tok = jnp.broadcast_to(jnp.arange(n, dtype=jnp.int32)[None, :], dest.shape)
    row_tok = jnp.zeros((n_blocks * br,), jnp.int32).at[dest.reshape(-1)].set(tok.reshape(-1))
    xs = x1[row_tok]
    out = _experts(xs, blk_expert, n_used, wg, wu, wd, br)
    y = jnp.sum(out[dest] * top_w_t[:, :, None], axis=0)
    return _ffn_out(x1, y, sg, su, sd, ln_g, ln_b, tm, alpha)


def _sb_dec_kernel(pt_ref, q_ref, *refs, pc):
    k_refs, v_refs = refs[:pc], refs[pc:2 * pc]
    tri_ref, o_ref, carry_ref, acc_ref = refs[2 * pc:]
    c = pl.program_id(1)

    @pl.when(c == 0)
    def _init():
        carry_ref[...] = jnp.zeros_like(carry_ref)
        acc_ref[...] = jnp.zeros_like(acc_ref)

    row = lax.broadcasted_iota(jnp.int32, (SUBLANES, W_SB), 0)
    col = lax.broadcasted_iota(jnp.int32, (SUBLANES, W_SB), 1)
    own = (col // D_SB) == row
    qbd = jnp.where(own, jnp.broadcast_to(q_ref[...], (SUBLANES, W_SB)), 0.0).astype(BF16)
    carry = carry_ref[...]
    acc = acc_ref[...]
    for p in range(pc):
        z = _dot_nt(qbd, k_refs[p][...].astype(BF16)) * (D_SB ** -0.5)
        log_beta, log_keep = _log_sigmoid_pair(z)
        st = _split_dot(log_keep, tri_ref[...], terms=3)
        w = jnp.exp(log_beta + st[:, :LANES] + carry)
        acc = acc + _dot(w.astype(BF16), v_refs[p][...].astype(BF16))
        carry = carry + st[:, LANES:]
    carry_ref[...] = carry
    acc_ref[...] = acc

    @pl.when(c == pl.num_programs(1) - 1)
    def _fin():
        o_ref[...] = jnp.sum(jnp.where(own, acc, 0.0), axis=0, keepdims=True)


def _page_spec(width, n_pages, pc, p):
    def index(b, c, pt):
        return (pt[b * n_pages + n_pages - 1 - (c * pc + p)], 0, 0)
    return pl.BlockSpec((None, PAGE_SIZE, width), index)


def _sb_decode(q, cache_k, cache_v, page_table, pc):
    nb, n_pages = page_table.shape
    row = lambda w: pl.BlockSpec((None, 1, w), lambda b, c, pt: (b, 0, 0))
    grid_spec = pltpu.PrefetchScalarGridSpec(
        num_scalar_prefetch=1,
        grid=(nb, n_pages // pc),
        in_specs=([row(W_SB)]
                  + [_page_spec(W_SB, n_pages, pc, p) for p in range(pc)]
                  + [_page_spec(W_SB, n_pages, pc, p) for p in range(pc)]
                  + [pl.BlockSpec((LANES, 2 * LANES), lambda b, c, pt: (0, 0))]),
        out_specs=row(W_SB),
        scratch_shapes=[pltpu.VMEM((SUBLANES, LANES), F32), pltpu.VMEM((SUBLANES, W_SB), F32)],
    )
    out = pl.pallas_call(
        functools.partial(_sb_dec_kernel, pc=pc),
        grid_spec=grid_spec,
        out_shape=jax.ShapeDtypeStruct((nb, 1, W_SB), F32),
        compiler_params=_params(("parallel", "arbitrary")),
        name="sb_decode",
    )(page_table.reshape(-1), q.reshape(nb, 1, W_SB), *([cache_k] * pc), *([cache_v] * pc), _suffix_matrix())
    return out.reshape(nb, W_SB)


def _df_dec_kernel(pt_ref, q_ref, kn_ref, vn_ref, bias_ref, bself_ref, lam_ref, g_ref, *refs,
                   pc, n_pages, lam_init):
    k_refs, v_refs = refs[:pc], refs[pc:2 * pc]
    o_ref, s_ref, m_ref, aself_ref, acc_ref = refs[2 * pc:]
    c = pl.program_id(1)
    nc = n_pages // pc

    @pl.when(c == 0)
    def _init():
        m_ref[...] = jnp.full_like(m_ref, NEG_INF)
        acc_ref[...] = jnp.zeros_like(acc_ref)

    def page_cols(page):
        return pl.ds(pl.multiple_of(page * PAGE_SIZE, PAGE_SIZE), PAGE_SIZE)

    @pl.when(c < nc)
    def _scores():
        row = lax.broadcasted_iota(jnp.int32, (SUBLANES, W_DF), 0)
        col = lax.broadcasted_iota(jnp.int32, (SUBLANES, W_DF), 1)
        own = (col // D_DF) == 2 * (row % H_DF) + row // H_DF
        qbd = jnp.where(own, jnp.broadcast_to(q_ref[...], (SUBLANES, W_DF)), 0.0).astype(BF16)
        m = m_ref[...]
        for p in range(pc):
            cols = page_cols(c * pc + p)
            s = _dot_nt(qbd, k_refs[p][...].astype(BF16)) * (D_DF ** -0.5) + bias_ref[:, cols]
            s_ref[:, cols] = s
            m = jnp.maximum(m, jnp.max(s, axis=-1, keepdims=True))
        m_ref[...] = m

        @pl.when(c == nc - 1)
        def _weights():
            kn = jnp.broadcast_to(_bf16_round(kn_ref[...]), (SUBLANES, W_DF))
            s_self = jnp.sum(qbd.astype(F32) * kn, axis=-1, keepdims=True) * (D_DF ** -0.5) + bself_ref[...]
            m_fin = jnp.maximum(m, s_self)
            pr = jnp.exp(s_ref[...] - m_fin)
            p_self = jnp.exp(s_self - m_fin)
            total = jnp.sum(pr, axis=-1, keepdims=True) + p_self
            lam = _lambda(lam_ref[0:1, :], lam_ref[1:2, :], lam_ref[2:3, :], lam_ref[3:4, :], lam_init)
            pn = pr / total
            pn_self = jnp.broadcast_to(p_self / total, (SUBLANES, LANES))
            s_ref[...] = pn - lam * pltpu.roll(pn, shift=H_DF, axis=0)
            aself_ref[...] = pn_self - lam * pltpu.roll(pn_self, shift=H_DF, axis=0)

    @pl.when(c >= nc)
    def _values():
        acc = acc_ref[...]
        for p in range(pc):
            cols = page_cols((c - nc) * pc + p)
            acc = acc + _dot(s_ref[:, cols].astype(BF16), v_refs[p][...].astype(BF16))
        acc_ref[...] = acc

        @pl.when(c == 2 * nc - 1)
        def _fin():
            vn = jnp.broadcast_to(_bf16_round(vn_ref[...]), (SUBLANES, W_DF))
            o = acc + _bf16_round(aself_ref[:, 0:1]) * vn
            for h in range(H_DF):
                hs = slice(h * 2 * D_DF, (h + 1) * 2 * D_DF)
                o_ref[:, hs] = _sub_norm(o[h:h + 1, hs], g_ref[...], 1.0 - lam_init)


def _df_decode(q, k_new, v_new, cache_k, cache_v, page_table, bias_past, bias_self, lam_vecs, subln_g,
               pc, lam_init):
    nb, n_pages = page_table.shape
    nc = n_pages // pc
    row = lambda w: pl.BlockSpec((None, 1, w), lambda b, c, pt: (b, 0, 0))
    full = lambda a: pl.BlockSpec(a.shape, lambda b, c, pt: (0,) * a.ndim)

    def page_spec(p, second_pass):
        def index(b, c, pt):
            chunk = jnp.maximum(c - nc, 0) if second_pass else jnp.minimum(c, nc - 1)
            return (pt[b * n_pages + chunk * pc + p], 0, 0)
        return pl.BlockSpec((None, PAGE_SIZE, W_DF), index)

    grid_spec = pltpu.PrefetchScalarGridSpec(
        num_scalar_prefetch=1,
        grid=(nb, 2 * nc),
        in_specs=([row(W_DF), row(W_DF), row(W_DF), full(bias_past), full(bias_self), full(lam_vecs),
                   full(subln_g)]
                  + [page_spec(p, False) for p in range(pc)]
                  + [page_spec(p, True) for p in range(pc)]),
        out_specs=row(W_DF),
        scratch_shapes=[pltpu.VMEM((SUBLANES, n_pages * PAGE_SIZE), F32), pltpu.VMEM((SUBLANES, 1), F32),
                        pltpu.VMEM((SUBLANES, LANES), F32), pltpu.VMEM((SUBLANES, W_DF), F32)],
    )
    r3 = lambda a: a.reshape(nb, 1, W_DF)
    out = pl.pallas_call(
        functools.partial(_df_dec_kernel, pc=pc, n_pages=n_pages, lam_init=lam_init),
        grid_spec=grid_spec,
        out_shape=jax.ShapeDtypeStruct((nb, 1, W_DF), F32),
        compiler_params=_params(("parallel", "arbitrary")),
        name="df_decode",
    )(page_table.reshape(-1), r3(q), r3(k_new), r3(v_new), bias_past, bias_self, lam_vecs, subln_g,
      *([cache_k] * pc), *([cache_v] * pc))
    return out.reshape(nb, W_DF)


PROMPT_TM = 256
PROMPT_ATT_BLOCK = 256
PROMPT_CONV_TILE = 512
PROMPT_MOE_BLOCK = 128
SAMPLE_MOE_BLOCK = 32
DECODE_PAGES_PER_STEP = 8


def kernel(x_prompt, x_sample, cache_sb_k, cache_sb_v, cache_df_k, cache_df_v, state_conv, page_table,
           w_in, rel_bias_table, lam_q1, lam_k1, lam_q2, lam_k2, subln_g, conv_w, conv_b, conv_ln_g,
           conv_ln_b, w_sb_out, w_df_out, w_conv_out, w_o, ln1_g, ln1_b, router_w, router_bias, w_gate,
           w_up, w_down, sh_gate, sh_up, sh_down, ln2_g, ln2_b):
    depth = w_in.shape[0]
    batch, seq, _ = x_prompt.shape
    nb = x_sample.shape[0]
    n_pages = page_table.shape[1]
    past_len = n_pages * PAGE_SIZE
    alpha = (2 * depth) ** 0.25
    n_phys = cache_sb_k.shape[1]

    xp = x_prompt.reshape(batch * seq, D_MODEL)
    xs = x_sample.reshape(nb, D_MODEL)
    bias_tiles = _bias_tiles(rel_bias_table, PROMPT_ATT_BLOCK)
    dist = past_len - jnp.arange(past_len, dtype=jnp.int32)
    bias_past = jnp.tile(_bias_of_distance(rel_bias_table, dist).T, (2, 1))
    bias_self = jnp.tile(_bias_of_distance(rel_bias_table, jnp.zeros((1,), jnp.int32)).T, (2, 1))
    row = lambda a: a.reshape(1, -1)

    new_p, new_s = [], []
    for l in range(depth):
        lam_init = 0.8 - 0.6 * math.exp(-0.3 * l)
        w_in_b = w_in[l].astype(BF16)
        lam_vecs = jnp.stack([lam_q1[l], lam_k1[l], lam_q2[l], lam_k2[l]])
        merge_w = (w_sb_out[l].astype(BF16), w_df_out[l].astype(BF16), w_conv_out[l].astype(BF16),
                   w_o[l].astype(BF16), row(ln1_g[l]), row(ln1_b[l]))
        ffn = (router_w[l].T.astype(BF16), router_bias[l].reshape(N_EXPERTS, 1),
               w_gate[l].astype(BF16), w_up[l].astype(BF16), w_down[l].astype(BF16),
               sh_gate[l].astype(BF16), sh_up[l].astype(BF16), sh_down[l].astype(BF16),
               row(ln2_g[l]), row(ln2_b[l]))
        conv_p = (conv_w[l], row(conv_b[l]), row(conv_ln_g[l]), row(conv_ln_b[l]))

        q_sb, k_sb, v_sb, q_df, k_df, v_df, glu, gates = _inproj(xp, w_in_b, PROMPT_TM)
        o_sb = _sb_attention(q_sb, k_sb, v_sb, batch, seq, PROMPT_ATT_BLOCK)
        o_df = _df_attention(q_df, k_df, v_df, bias_tiles, lam_vecs, row(subln_g[l]), batch, seq,
                             PROMPT_ATT_BLOCK, lam_init)
        o_c, p_conv = _conv_branch(glu, *conv_p, batch, seq, PROMPT_CONV_TILE)
        x1 = _merge(xp, o_sb, o_df, o_c, gates, *merge_w, PROMPT_TM, alpha)
        xp = _moe(x1, ffn, PROMPT_TM, PROMPT_MOE_BLOCK, alpha)
        new_p.append((k_sb, v_sb, k_df, v_df, p_conv))

        q_sb, k_sb, v_sb, q_df, k_df, v_df, glu, gates = _inproj(xs, w_in_b, nb)
        o_sb = _sb_decode(q_sb, cache_sb_k[l].reshape(n_phys, PAGE_SIZE, W_SB),
                          cache_sb_v[l].reshape(n_phys, PAGE_SIZE, W_SB), page_table, DECODE_PAGES_PER_STEP)
        o_df = _df_decode(q_df, k_df, v_df, cache_df_k[l].reshape(n_phys, PAGE_SIZE, W_DF),
                          cache_df_v[l].reshape(n_phys, PAGE_SIZE, W_DF), page_table, bias_past, bias_self,
                          lam_vecs, row(subln_g[l]), DECODE_PAGES_PER_STEP, lam_init)
        o_c, u_new = _conv_decode(glu, jnp.transpose(state_conv[l], (1, 0, 2)), *conv_p)
        s_conv = jnp.concatenate([state_conv[l][:, 1:], u_new[:, None, :]], axis=1)
        x1 = _merge(xs, o_sb, o_df, o_c, gates, *merge_w, nb, alpha)
        xs = _moe(x1, ffn, nb, SAMPLE_MOE_BLOCK, alpha)
        new_s.append((k_sb, v_sb, k_df, v_df, s_conv))

    def stacked(rows, i, shape):
        return jnp.stack([r[i] for r in rows]).reshape((depth,) + shape)

    return (xp.reshape(batch, seq, D_MODEL),
            xs.reshape(nb, 1, D_MODEL),
            stacked(new_p, 0, (batch, seq, H_SB, D_SB)),
            stacked(new_p, 1, (batch, seq, H_SB, D_SB)),
            stacked(new_p, 2, (batch, seq, H_DF, 2, D_DF)),
            stacked(new_p, 3, (batch, seq, H_DF, 2 * D_DF)),
            stacked(new_p, 4, (batch, CONV_HIST, C_CONV)),
            stacked(new_s, 0, (nb, 1, H_SB, D_SB)),
            stacked(new_s, 1, (nb, 1, H_SB, D_SB)),
            stacked(new_s, 2, (nb, 1, H_DF, 2, D_DF)),
            stacked(new_s, 3, (nb, 1, H_DF, 2 * D_DF)),
            stacked(new_s, 4, (nb, CONV_HIST, C_CONV)))
```

```python
import functools
import math

import jax
import jax.numpy as jnp
from jax import lax
from jax.experimental import pallas as pl
from jax.experimental.pallas import tpu as pltpu

F32 = jnp.float32
BF16 = jnp.bfloat16

D_MODEL = 1024
H_SB, D_SB = 4, 64
H_DF, D_DF = 4, 64
C_CONV = D_MODEL // 4
CONV_W = 31
N_BUCKETS, MAX_DIST = 32, 128
N_EXPERTS, TOP_K, N_GROUPS, TOPK_GROUPS = 64, 8, 8, 4
GROUP_SIZE = N_EXPERTS // N_GROUPS
D_EXPERT = D_MODEL // 4
D_SHARED = D_MODEL // 4
ROUTE_SCALE = 2.5
LN_EPS = 1e-5
PAGE_SIZE = 128
W_SB = H_SB * D_SB
W_DF = H_DF * 2 * D_DF
IN_SIZES = (W_SB, W_SB, W_SB, W_DF, W_DF, W_DF, 2 * C_CONV, 3 * D_MODEL)
IN_WIDTH = sum(IN_SIZES)

LANES = 128
SUBLANES = 8
VMEM_LIMIT = 56 * 1024 * 1024
NEG_INF = float("-inf")
NT_DIMS = (((1,), (1,)), ((), ()))


def _params(semantics, vmem=VMEM_LIMIT):
    return pltpu.CompilerParams(dimension_semantics=semantics, vmem_limit_bytes=vmem)


def _dot(a, b):
    return jnp.dot(a, b, preferred_element_type=F32)


def _dot_nt(a, b):
    return lax.dot_general(a, b, NT_DIMS, preferred_element_type=F32)


def _layer_norm(y, g, b):
    mu = jnp.mean(y, axis=-1, keepdims=True)
    d = y - mu
    var = jnp.mean(d * d, axis=-1, keepdims=True)
    return d * lax.rsqrt(var + LN_EPS) * g + b


def _silu(x):
    return x * jax.nn.sigmoid(x)


def _inproj_kernel(x_ref, w_ref, *out_refs):
    x = x_ref[...].astype(BF16)
    off = 0
    for ref, width in zip(out_refs, IN_SIZES):
        for c in range(0, width, 512):
            cw = min(512, width - c)
            ref[:, c:c + cw] = _dot(x, w_ref[:, off + c:off + c + cw])
        off += width


def _inproj(x, w_bf16, tm):
    n = x.shape[0]
    return pl.pallas_call(
        _inproj_kernel,
        grid=(n // tm,),
        in_specs=[pl.BlockSpec((tm, D_MODEL), lambda i: (i, 0)),
                  pl.BlockSpec((D_MODEL, IN_WIDTH), lambda i: (0, 0), pipeline_mode=pl.Buffered(1))],
        out_specs=[pl.BlockSpec((tm, w), lambda i: (i, 0)) for w in IN_SIZES],
        out_shape=[jax.ShapeDtypeStruct((n, w), F32) for w in IN_SIZES],
        compiler_params=_params(("parallel",)),
        name="inproj",
    )(x, w_bf16)


def _causal_pairs(nq, descending):
    qi, kj = [], []
    for i in range(nq):
        ks = range(i, -1, -1) if descending else range(i + 1)
        for j in ks:
            qi.append(i)
            kj.append(j)
    return jnp.asarray(qi, jnp.int32), jnp.asarray(kj, jnp.int32)


def _suffix_matrix():
    r = lax.broadcasted_iota(jnp.int32, (LANES, LANES), 0)
    c = lax.broadcasted_iota(jnp.int32, (LANES, LANES), 1)
    u = (r > c).astype(BF16)
    return jnp.concatenate([u, jnp.ones((LANES, LANES), BF16)], axis=1)


def _split_dot(x, rhs_bf16, terms=2):
    out = None
    for _ in range(terms):
        part = x.astype(BF16)
        x = x - part.astype(F32)
        d = _dot(part, rhs_bf16)
        out = d if out is None else out + d
    return out


def _log_sigmoid_pair(z):
    sp = jnp.log1p(jnp.exp(-jnp.abs(z)))
    return jnp.minimum(z, 0.0) - sp, -jnp.maximum(z, 0.0) - sp


def _sb_kernel(qi_ref, kj_ref, q_ref, k_ref, v_ref, tri_ref, o_ref, carry_ref, acc_ref, *, tq):
    p = pl.program_id(1)
    qi = qi_ref[p]
    kj = kj_ref[p]

    @pl.when(kj == qi)
    def _init():
        carry_ref[...] = jnp.zeros_like(carry_ref)
        acc_ref[...] = jnp.zeros_like(acc_ref)

    def step(diag):
        if diag:
            row = lax.broadcasted_iota(jnp.int32, (tq, tq), 0)
            col = lax.broadcasted_iota(jnp.int32, (tq, tq), 1)
            valid = col < row
        for h in range(H_SB):
            hs = slice(h * D_SB, (h + 1) * D_SB)
            q = q_ref[:, hs].astype(BF16)
            k = k_ref[:, hs].astype(BF16)
            z = _dot_nt(q, k) * (D_SB ** -0.5)
            log_beta, log_keep = _log_sigmoid_pair(z)
            if diag:
                log_keep = jnp.where(valid, log_keep, 0.0)
            carry = carry_ref[h]
            acc = acc_ref[h]
            for c in reversed(range(tq // LANES)):
                sl = slice(c * LANES, (c + 1) * LANES)
                st = _split_dot(log_keep[:, sl], tri_ref[...])
                w = jnp.exp(log_beta[:, sl] + st[:, :LANES] + carry)
                if diag:
                    w = jnp.where(valid[:, sl], w, 0.0)
                acc = acc + _dot(w.astype(BF16), v_ref[sl, hs].astype(BF16))
                carry = carry + st[:, LANES:]
            carry_ref[h] = carry
            acc_ref[h] = acc

    @pl.when(kj == qi)
    def _diag():
        step(True)

    @pl.when(kj < qi)
    def _off():
        step(False)

    @pl.when(kj == 0)
    def _fin():
        for h in range(H_SB):
            o_ref[:, h * D_SB:(h + 1) * D_SB] = acc_ref[h]


def _sb_attention(q, k, v, batch, seq, tq):
    nq = seq // tq
    qi, kj = _causal_pairs(nq, descending=True)
    grid_spec = pltpu.PrefetchScalarGridSpec(
        num_scalar_prefetch=2,
        grid=(batch, int(qi.shape[0])),
        in_specs=[pl.BlockSpec((tq, W_SB), lambda b, p, qi, kj: (b * nq + qi[p], 0)),
                  pl.BlockSpec((tq, W_SB), lambda b, p, qi, kj: (b * nq + kj[p], 0)),
                  pl.BlockSpec((tq, W_SB), lambda b, p, qi, kj: (b * nq + kj[p], 0)),
                  pl.BlockSpec((LANES, 2 * LANES), lambda b, p, qi, kj: (0, 0))],
        out_specs=pl.BlockSpec((tq, W_SB), lambda b, p, qi, kj: (b * nq + qi[p], 0)),
        scratch_shapes=[pltpu.VMEM((H_SB, tq, LANES), F32), pltpu.VMEM((H_SB, tq, D_SB), F32)],
    )
    return pl.pallas_call(
        functools.partial(_sb_kernel, tq=tq),
        grid_spec=grid_spec,
        out_shape=jax.ShapeDtypeStruct((batch * seq, W_SB), F32),
        compiler_params=_params(("parallel", "arbitrary")),
        name="sb_attention",
    )(qi, kj, q, k, v, _suffix_matrix())


def _bias_of_distance(table, n):
    max_exact = N_BUCKETS // 2
    nf = jnp.maximum(n, 1).astype(F32)
    large = max_exact + (jnp.log(nf / max_exact) / math.log(MAX_DIST / max_exact)
                         * (N_BUCKETS - max_exact)).astype(jnp.int32)
    large = jnp.minimum(large, N_BUCKETS - 1)
    bucket = jnp.where(n < max_exact, n, large)
    return table[bucket].astype(F32)


def _bias_tiles(table, tq):
    assert tq >= MAX_DIST, "blocks two or more behind must all fall in the last bucket"
    r = jnp.arange(tq, dtype=jnp.int32)[:, None]
    c = jnp.arange(tq, dtype=jnp.int32)[None, :]
    tiles = [_bias_of_distance(table, jnp.maximum(t * tq + r - c, 0)) for t in range(3)]
    return jnp.transpose(jnp.stack(tiles), (0, 3, 1, 2))


def _lambda(lq1, lk1, lq2, lk2, lam_init):
    return (jnp.exp(jnp.sum(lq1 * lk1, axis=-1, keepdims=True))
            - jnp.exp(jnp.sum(lq2 * lk2, axis=-1, keepdims=True)) + lam_init)


def _sub_norm(o, g, post_scale):
    ms = jnp.mean(o * o, axis=-1, keepdims=True)
    return o * lax.rsqrt(ms + LN_EPS) * g * post_scale


def _df_kernel(qi_ref, kj_ref, q_ref, k_ref, v_ref, bias_ref, lam_ref, g_ref, o_ref,
               m_ref, l_ref, acc_ref, *, tq, lam_init):
    p = pl.program_id(1)
    qi = qi_ref[p]
    kj = kj_ref[p]

    @pl.when(kj == 0)
    def _init():
        m_ref[...] = jnp.full_like(m_ref, NEG_INF)
        l_ref[...] = jnp.zeros_like(l_ref)
        acc_ref[...] = jnp.zeros_like(acc_ref)

    def step(diag):
        tile = jnp.minimum(qi - kj, 2)
        if diag:
            row = lax.broadcasted_iota(jnp.int32, (tq, tq), 0)
            col = lax.broadcasted_iota(jnp.int32, (tq, tq), 1)
            causal = col <= row
        for h in range(H_DF):
            vh = v_ref[:, h * 2 * D_DF:(h + 1) * 2 * D_DF].astype(BF16)
            bias = bias_ref[tile, h]
            for mp in range(2):
                r = 2 * h + mp
                rs = slice(r * D_DF, (r + 1) * D_DF)
                s = _dot_nt(q_ref[:, rs].astype(BF16), k_ref[:, rs].astype(BF16)) * (D_DF ** -0.5) + bias
                if diag:
                    s = jnp.where(causal, s, NEG_INF)
                m_old = m_ref[r]
                m_new = jnp.maximum(m_old, jnp.max(s, axis=-1, keepdims=True))
                pr = jnp.exp(s - m_new)
                alpha = jnp.exp(m_old - m_new)
                l_ref[r] = alpha * l_ref[r] + jnp.sum(pr, axis=-1, keepdims=True)
                acc_ref[r] = alpha * acc_ref[r] + _dot(pr.astype(BF16), vh)
                m_ref[r] = m_new

    @pl.when(kj == qi)
    def _diag():
        step(True)

    @pl.when(kj < qi)
    def _off():
        step(False)

    @pl.when(kj == qi)
    def _fin():
        lam = _lambda(lam_ref[0:1, :], lam_ref[1:2, :], lam_ref[2:3, :], lam_ref[3:4, :], lam_init)
        for h in range(H_DF):
            o = acc_ref[2 * h] / l_ref[2 * h] - lam * (acc_ref[2 * h + 1] / l_ref[2 * h + 1])
            o_ref[:, h * 2 * D_DF:(h + 1) * 2 * D_DF] = _sub_norm(o, g_ref[...], 1.0 - lam_init)


def _df_attention(q, k, v, bias_tiles, lam_vecs, subln_g, batch, seq, tq, lam_init):
    nq = seq // tq
    qi, kj = _causal_pairs(nq, descending=False)
    blk = lambda which: (lambda b, p, qi, kj: (b * nq + (qi if which == "q" else kj)[p], 0))
    grid_spec = pltpu.PrefetchScalarGridSpec(
        num_scalar_prefetch=2,
        grid=(batch, int(qi.shape[0])),
        in_specs=[pl.BlockSpec((tq, W_DF), blk("q")),
                  pl.BlockSpec((tq, W_DF), blk("k")),
                  pl.BlockSpec((tq, W_DF), blk("k")),
                  pl.BlockSpec((3, H_DF, tq, tq), lambda b, p, qi, kj: (0, 0, 0, 0)),
                  pl.BlockSpec((4, D_DF), lambda b, p, qi, kj: (0, 0)),
                  pl.BlockSpec((1, 2 * D_DF), lambda b, p, qi, kj: (0, 0))],
        out_specs=pl.BlockSpec((tq, W_DF), blk("q")),
        scratch_shapes=[pltpu.VMEM((2 * H_DF, tq, 1), F32), pltpu.VMEM((2 * H_DF, tq, 1), F32),
                        pltpu.VMEM((2 * H_DF, tq, 2 * D_DF), F32)],
    )
    return pl.pallas_call(
        functools.partial(_df_kernel, tq=tq, lam_init=lam_init),
        grid_spec=grid_spec,
        out_shape=jax.ShapeDtypeStruct((batch * seq, W_DF), F32),
        compiler_params=_params(("parallel", "arbitrary")),
        name="df_attention",
    )(qi, kj, q, k, v, bias_tiles, lam_vecs, subln_g)


CONV_HIST = CONV_W - 1
CONV_PAD = 32
CONV_CHUNK = 64


def _bf16_round(x):
    return x.astype(BF16).astype(F32)


def _conv_kernel(glu_ref, w_ref, cb_ref, g_ref, b_ref, o_ref, st_ref, ubuf_ref, tail_ref, *, tt):
    i = pl.program_id(1)

    @pl.when(i == 0)
    def _zero_history():
        ubuf_ref[0:CONV_PAD, :] = jnp.zeros((CONV_PAD, C_CONV), F32)

    @pl.when(i > 0)
    def _carry_history():
        ubuf_ref[0:CONV_PAD, :] = ubuf_ref[tt:tt + CONV_PAD, :]

    u = glu_ref[:, :C_CONV] * jax.nn.sigmoid(glu_ref[:, C_CONV:])
    ubuf_ref[CONV_PAD:CONV_PAD + tt, :] = _bf16_round(u)
    taps = [_bf16_round(w_ref[j:j + 1, :]) for j in range(CONV_W)]
    first = CONV_PAD - CONV_HIST
    for r0 in range(0, tt, CONV_CHUNK):
        acc = jnp.zeros((CONV_CHUNK, C_CONV), F32)
        for j in range(CONV_W):
            acc = acc + ubuf_ref[first + r0 + j:first + r0 + j + CONV_CHUNK, :] * taps[j]
        y = _layer_norm(acc + cb_ref[...], g_ref[...], b_ref[...])
        o_ref[r0:r0 + CONV_CHUNK, :] = _silu(y)

    @pl.when(i == pl.num_programs(1) - 1)
    def _final_state():
        tail_ref[...] = u[tt - CONV_PAD:, :]
        st_ref[...] = tail_ref[CONV_PAD - CONV_HIST:, :]


def _conv_branch(glu, conv_w, conv_b, g, b, batch, seq, tt):
    nt = seq // tt
    vec = lambda: pl.BlockSpec((1, C_CONV), lambda bb, i: (0, 0))
    return pl.pallas_call(
        functools.partial(_conv_kernel, tt=tt),
        grid=(batch, nt),
        in_specs=[pl.BlockSpec((tt, 2 * C_CONV), lambda bb, i: (bb * nt + i, 0)),
                  pl.BlockSpec((CONV_W, C_CONV), lambda bb, i: (0, 0)),
                  vec(), vec(), vec()],
        out_specs=[pl.BlockSpec((tt, C_CONV), lambda bb, i: (bb * nt + i, 0)),
                   pl.BlockSpec((None, CONV_HIST, C_CONV), lambda bb, i: (bb, 0, 0))],
        out_shape=[jax.ShapeDtypeStruct((batch * seq, C_CONV), F32),
                   jax.ShapeDtypeStruct((batch, CONV_HIST, C_CONV), F32)],
        scratch_shapes=[pltpu.VMEM((CONV_PAD + tt, C_CONV), F32), pltpu.VMEM((CONV_PAD, C_CONV), F32)],
        compiler_params=_params(("parallel", "arbitrary")),
        name="conv_branch",
    )(glu, conv_w, conv_b, g, b)


def _conv_dec_kernel(glu_ref, st_ref, w_ref, cb_ref, g_ref, b_ref, o_ref, u_ref):
    u = glu_ref[:, :C_CONV] * jax.nn.sigmoid(glu_ref[:, C_CONV:])
    acc = u * w_ref[CONV_HIST:CONV_W, :]
    for j in range(CONV_HIST):
        acc = acc + st_ref[j] * w_ref[j:j + 1, :]
    y = _layer_norm(acc + cb_ref[...], g_ref[...], b_ref[...])
    o_ref[...] = _silu(y)
    u_ref[...] = u


def _conv_decode(glu, state_t, conv_w, conv_b, g, b):
    n = glu.shape[0]
    return pl.pallas_call(
        _conv_dec_kernel,
        out_shape=[jax.ShapeDtypeStruct((n, C_CONV), F32), jax.ShapeDtypeStruct((n, C_CONV), F32)],
        name="conv_decode",
    )(glu, state_t, conv_w, conv_b, g, b)


def _merge_kernel(x_ref, osb_ref, odf_ref, oc_ref, gate_ref, wsb_ref, wdf_ref, wc_ref, wo_ref,
                  g_ref, b_ref, o_ref, *, alpha):
    merged = (jax.nn.sigmoid(gate_ref[:, 0:D_MODEL]) * _dot(osb_ref[...].astype(BF16), wsb_ref[...])
              + jax.nn.sigmoid(gate_ref[:, D_MODEL:2 * D_MODEL]) * _dot(odf_ref[...].astype(BF16), wdf_ref[...])
              + jax.nn.sigmoid(gate_ref[:, 2 * D_MODEL:]) * _dot(oc_ref[...].astype(BF16), wc_ref[...]))
    m = _dot(merged.astype(BF16), wo_ref[...])
    o_ref[...] = _layer_norm(alpha * x_ref[...] + m, g_ref[...], b_ref[...])


def _merge(x, o_sb, o_df, o_c, gates, w_sb, w_df, w_c, w_o, g, b, tm, alpha):
    n = x.shape[0]
    rows = lambda w: pl.BlockSpec((tm, w), lambda i: (i, 0))
    full = lambda a: pl.BlockSpec(a.shape, lambda i: (0, 0))
    return pl.pallas_call(
        functools.partial(_merge_kernel, alpha=alpha),
        grid=(n // tm,),
        in_specs=[rows(D_MODEL), rows(W_SB), rows(W_DF), rows(C_CONV), rows(3 * D_MODEL),
                  full(w_sb), full(w_df), full(w_c), full(w_o), full(g), full(b)],
        out_specs=rows(D_MODEL),
        out_shape=jax.ShapeDtypeStruct((n, D_MODEL), F32),
        compiler_params=_params(("parallel",)),
        name="merge_ln",
    )(x, o_sb, o_df, o_c, gates, w_sb, w_df, w_c, w_o, g, b)


def _first_argmax(vals, ids, axis, sentinel):
    mx = jnp.max(vals, axis=axis, keepdims=True)
    ix = jnp.min(jnp.where(vals == mx, ids, sentinel), axis=axis, keepdims=True)
    return mx, ix


def _router_kernel(x_ref, rw_ref, rb_ref, idx_ref, w_ref, *, tm):
    logits = _dot_nt(rw_ref[...], x_ref[...].astype(BF16))
    scores = jax.nn.sigmoid(logits)
    biased = scores + rb_ref[...]
    sub = lax.broadcasted_iota(jnp.int32, (GROUP_SIZE, tm), 0).astype(F32)
    gscore = jnp.zeros((N_GROUPS, tm), F32)
    for g in range(N_GROUPS):
        blk = biased[g * GROUP_SIZE:(g + 1) * GROUP_SIZE, :]
        m1, i1 = _first_argmax(blk, sub, 0, float(GROUP_SIZE))
        m2 = jnp.max(jnp.where(sub == i1, NEG_INF, blk), axis=0, keepdims=True)
        gscore = jnp.where(sub == float(g), m1 + m2, gscore)
    gmask = jnp.zeros((N_GROUPS, tm), F32)
    for _ in range(TOPK_GROUPS):
        _, ig = _first_argmax(gscore, sub, 0, float(N_GROUPS))
        sel = sub == ig
        gmask = jnp.where(sel, 1.0, gmask)
        gscore = jnp.where(sel, NEG_INF, gscore)
    eid = lax.broadcasted_iota(jnp.int32, (N_EXPERTS, tm), 0).astype(F32)
    cand = jnp.concatenate(
        [jnp.where(gmask[g:g + 1, :] > 0.0, biased[g * GROUP_SIZE:(g + 1) * GROUP_SIZE, :], NEG_INF)
         for g in range(N_GROUPS)], axis=0)
    total = jnp.zeros((1, tm), F32)
    picked = []
    for k in range(TOP_K):
        _, ie = _first_argmax(cand, eid, 0, float(N_EXPERTS))
        sel = eid == ie
        wk = jnp.sum(jnp.where(sel, scores, 0.0), axis=0, keepdims=True)
        cand = jnp.where(sel, NEG_INF, cand)
        idx_ref[k:k + 1, :] = ie.astype(jnp.int32)
        picked.append(wk)
        total = total + wk
    for k in range(TOP_K):
        w_ref[k:k + 1, :] = picked[k] / total * ROUTE_SCALE


def _router(x, rw_pad, rb_col, tm):
    n = x.shape[0]
    return pl.pallas_call(
        functools.partial(_router_kernel, tm=tm),
        grid=(n // tm,),
        in_specs=[pl.BlockSpec((tm, D_MODEL), lambda i: (i, 0)),
                  pl.BlockSpec((N_EXPERTS, D_MODEL), lambda i: (0, 0)),
                  pl.BlockSpec((N_EXPERTS, 1), lambda i: (0, 0))],
        out_specs=[pl.BlockSpec((TOP_K, tm), lambda i: (0, i)),
                   pl.BlockSpec((TOP_K, tm), lambda i: (0, i))],
        out_shape=[jax.ShapeDtypeStruct((TOP_K, n), jnp.int32),
                   jax.ShapeDtypeStruct((TOP_K, n), F32)],
        compiler_params=_params(("parallel",)),
        name="router",
    )(x, rw_pad, rb_col)


GATHER_UNROLL = 8


def _start_row_gather(src_hbm, row_of, dst_ref, sem, count):
    def body(j, carry):
        pltpu.make_async_copy(src_hbm.at[pl.ds(row_of(j), 1)], dst_ref.at[pl.ds(j, 1)], sem).start()
        return carry
    lax.fori_loop(0, count, body, 0, unroll=GATHER_UNROLL)


def _wait_row_gather(src_hbm, dst_ref, sem):
    pltpu.make_async_copy(src_hbm.at[pl.ds(0, dst_ref.shape[0])], dst_ref, sem).wait()


def _expert_kernel(be_ref, nused_ref, rt_ref, x_hbm, wg_ref, wu_ref, wd_ref, o_ref, xbuf, sem, *, br):
    b = pl.program_id(0)
    n_used = nused_ref[0]
    slot = b % 2

    def gather(blk, into):
        _start_row_gather(x_hbm, lambda j: rt_ref[blk, j], xbuf.at[into], sem.at[into], br)

    @pl.when((b == 0) & (n_used > 0))
    def _first():
        gather(0, 0)

    @pl.when(b + 1 < n_used)
    def _prefetch():
        gather(b + 1, 1 - slot)

    @pl.when(b < n_used)
    def _compute():
        _wait_row_gather(x_hbm, xbuf.at[slot], sem.at[slot])
        x = xbuf[slot].astype(BF16)
        hdn = _silu(_dot(x, wg_ref[...])) * _dot(x, wu_ref[...])
        o_ref[...] = _dot(hdn.astype(BF16), wd_ref[...])

    @pl.when(b >= n_used)
    def _unused():
        o_ref[...] = jnp.zeros_like(o_ref)


def _experts(x, row_tok, blk_expert, n_used, wg, wu, wd, br):
    n_blocks = row_tok.shape[0]
    grid_spec = pltpu.PrefetchScalarGridSpec(
        num_scalar_prefetch=3,
        grid=(n_blocks,),
        in_specs=[pl.BlockSpec(memory_space=pl.ANY),
                  pl.BlockSpec((None, D_MODEL, D_EXPERT), lambda b, be, nu, rt: (be[b], 0, 0)),
                  pl.BlockSpec((None, D_MODEL, D_EXPERT), lambda b, be, nu, rt: (be[b], 0, 0)),
                  pl.BlockSpec((None, D_EXPERT, D_MODEL), lambda b, be, nu, rt: (be[b], 0, 0))],
        out_specs=pl.BlockSpec((br, D_MODEL), lambda b, be, nu, rt: (b, 0)),
        scratch_shapes=[pltpu.VMEM((2, br, D_MODEL), F32), pltpu.SemaphoreType.DMA((2,))],
    )
    return pl.pallas_call(
        functools.partial(_expert_kernel, br=br),
        grid_spec=grid_spec,
        out_shape=jax.ShapeDtypeStruct((n_blocks * br, D_MODEL), F32),
        compiler_params=_params(("arbitrary",)),
        name="experts",
    )(blk_expert, n_used, row_tok, x, wg, wu, wd)


def _dispatch_tables(top_idx_t, br):
    k, n = top_idx_t.shape
    m = k * n
    onehot = (top_idx_t[:, :, None] == jnp.arange(N_EXPERTS, dtype=jnp.int32)).astype(jnp.int32).sum(0)
    rank = jnp.cumsum(onehot, axis=0) - onehot
    counts = onehot.sum(0)
    padded = (counts + br - 1) // br * br
    pend = jnp.cumsum(padded)
    pstart = pend - padded
    dest = pstart[top_idx_t] + jnp.take_along_axis(rank, top_idx_t.T, axis=1).T
    n_blocks = -(-m // br) + N_EXPERTS
    blk_expert = jnp.minimum(jnp.searchsorted(pend, jnp.arange(n_blocks, dtype=jnp.int32) * br, side="right"),
                             N_EXPERTS - 1).astype(jnp.int32)
    n_used = (pend[-1] // br).astype(jnp.int32).reshape(1)
    return dest.astype(jnp.int32), blk_expert, n_used, n_blocks


def _ffn_out_kernel(dest_ref, x_ref, tw_ref, out_hbm, sg_ref, su_ref, sd_ref, g_ref, b_ref, o_ref, ybuf, sem,
                    *, alpha, tm):
    i = pl.program_id(0)
    slot = i % 2

    def gather(tile, into):
        for k in range(TOP_K):
            _start_row_gather(out_hbm, lambda j, k=k: dest_ref[k, tile * tm + j], ybuf.at[into, k],
                              sem.at[into], tm)

    @pl.when(i == 0)
    def _first():
        gather(0, 0)

    @pl.when(i + 1 < pl.num_programs(0))
    def _prefetch():
        gather(i + 1, 1 - slot)

    for k in range(TOP_K):
        _wait_row_gather(out_hbm, ybuf.at[slot, k], sem.at[slot])
    x = x_ref[...]
    xb = x.astype(BF16)
    hdn = _silu(_dot(xb, sg_ref[...])) * _dot(xb, su_ref[...])
    y = ybuf[slot, 0] * tw_ref[:, 0:1]
    for k in range(1, TOP_K):
        y = y + ybuf[slot, k] * tw_ref[:, k:k + 1]
    y = y + _dot(hdn.astype(BF16), sd_ref[...])
    o_ref[...] = _layer_norm(alpha * x + y, g_ref[...], b_ref[...])


def _ffn_out(x, expert_out, dest, top_w, sg, su, sd, g, b, tm, alpha):
    n = x.shape[0]
    rows = lambda w: pl.BlockSpec((tm, w), lambda i, d: (i, 0))
    full = lambda a: pl.BlockSpec(a.shape, lambda i, d: (0, 0))
    grid_spec = pltpu.PrefetchScalarGridSpec(
        num_scalar_prefetch=1,
        grid=(n // tm,),
        in_specs=[rows(D_MODEL), rows(TOP_K), pl.BlockSpec(memory_space=pl.ANY),
                  full(sg), full(su), full(sd), full(g), full(b)],
        out_specs=rows(D_MODEL),
        scratch_shapes=[pltpu.VMEM((2, TOP_K, tm, D_MODEL), F32), pltpu.SemaphoreType.DMA((2,))],
    )
    return pl.pallas_call(
        functools.partial(_ffn_out_kernel, alpha=alpha, tm=tm),
        grid_spec=grid_spec,
        out_shape=jax.ShapeDtypeStruct((n, D_MODEL), F32),
        compiler_params=_params(("arbitrary",)),
        name="ffn_out_ln",
    )(dest, x, top_w, expert_out, sg, su, sd, g, b)


def _moe(x1, f, tm, br, alpha):
    router_w, router_b, wg, wu, wd, sg, su, sd, ln_g, ln_b = f
    n = x1.shape[0]
    top_idx_t, top_w_t = _router(x1, router_w, router_b, tm)
    dest, blk_expert, n_used, n_blocks = _dispatch_tables(top_idx_t, br)
    tok = jnp.broadcast_to(jnp.arange(n, dtype=jnp.int32)[None, :], dest.shape)
    row_tok = jnp.zeros((n_blocks * br,), jnp.int32).at[dest.reshape(-1)].set(tok.reshape(-1))
    out = _experts(x1, row_tok.reshape(n_blocks, br), blk_expert, n_used, wg, wu, wd, br)
    return _ffn_out(x1, out, dest, top_w_t.T, sg, su, sd, ln_g, ln_b, min(tm, COMBINE_TM), alpha)


def _sb_dec_kernel(pt_ref, q_ref, *refs, pc):
    k_refs, v_refs = refs[:pc], refs[pc:2 * pc]
    tri_ref, o_ref, carry_ref, acc_ref = refs[2 * pc:]
    c = pl.program_id(1)

    @pl.when(c == 0)
    def _init():
        carry_ref[...] = jnp.zeros_like(carry_ref)
        acc_ref[...] = jnp.zeros_like(acc_ref)

    row = lax.broadcasted_iota(jnp.int32, (SUBLANES, W_SB), 0)
    col = lax.broadcasted_iota(jnp.int32, (SUBLANES, W_SB), 1)
    own = (col // D_SB) == row
    qbd = jnp.where(own, jnp.broadcast_to(q_ref[...], (SUBLANES, W_SB)), 0.0).astype(BF16)
    carry = carry_ref[...]
    acc = acc_ref[...]
    for p in range(pc):
        z = _dot(qbd, k_refs[p][...].astype(BF16)) * (D_SB ** -0.5)
        log_beta, log_keep = _log_sigmoid_pair(z)
        st = _split_dot(log_keep, tri_ref[...], terms=3)
        w = jnp.exp(log_beta + st[:, :LANES] + carry)
        acc = acc + _dot_nt(w.astype(BF16), v_refs[p][...].astype(BF16))
        carry = carry + st[:, LANES:]
    carry_ref[...] = carry
    acc_ref[...] = acc

    @pl.when(c == pl.num_programs(1) - 1)
    def _fin():
        o_ref[...] = jnp.sum(jnp.where(own, acc, 0.0), axis=0, keepdims=True)


def _sb_decode(q, cache_kt, cache_vt, layer, page_table, pc):
    nb, n_pages = page_table.shape

    def page_spec(p):
        def index(b, c, pt):
            return (layer, pt[b * n_pages + n_pages - 1 - (c * pc + p)], 0, 0)
        return pl.BlockSpec((None, None, W_SB, PAGE_SIZE), index)

    row = lambda w: pl.BlockSpec((None, 1, w), lambda b, c, pt: (b, 0, 0))
    grid_spec = pltpu.PrefetchScalarGridSpec(
        num_scalar_prefetch=1,
        grid=(nb, n_pages // pc),
        in_specs=([row(W_SB)]
                  + [page_spec(p) for p in range(pc)]
                  + [page_spec(p) for p in range(pc)]
                  + [pl.BlockSpec((LANES, 2 * LANES), lambda b, c, pt: (0, 0))]),
        out_specs=row(W_SB),
        scratch_shapes=[pltpu.VMEM((SUBLANES, LANES), F32), pltpu.VMEM((SUBLANES, W_SB), F32)],
    )
    out = pl.pallas_call(
        functools.partial(_sb_dec_kernel, pc=pc),
        grid_spec=grid_spec,
        out_shape=jax.ShapeDtypeStruct((nb, 1, W_SB), F32),
        compiler_params=_params(("parallel", "arbitrary")),
        name="sb_decode",
    )(page_table.reshape(-1), q.reshape(nb, 1, W_SB), *([cache_kt] * pc), *([cache_vt] * pc), _suffix_matrix())
    return out.reshape(nb, W_SB)


def _df_dec_kernel(pt_ref, q_ref, kn_ref, vn_ref, bias_ref, bself_ref, lam_ref, g_ref, spread_ref, *refs,
                   pc, n_pages, lam_init):
    k_refs, v_refs = refs[:pc], refs[pc:2 * pc]
    o_ref, s_ref, m_ref, aself_ref, acc_ref = refs[2 * pc:]
    c = pl.program_id(1)
    nc = n_pages // pc

    @pl.when(c == 0)
    def _init():
        m_ref[...] = jnp.full_like(m_ref, NEG_INF)
        acc_ref[...] = jnp.zeros_like(acc_ref)

    def page_cols(page):
        return pl.ds(pl.multiple_of(page * PAGE_SIZE, PAGE_SIZE), PAGE_SIZE)

    @pl.when(c < nc)
    def _scores():
        row = lax.broadcasted_iota(jnp.int32, (SUBLANES, W_DF), 0)
        col = lax.broadcasted_iota(jnp.int32, (SUBLANES, W_DF), 1)
        own = (col // D_DF) == 2 * (row % H_DF) + row // H_DF
        qbd = jnp.where(own, jnp.broadcast_to(q_ref[...], (SUBLANES, W_DF)), 0.0).astype(BF16)
        m = m_ref[...]
        for p in range(pc):
            cols = page_cols(c * pc + p)
            s = _dot(qbd, k_refs[p][...].astype(BF16)) * (D_DF ** -0.5) + bias_ref[:, cols]
            s_ref[:, cols] = s
            m = jnp.maximum(m, jnp.max(s, axis=-1, keepdims=True))
        m_ref[...] = m

        @pl.when(c == nc - 1)
        def _weights():
            kn = jnp.broadcast_to(_bf16_round(kn_ref[...]), (SUBLANES, W_DF))
            s_self = jnp.sum(qbd.astype(F32) * kn, axis=-1, keepdims=True) * (D_DF ** -0.5) + bself_ref[...]
            m_fin = jnp.maximum(m, s_self)
            pr = jnp.exp(s_ref[...] - m_fin)
            p_self = jnp.exp(s_self - m_fin)
            total = jnp.sum(pr, axis=-1, keepdims=True) + p_self
            lam = _lambda(lam_ref[0:1, :], lam_ref[1:2, :], lam_ref[2:3, :], lam_ref[3:4, :], lam_init)
            pn = pr / total
            pn_self = jnp.broadcast_to(p_self / total, (SUBLANES, LANES))
            s_ref[...] = pn - lam * pltpu.roll(pn, shift=H_DF, axis=0)
            aself_ref[...] = pn_self - lam * pltpu.roll(pn_self, shift=H_DF, axis=0)

    @pl.when(c >= nc)
    def _values():
        rows = PAGE_SIZE * H_DF
        row = lax.broadcasted_iota(jnp.int32, (SUBLANES, rows), 0)
        col = lax.broadcasted_iota(jnp.int32, (SUBLANES, rows), 1)
        own = (col % H_DF) == row
        acc = acc_ref[...]
        for p in range(pc):
            cols = page_cols((c - nc) * pc + p)
            spread = _dot(s_ref[:, cols].astype(BF16), spread_ref[...])
            acc = acc + _dot(jnp.where(own, spread, 0.0).astype(BF16), v_refs[p][...].astype(BF16))
        acc_ref[...] = acc

        @pl.when(c == 2 * nc - 1)
        def _fin():
            o = acc[0:H_DF] + _bf16_round(aself_ref[0:H_DF, 0:1]) * _bf16_round(vn_ref[...])
            o_ref[...] = _sub_norm(o, g_ref[...], 1.0 - lam_init)


def _df_decode(q, k_new, v_new, cache_kt, cache_v, layer, page_table, bias_past, bias_self, lam_vecs, subln_g,
               pc, lam_init):
    nb, n_pages = page_table.shape
    nc = n_pages // pc
    head_w = 2 * D_DF
    row = lambda w: pl.BlockSpec((None, 1, w), lambda b, c, pt: (b, 0, 0))
    heads = pl.BlockSpec((None, H_DF, head_w), lambda b, c, pt: (b, 0, 0))
    full = lambda a: pl.BlockSpec(a.shape, lambda b, c, pt: (0,) * a.ndim)

    def page_spec(p, second_pass):
        def index(b, c, pt):
            chunk = jnp.maximum(c - nc, 0) if second_pass else jnp.minimum(c, nc - 1)
            return (layer, pt[b * n_pages + chunk * pc + p], 0, 0)
        shape = (None, None, PAGE_SIZE * H_DF, head_w) if second_pass else (None, None, W_DF, PAGE_SIZE)
        return pl.BlockSpec(shape, index)

    r = lax.broadcasted_iota(jnp.int32, (PAGE_SIZE, PAGE_SIZE * H_DF), 0)
    c = lax.broadcasted_iota(jnp.int32, (PAGE_SIZE, PAGE_SIZE * H_DF), 1)
    spread = (c // H_DF == r).astype(BF16)
    grid_spec = pltpu.PrefetchScalarGridSpec(
        num_scalar_prefetch=1,
        grid=(nb, 2 * nc),
        in_specs=([row(W_DF), row(W_DF), heads, full(bias_past), full(bias_self), full(lam_vecs),
                   full(subln_g), full(spread)]
                  + [page_spec(p, False) for p in range(pc)]
                  + [page_spec(p, True) for p in range(pc)]),
        out_specs=heads,
        scratch_shapes=[pltpu.VMEM((SUBLANES, n_pages * PAGE_SIZE), F32), pltpu.VMEM((SUBLANES, 1), F32),
                        pltpu.VMEM((SUBLANES, LANES), F32), pltpu.VMEM((SUBLANES, head_w), F32)],
    )
    r3 = lambda a: a.reshape(nb, 1, W_DF)
    out = pl.pallas_call(
        functools.partial(_df_dec_kernel, pc=pc, n_pages=n_pages, lam_init=lam_init),
        grid_spec=grid_spec,
        out_shape=jax.ShapeDtypeStruct((nb, H_DF, head_w), F32),
        compiler_params=_params(("parallel", "arbitrary")),
        name="df_decode",
    )(page_table.reshape(-1), r3(q), r3(k_new), v_new.reshape(nb, H_DF, head_w), bias_past, bias_self, lam_vecs,
      subln_g, spread, *([cache_kt] * pc), *([cache_v] * pc))
    return out.reshape(nb, W_DF)


PROMPT_TM = 256
PROMPT_ATT_BLOCK = 256
PROMPT_CONV_TILE = 512
PROMPT_MOE_BLOCK = 128
COMBINE_TM = 128
SAMPLE_MOE_BLOCK = 32
DECODE_PAGES_PER_STEP = 8


def kernel(x_prompt, x_sample, cache_sb_k, cache_sb_v, cache_df_k, cache_df_v, state_conv, page_table,
           w_in, rel_bias_table, lam_q1, lam_k1, lam_q2, lam_k2, subln_g, conv_w, conv_b, conv_ln_g,
           conv_ln_b, w_sb_out, w_df_out, w_conv_out, w_o, ln1_g, ln1_b, router_w, router_bias, w_gate,
           w_up, w_down, sh_gate, sh_up, sh_down, ln2_g, ln2_b):
    depth = w_in.shape[0]
    batch, seq, _ = x_prompt.shape
    nb = x_sample.shape[0]
    n_pages = page_table.shape[1]
    past_len = n_pages * PAGE_SIZE
    alpha = (2 * depth) ** 0.25
    n_phys = cache_sb_k.shape[1]

    xp = x_prompt.reshape(batch * seq, D_MODEL)
    xs = x_sample.reshape(nb, D_MODEL)
    bias_tiles = _bias_tiles(rel_bias_table, PROMPT_ATT_BLOCK)
    dist = past_len - jnp.arange(past_len, dtype=jnp.int32)
    bias_past = jnp.tile(_bias_of_distance(rel_bias_table, dist).T, (2, 1))
    bias_self = jnp.tile(_bias_of_distance(rel_bias_table, jnp.zeros((1,), jnp.int32)).T, (2, 1))
    row = lambda a: a.reshape(1, -1)
    sb_kt = jnp.transpose(cache_sb_k, (0, 1, 3, 4, 2)).reshape(depth, n_phys, W_SB, PAGE_SIZE)
    sb_vt = jnp.transpose(cache_sb_v, (0, 1, 3, 4, 2)).reshape(depth, n_phys, W_SB, PAGE_SIZE)
    df_kt = jnp.transpose(cache_df_k, (0, 1, 3, 4, 5, 2)).reshape(depth, n_phys, W_DF, PAGE_SIZE)
    df_v = cache_df_v.reshape(depth, n_phys, PAGE_SIZE * H_DF, 2 * D_DF)

    new_p, new_s = [], []
    for l in range(depth):
        lam_init = 0.8 - 0.6 * math.exp(-0.3 * l)
        w_in_b = w_in[l].astype(BF16)
        lam_vecs = jnp.stack([lam_q1[l], lam_k1[l], lam_q2[l], lam_k2[l]])
        merge_w = (w_sb_out[l].astype(BF16), w_df_out[l].astype(BF16), w_conv_out[l].astype(BF16),
                   w_o[l].astype(BF16), row(ln1_g[l]), row(ln1_b[l]))
        ffn = (router_w[l].T.astype(BF16), router_bias[l].reshape(N_EXPERTS, 1),
               w_gate[l].astype(BF16), w_up[l].astype(BF16), w_down[l].astype(BF16),
               sh_gate[l].astype(BF16), sh_up[l].astype(BF16), sh_down[l].astype(BF16),
               row(ln2_g[l]), row(ln2_b[l]))
        conv_p = (conv_w[l], row(conv_b[l]), row(conv_ln_g[l]), row(conv_ln_b[l]))

        q_sb, k_sb, v_sb, q_df, k_df, v_df, glu, gates = _inproj(xp, w_in_b, PROMPT_TM)
        o_sb = _sb_attention(q_sb, k_sb, v_sb, batch, seq, PROMPT_ATT_BLOCK)
        o_df = _df_attention(q_df, k_df, v_df, bias_tiles, lam_vecs, row(subln_g[l]), batch, seq,
                             PROMPT_ATT_BLOCK, lam_init)
        o_c, p_conv = _conv_branch(glu, *conv_p, batch, seq, PROMPT_CONV_TILE)
        x1 = _merge(xp, o_sb, o_df, o_c, gates, *merge_w, PROMPT_TM, alpha)
        xp = _moe(x1, ffn, PROMPT_TM, PROMPT_MOE_BLOCK, alpha)
        new_p.append((k_sb, v_sb, k_df, v_df, p_conv))

        q_sb, k_sb, v_sb, q_df, k_df, v_df, glu, gates = _inproj(xs, w_in_b, nb)
        o_sb = _sb_decode(q_sb, sb_kt, sb_vt, l, page_table, DECODE_PAGES_PER_STEP)
        o_df = _df_decode(q_df, k_df, v_df, df_kt, df_v, l, page_table, bias_past, bias_self,
                          lam_vecs, row(subln_g[l]), DECODE_PAGES_PER_STEP, lam_init)
        o_c, u_new = _conv_decode(glu, jnp.transpose(state_conv[l], (1, 0, 2)), *conv_p)
        s_conv = jnp.concatenate([state_conv[l][:, 1:], u_new[:, None, :]], axis=1)
        x1 = _merge(xs, o_sb, o_df, o_c, gates, *merge_w, nb, alpha)
        xs = _moe(x1, ffn, nb, SAMPLE_MOE_BLOCK, alpha)
        new_s.append((k_sb, v_sb, k_df, v_df, s_conv))

    def stacked(rows, i, shape):
        return jnp.stack([r[i] for r in rows]).reshape((depth,) + shape)

    return (xp.reshape(batch, seq, D_MODEL),
            xs.reshape(nb, 1, D_MODEL),
            stacked(new_p, 0, (batch, seq, H_SB, D_SB)),
            stacked(new_p, 1, (batch, seq, H_SB, D_SB)),
            stacked(new_p, 2, (batch, seq, H_DF, 2, D_DF)),
            stacked(new_p, 3, (batch, seq, H_DF, 2 * D_DF)),
            stacked(new_p, 4, (batch, CONV_HIST, C_CONV)),
            stacked(new_s, 0, (nb, 1, H_SB, D_SB)),
            stacked(new_s, 1, (nb, 1, H_SB, D_SB)),
            stacked(new_s, 2, (nb, 1, H_DF, 2, D_DF)),
            stacked(new_s, 3, (nb, 1, H_DF, 2 * D_DF)),
            stacked(new_s, 4, (nb, CONV_HIST, C_CONV)))
```

```python
import functools
import math

import jax
import jax.numpy as jnp
from jax import lax
from jax.experimental import pallas as pl
from jax.experimental.pallas import tpu as pltpu

F32 = jnp.float32
BF16 = jnp.bfloat16

D_MODEL = 1024
H_SB, D_SB = 4, 64
H_DF, D_DF = 4, 64
C_CONV = D_MODEL // 4
CONV_W = 31
N_BUCKETS, MAX_DIST = 32, 128
N_EXPERTS, TOP_K, N_GROUPS, TOPK_GROUPS = 64, 8, 8, 4
GROUP_SIZE = N_EXPERTS // N_GROUPS
D_EXPERT = D_MODEL // 4
D_SHARED = D_MODEL // 4
ROUTE_SCALE = 2.5
LN_EPS = 1e-5
PAGE_SIZE = 128
W_SB = H_SB * D_SB
W_DF = H_DF * 2 * D_DF
IN_SIZES = (W_SB, W_SB, W_SB, W_DF, W_DF, W_DF, 2 * C_CONV, 3 * D_MODEL)
IN_WIDTH = sum(IN_SIZES)

LANES = 128
SUBLANES = 8
VMEM_LIMIT = 56 * 1024 * 1024
NEG_INF = float("-inf")
NT_DIMS = (((1,), (1,)), ((), ()))


def _params(semantics, vmem=VMEM_LIMIT):
    return pltpu.CompilerParams(dimension_semantics=semantics, vmem_limit_bytes=vmem)


def _dot(a, b):
    return jnp.dot(a, b, preferred_element_type=F32)


def _dot_nt(a, b):
    return lax.dot_general(a, b, NT_DIMS, preferred_element_type=F32)


def _layer_norm(y, g, b):
    mu = jnp.mean(y, axis=-1, keepdims=True)
    d = y - mu
    var = jnp.mean(d * d, axis=-1, keepdims=True)
    return d * lax.rsqrt(var + LN_EPS) * g + b


def _silu(x):
    return x * jax.nn.sigmoid(x)


def _inproj_kernel(x_ref, w_ref, *out_refs):
    x = x_ref[...].astype(BF16)
    off = 0
    for ref, width in zip(out_refs, IN_SIZES):
        for c in range(0, width, 512):
            cw = min(512, width - c)
            ref[:, c:c + cw] = _dot(x, w_ref[:, off + c:off + c + cw])
        off += width


def _inproj(x, w_bf16, tm):
    n = x.shape[0]
    return pl.pallas_call(
        _inproj_kernel,
        grid=(n // tm,),
        in_specs=[pl.BlockSpec((tm, D_MODEL), lambda i: (i, 0)),
                  pl.BlockSpec((D_MODEL, IN_WIDTH), lambda i: (0, 0), pipeline_mode=pl.Buffered(1))],
        out_specs=[pl.BlockSpec((tm, w), lambda i: (i, 0)) for w in IN_SIZES],
        out_shape=[jax.ShapeDtypeStruct((n, w), F32) for w in IN_SIZES],
        compiler_params=_params(("parallel",)),
        name="inproj",
    )(x, w_bf16)


def _causal_pairs(nq, descending):
    qi, kj = [], []
    for i in range(nq):
        ks = range(i, -1, -1) if descending else range(i + 1)
        for j in ks:
            qi.append(i)
            kj.append(j)
    return jnp.asarray(qi, jnp.int32), jnp.asarray(kj, jnp.int32)


def _suffix_matrix():
    r = lax.broadcasted_iota(jnp.int32, (LANES, LANES), 0)
    c = lax.broadcasted_iota(jnp.int32, (LANES, LANES), 1)
    u = (r > c).astype(BF16)
    return jnp.concatenate([u, jnp.ones((LANES, LANES), BF16)], axis=1)


def _split_dot(x, rhs_bf16, terms=2):
    out = None
    for _ in range(terms):
        part = x.astype(BF16)
        x = x - part.astype(F32)
        d = _dot(part, rhs_bf16)
        out = d if out is None else out + d
    return out


def _log_sigmoid_pair(z):
    sp = jnp.log1p(jnp.exp(-jnp.abs(z)))
    return jnp.minimum(z, 0.0) - sp, -jnp.maximum(z, 0.0) - sp


def _sb_kernel(qi_ref, kj_ref, q_ref, k_ref, v_ref, tri_ref, o_ref, carry_ref, acc_ref, *, tq):
    p = pl.program_id(1)
    qi = qi_ref[p]
    kj = kj_ref[p]

    @pl.when(kj == qi)
    def _init():
        carry_ref[...] = jnp.zeros_like(carry_ref)
        acc_ref[...] = jnp.zeros_like(acc_ref)

    def step(diag):
        if diag:
            row = lax.broadcasted_iota(jnp.int32, (tq, tq), 0)
            col = lax.broadcasted_iota(jnp.int32, (tq, tq), 1)
            valid = col < row
        for h in range(H_SB):
            hs = slice(h * D_SB, (h + 1) * D_SB)
            q = q_ref[:, hs].astype(BF16)
            k = k_ref[:, hs].astype(BF16)
            z = _dot_nt(q, k) * (D_SB ** -0.5)
            log_beta, log_keep = _log_sigmoid_pair(z)
            if diag:
                log_keep = jnp.where(valid, log_keep, 0.0)
            carry = carry_ref[h]
            acc = acc_ref[h]
            for c in reversed(range(tq // LANES)):
                sl = slice(c * LANES, (c + 1) * LANES)
                st = _split_dot(log_keep[:, sl], tri_ref[...])
                w = jnp.exp(log_beta[:, sl] + st[:, :LANES] + carry)
                if diag:
                    w = jnp.where(valid[:, sl], w, 0.0)
                acc = acc + _dot(w.astype(BF16), v_ref[sl, hs].astype(BF16))
                carry = carry + st[:, LANES:]
            carry_ref[h] = carry
            acc_ref[h] = acc

    @pl.when(kj == qi)
    def _diag():
        step(True)

    @pl.when(kj < qi)
    def _off():
        step(False)

    @pl.when(kj == 0)
    def _fin():
        for h in range(H_SB):
            o_ref[:, h * D_SB:(h + 1) * D_SB] = acc_ref[h]


def _sb_attention(q, k, v, batch, seq, tq):
    nq = seq // tq
    qi, kj = _causal_pairs(nq, descending=True)
    grid_spec = pltpu.PrefetchScalarGridSpec(
        num_scalar_prefetch=2,
        grid=(batch, int(qi.shape[0])),
        in_specs=[pl.BlockSpec((tq, W_SB), lambda b, p, qi, kj: (b * nq + qi[p], 0)),
                  pl.BlockSpec((tq, W_SB), lambda b, p, qi, kj: (b * nq + kj[p], 0)),
                  pl.BlockSpec((tq, W_SB), lambda b, p, qi, kj: (b * nq + kj[p], 0)),
                  pl.BlockSpec((LANES, 2 * LANES), lambda b, p, qi, kj: (0, 0))],
        out_specs=pl.BlockSpec((tq, W_SB), lambda b, p, qi, kj: (b * nq + qi[p], 0)),
        scratch_shapes=[pltpu.VMEM((H_SB, tq, LANES), F32), pltpu.VMEM((H_SB, tq, D_SB), F32)],
    )
    return pl.pallas_call(
        functools.partial(_sb_kernel, tq=tq),
        grid_spec=grid_spec,
        out_shape=jax.ShapeDtypeStruct((batch * seq, W_SB), F32),
        compiler_params=_params(("parallel", "arbitrary")),
        name="sb_attention",
    )(qi, kj, q, k, v, _suffix_matrix())


def _bias_of_distance(table, n):
    max_exact = N_BUCKETS // 2
    nf = jnp.maximum(n, 1).astype(F32)
    large = max_exact + (jnp.log(nf / max_exact) / math.log(MAX_DIST / max_exact)
                         * (N_BUCKETS - max_exact)).astype(jnp.int32)
    large = jnp.minimum(large, N_BUCKETS - 1)
    bucket = jnp.where(n < max_exact, n, large)
    out = jnp.zeros(n.shape + (H_DF,), F32)
    for b in range(N_BUCKETS):
        out = jnp.where((bucket == b)[..., None], table[b].astype(F32), out)
    return out


def _bias_tiles(table, tq):
    assert tq >= MAX_DIST, "blocks two or more behind must all fall in the last bucket"
    r = jnp.arange(tq, dtype=jnp.int32)[:, None]
    c = jnp.arange(tq, dtype=jnp.int32)[None, :]
    tiles = [_bias_of_distance(table, jnp.maximum(t * tq + r - c, 0)) for t in range(3)]
    return jnp.transpose(jnp.stack(tiles), (0, 3, 1, 2))


def _lambda(lq1, lk1, lq2, lk2, lam_init):
    return (jnp.exp(jnp.sum(lq1 * lk1, axis=-1, keepdims=True))
            - jnp.exp(jnp.sum(lq2 * lk2, axis=-1, keepdims=True)) + lam_init)


def _sub_norm(o, g, post_scale):
    ms = jnp.mean(o * o, axis=-1, keepdims=True)
    return o * lax.rsqrt(ms + LN_EPS) * g * post_scale


def _df_kernel(qi_ref, kj_ref, q_ref, k_ref, v_ref, bias_ref, lam_ref, g_ref, o_ref,
               m_ref, l_ref, acc_ref, *, tq, lam_init):
    p = pl.program_id(1)
    qi = qi_ref[p]
    kj = kj_ref[p]

    @pl.when(kj == 0)
    def _init():
        m_ref[...] = jnp.full_like(m_ref, NEG_INF)
        l_ref[...] = jnp.zeros_like(l_ref)
        acc_ref[...] = jnp.zeros_like(acc_ref)

    def step(diag):
        tile = jnp.minimum(qi - kj, 2)
        if diag:
            row = lax.broadcasted_iota(jnp.int32, (tq, tq), 0)
            col = lax.broadcasted_iota(jnp.int32, (tq, tq), 1)
            causal = col <= row
        for h in range(H_DF):
            vh = v_ref[:, h * 2 * D_DF:(h + 1) * 2 * D_DF].astype(BF16)
            bias = bias_ref[tile, h]
            for mp in range(2):
                r = 2 * h + mp
                rs = slice(r * D_DF, (r + 1) * D_DF)
                s = _dot_nt(q_ref[:, rs].astype(BF16), k_ref[:, rs].astype(BF16)) * (D_DF ** -0.5) + bias
                if diag:
                    s = jnp.where(causal, s, NEG_INF)
                m_old = m_ref[r]
                m_new = jnp.maximum(m_old, jnp.max(s, axis=-1, keepdims=True))
                pr = jnp.exp(s - m_new)
                alpha = jnp.exp(m_old - m_new)
                l_ref[r] = alpha * l_ref[r] + jnp.sum(pr, axis=-1, keepdims=True)
                acc_ref[r] = alpha * acc_ref[r] + _dot(pr.astype(BF16), vh)
                m_ref[r] = m_new

    @pl.when(kj == qi)
    def _diag():
        step(True)

    @pl.when(kj < qi)
    def _off():
        step(False)

    @pl.when(kj == qi)
    def _fin():
        lam = _lambda(lam_ref[0:1, :], lam_ref[1:2, :], lam_ref[2:3, :], lam_ref[3:4, :], lam_init)
        for h in range(H_DF):
            o = acc_ref[2 * h] / l_ref[2 * h] - lam * (acc_ref[2 * h + 1] / l_ref[2 * h + 1])
            o_ref[:, h * 2 * D_DF:(h + 1) * 2 * D_DF] = _sub_norm(o, g_ref[...], 1.0 - lam_init)


def _df_attention(q, k, v, bias_tiles, lam_vecs, subln_g, batch, seq, tq, lam_init):
    nq = seq // tq
    qi, kj = _causal_pairs(nq, descending=False)
    blk = lambda which: (lambda b, p, qi, kj: (b * nq + (qi if which == "q" else kj)[p], 0))
    grid_spec = pltpu.PrefetchScalarGridSpec(
        num_scalar_prefetch=2,
        grid=(batch, int(qi.shape[0])),
        in_specs=[pl.BlockSpec((tq, W_DF), blk("q")),
                  pl.BlockSpec((tq, W_DF), blk("k")),
                  pl.BlockSpec((tq, W_DF), blk("k")),
                  pl.BlockSpec((3, H_DF, tq, tq), lambda b, p, qi, kj: (0, 0, 0, 0)),
                  pl.BlockSpec((4, D_DF), lambda b, p, qi, kj: (0, 0)),
                  pl.BlockSpec((1, 2 * D_DF), lambda b, p, qi, kj: (0, 0))],
        out_specs=pl.BlockSpec((tq, W_DF), blk("q")),
        scratch_shapes=[pltpu.VMEM((2 * H_DF, tq, 1), F32), pltpu.VMEM((2 * H_DF, tq, 1), F32),
                        pltpu.VMEM((2 * H_DF, tq, 2 * D_DF), F32)],
    )
    return pl.pallas_call(
        functools.partial(_df_kernel, tq=tq, lam_init=lam_init),
        grid_spec=grid_spec,
        out_shape=jax.ShapeDtypeStruct((batch * seq, W_DF), F32),
        compiler_params=_params(("parallel", "arbitrary")),
        name="df_attention",
    )(qi, kj, q, k, v, bias_tiles, lam_vecs, subln_g)


CONV_HIST = CONV_W - 1
CONV_PAD = 32
CONV_CHUNK = 64


def _bf16_round(x):
    return x.astype(BF16).astype(F32)


def _conv_kernel(glu_ref, w_ref, cb_ref, g_ref, b_ref, o_ref, st_ref, ubuf_ref, tail_ref, *, tt):
    i = pl.program_id(1)

    @pl.when(i == 0)
    def _zero_history():
        ubuf_ref[0:CONV_PAD, :] = jnp.zeros((CONV_PAD, C_CONV), F32)

    @pl.when(i > 0)
    def _carry_history():
        ubuf_ref[0:CONV_PAD, :] = ubuf_ref[tt:tt + CONV_PAD, :]

    u = glu_ref[:, :C_CONV] * jax.nn.sigmoid(glu_ref[:, C_CONV:])
    ubuf_ref[CONV_PAD:CONV_PAD + tt, :] = _bf16_round(u)
    taps = [_bf16_round(w_ref[j:j + 1, :]) for j in range(CONV_W)]
    first = CONV_PAD - CONV_HIST
    for r0 in range(0, tt, CONV_CHUNK):
        acc = jnp.zeros((CONV_CHUNK, C_CONV), F32)
        for j in range(CONV_W):
            acc = acc + ubuf_ref[first + r0 + j:first + r0 + j + CONV_CHUNK, :] * taps[j]
        y = _layer_norm(acc + cb_ref[...], g_ref[...], b_ref[...])
        o_ref[r0:r0 + CONV_CHUNK, :] = _silu(y)

    @pl.when(i == pl.num_programs(1) - 1)
    def _final_state():
        tail_ref[...] = u[tt - CONV_PAD:, :]
        st_ref[...] = tail_ref[CONV_PAD - CONV_HIST:, :]


def _conv_branch(glu, conv_w, conv_b, g, b, batch, seq, tt):
    nt = seq // tt
    vec = lambda: pl.BlockSpec((1, C_CONV), lambda bb, i: (0, 0))
    return pl.pallas_call(
        functools.partial(_conv_kernel, tt=tt),
        grid=(batch, nt),
        in_specs=[pl.BlockSpec((tt, 2 * C_CONV), lambda bb, i: (bb * nt + i, 0)),
                  pl.BlockSpec((CONV_W, C_CONV), lambda bb, i: (0, 0)),
                  vec(), vec(), vec()],
        out_specs=[pl.BlockSpec((tt, C_CONV), lambda bb, i: (bb * nt + i, 0)),
                   pl.BlockSpec((None, CONV_HIST, C_CONV), lambda bb, i: (bb, 0, 0))],
        out_shape=[jax.ShapeDtypeStruct((batch * seq, C_CONV), F32),
                   jax.ShapeDtypeStruct((batch, CONV_HIST, C_CONV), F32)],
        scratch_shapes=[pltpu.VMEM((CONV_PAD + tt, C_CONV), F32), pltpu.VMEM((CONV_PAD, C_CONV), F32)],
        compiler_params=_params(("parallel", "arbitrary")),
        name="conv_branch",
    )(glu, conv_w, conv_b, g, b)


def _conv_dec_kernel(glu_ref, st_ref, w_ref, cb_ref, g_ref, b_ref, o_ref, u_ref):
    u = glu_ref[:, :C_CONV] * jax.nn.sigmoid(glu_ref[:, C_CONV:])
    acc = u * w_ref[CONV_HIST:CONV_W, :]
    for j in range(CONV_HIST):
        acc = acc + st_ref[j] * w_ref[j:j + 1, :]
    y = _layer_norm(acc + cb_ref[...], g_ref[...], b_ref[...])
    o_ref[...] = _silu(y)
    u_ref[...] = u


def _conv_decode(glu, state_t, conv_w, conv_b, g, b):
    n = glu.shape[0]
    return pl.pallas_call(
        _conv_dec_kernel,
        out_shape=[jax.ShapeDtypeStruct((n, C_CONV), F32), jax.ShapeDtypeStruct((n, C_CONV), F32)],
        name="conv_decode",
    )(glu, state_t, conv_w, conv_b, g, b)


def _merge_kernel(x_ref, osb_ref, odf_ref, oc_ref, gate_ref, wsb_ref, wdf_ref, wc_ref, wo_ref,
                  g_ref, b_ref, o_ref, *, alpha):
    merged = (jax.nn.sigmoid(gate_ref[:, 0:D_MODEL]) * _dot(osb_ref[...].astype(BF16), wsb_ref[...])
              + jax.nn.sigmoid(gate_ref[:, D_MODEL:2 * D_MODEL]) * _dot(odf_ref[...].astype(BF16), wdf_ref[...])
              + jax.nn.sigmoid(gate_ref[:, 2 * D_MODEL:]) * _dot(oc_ref[...].astype(BF16), wc_ref[...]))
    m = _dot(merged.astype(BF16), wo_ref[...])
    o_ref[...] = _layer_norm(alpha * x_ref[...] + m, g_ref[...], b_ref[...])


def _merge(x, o_sb, o_df, o_c, gates, w_sb, w_df, w_c, w_o, g, b, tm, alpha):
    n = x.shape[0]
    rows = lambda w: pl.BlockSpec((tm, w), lambda i: (i, 0))
    full = lambda a: pl.BlockSpec(a.shape, lambda i: (0, 0))
    return pl.pallas_call(
        functools.partial(_merge_kernel, alpha=alpha),
        grid=(n // tm,),
        in_specs=[rows(D_MODEL), rows(W_SB), rows(W_DF), rows(C_CONV), rows(3 * D_MODEL),
                  full(w_sb), full(w_df), full(w_c), full(w_o), full(g), full(b)],
        out_specs=rows(D_MODEL),
        out_shape=jax.ShapeDtypeStruct((n, D_MODEL), F32),
        compiler_params=_params(("parallel",)),
        name="merge_ln",
    )(x, o_sb, o_df, o_c, gates, w_sb, w_df, w_c, w_o, g, b)


def _first_argmax(vals, ids, axis, sentinel):
    mx = jnp.max(vals, axis=axis, keepdims=True)
    ix = jnp.min(jnp.where(vals == mx, ids, sentinel), axis=axis, keepdims=True)
    return mx, ix


def _router_kernel(x_ref, rw_ref, rb_ref, before_ref, idx_ref, w_ref, rank_ref, count_ref, seen_ref, *, tm):
    @pl.when(pl.program_id(0) == 0)
    def _init():
        seen_ref[...] = jnp.zeros_like(seen_ref)

    logits = _dot_nt(rw_ref[...], x_ref[...].astype(BF16))
    scores = jax.nn.sigmoid(logits)
    biased = scores + rb_ref[...]
    sub = lax.broadcasted_iota(jnp.int32, (GROUP_SIZE, tm), 0).astype(F32)
    gscore = jnp.zeros((N_GROUPS, tm), F32)
    for g in range(N_GROUPS):
        blk = biased[g * GROUP_SIZE:(g + 1) * GROUP_SIZE, :]
        m1, i1 = _first_argmax(blk, sub, 0, float(GROUP_SIZE))
        m2 = jnp.max(jnp.where(sub == i1, NEG_INF, blk), axis=0, keepdims=True)
        gscore = jnp.where(sub == float(g), m1 + m2, gscore)
    gmask = jnp.zeros((N_GROUPS, tm), F32)
    for _ in range(TOPK_GROUPS):
        _, ig = _first_argmax(gscore, sub, 0, float(N_GROUPS))
        sel = sub == ig
        gmask = jnp.where(sel, 1.0, gmask)
        gscore = jnp.where(sel, NEG_INF, gscore)
    eid = lax.broadcasted_iota(jnp.int32, (N_EXPERTS, tm), 0).astype(F32)
    cand = jnp.concatenate(
        [jnp.where(gmask[g:g + 1, :] > 0.0, biased[g * GROUP_SIZE:(g + 1) * GROUP_SIZE, :], NEG_INF)
         for g in range(N_GROUPS)], axis=0)
    total = jnp.zeros((1, tm), F32)
    picked, chosen = [], []
    member = jnp.zeros((N_EXPERTS, tm), F32)
    for k in range(TOP_K):
        _, ie = _first_argmax(cand, eid, 0, float(N_EXPERTS))
        sel = eid == ie
        wk = jnp.sum(jnp.where(sel, scores, 0.0), axis=0, keepdims=True)
        cand = jnp.where(sel, NEG_INF, cand)
        member = jnp.where(sel, 1.0, member)
        idx_ref[k:k + 1, :] = ie.astype(jnp.int32)
        picked.append(wk)
        chosen.append(ie)
        total = total + wk
    for k in range(TOP_K):
        w_ref[k:k + 1, :] = picked[k] / total * ROUTE_SCALE
    earlier = _dot(member.astype(BF16), before_ref[...]) + seen_ref[...]
    for k in range(TOP_K):
        rank_ref[k:k + 1, :] = jnp.sum(jnp.where(eid == chosen[k], earlier, 0.0), axis=0,
                                       keepdims=True).astype(jnp.int32)
    seen_ref[...] = seen_ref[...] + jnp.sum(member, axis=1, keepdims=True)
    count_ref[...] = seen_ref[...].astype(jnp.int32)


def _router(x, rw_t, rb_col, tm):
    n = x.shape[0]
    r = lax.broadcasted_iota(jnp.int32, (tm, tm), 0)
    c = lax.broadcasted_iota(jnp.int32, (tm, tm), 1)
    before = (r < c).astype(BF16)
    pairs = lambda: pl.BlockSpec((TOP_K, tm), lambda i: (0, i))
    return pl.pallas_call(
        functools.partial(_router_kernel, tm=tm),
        grid=(n // tm,),
        in_specs=[pl.BlockSpec((tm, D_MODEL), lambda i: (i, 0)),
                  pl.BlockSpec((N_EXPERTS, D_MODEL), lambda i: (0, 0)),
                  pl.BlockSpec((N_EXPERTS, 1), lambda i: (0, 0)),
                  pl.BlockSpec((tm, tm), lambda i: (0, 0))],
        out_specs=[pairs(), pairs(), pairs(), pl.BlockSpec((N_EXPERTS, 1), lambda i: (0, 0))],
        out_shape=[jax.ShapeDtypeStruct((TOP_K, n), jnp.int32),
                   jax.ShapeDtypeStruct((TOP_K, n), F32),
                   jax.ShapeDtypeStruct((TOP_K, n), jnp.int32),
                   jax.ShapeDtypeStruct((N_EXPERTS, 1), jnp.int32)],
        scratch_shapes=[pltpu.VMEM((N_EXPERTS, 1), F32)],
        compiler_params=_params(("arbitrary",)),
        name="router",
    )(x, rw_t, rb_col, before)


GATHER_UNROLL = 8


def _start_row_gather(src_hbm, row_of, dst_ref, sem, count):
    def body(j, carry):
        pltpu.make_async_copy(src_hbm.at[pl.ds(row_of(j), 1)], dst_ref.at[pl.ds(j, 1)], sem).start()
        return carry
    lax.fori_loop(0, count, body, 0, unroll=GATHER_UNROLL)


def _wait_row_gather(src_hbm, dst_ref, sem):
    pltpu.make_async_copy(src_hbm.at[pl.ds(0, dst_ref.shape[0])], dst_ref, sem).wait()


def _dispatch_tables(top_idx_t, rank_t, counts, br):
    k, n = top_idx_t.shape
    counts = counts.reshape(N_EXPERTS)
    padded = (counts + br - 1) // br * br
    pend = jnp.cumsum(padded)
    pstart = pend - padded
    experts = jnp.arange(N_EXPERTS, dtype=jnp.int32)
    dest = rank_t + jnp.sum(jnp.where(top_idx_t[:, :, None] == experts, pstart, 0), axis=-1)
    n_blocks = -(-(k * n) // br) + N_EXPERTS
    first_row = jnp.arange(n_blocks, dtype=jnp.int32)[:, None] * br
    blk_expert = jnp.minimum(jnp.sum((pend[None, :] <= first_row).astype(jnp.int32), axis=1), N_EXPERTS - 1)
    n_used = (pend[-1] // br).astype(jnp.int32).reshape(1)
    return dest.astype(jnp.int32), blk_expert.astype(jnp.int32), n_used, pend.astype(jnp.int32), n_blocks


def _dispatch_kernel(dest_ref, pend_ref, nused_ref, x_hbm, xs_hbm, zbuf, sem, zsem, *, tm, br, n_blocks):
    i = pl.program_id(0)
    rows = TOP_K * tm

    @pl.when(i == 0)
    def _zero_fill():
        zbuf[...] = jnp.zeros_like(zbuf)

        def fill(first_row):
            return pltpu.make_async_copy(zbuf, xs_hbm.at[pl.ds(pl.multiple_of(first_row, br), br)], zsem)

        def has_rows(e):
            return pend_ref[e] > (pend_ref[e - 1] if e else 0)

        for e in range(N_EXPERTS):
            @pl.when(has_rows(e))
            def _start(e=e):
                fill(pend_ref[e] - br).start()

        def start_tail(b, carry):
            fill(b * br).start()
            return carry
        lax.fori_loop(nused_ref[0], n_blocks, start_tail, 0)

        for e in range(N_EXPERTS):
            @pl.when(has_rows(e))
            def _wait(e=e):
                fill(pend_ref[e] - br).wait()

        def wait_tail(b, carry):
            fill(b * br).wait()
            return carry
        lax.fori_loop(nused_ref[0], n_blocks, wait_tail, 0)

    def wait_one_tile():
        pltpu.make_async_copy(xs_hbm.at[pl.ds(0, rows)], xs_hbm.at[pl.ds(0, rows)], sem).wait()

    for k in range(TOP_K):
        def body(j, carry, k=k):
            t = i * tm + j
            pltpu.make_async_copy(x_hbm.at[pl.ds(t, 1)], xs_hbm.at[pl.ds(dest_ref[k, t], 1)], sem).start()
            return carry
        lax.fori_loop(0, tm, body, 0, unroll=GATHER_UNROLL)

    @pl.when(i > 0)
    def _wait_previous():
        wait_one_tile()

    @pl.when(i == pl.num_programs(0) - 1)
    def _wait_last():
        wait_one_tile()


def _dispatch(x, dest, pend, n_used, n_blocks, tm, br):
    n = x.shape[0]
    grid_spec = pltpu.PrefetchScalarGridSpec(
        num_scalar_prefetch=3,
        grid=(n // tm,),
        in_specs=[pl.BlockSpec(memory_space=pl.ANY)],
        out_specs=pl.BlockSpec(memory_space=pl.ANY),
        scratch_shapes=[pltpu.VMEM((br, D_MODEL), F32), pltpu.SemaphoreType.DMA(()), pltpu.SemaphoreType.DMA(())],
    )
    return pl.pallas_call(
        functools.partial(_dispatch_kernel, tm=tm, br=br, n_blocks=n_blocks),
        grid_spec=grid_spec,
        out_shape=jax.ShapeDtypeStruct((n_blocks * br, D_MODEL), F32),
        compiler_params=_params(("arbitrary",)),
        name="dispatch",
    )(dest, pend, n_used, x)


def _expert_kernel(be_ref, nused_ref, xs_ref, wg_ref, wu_ref, wd_ref, o_ref):
    b = pl.program_id(0)

    @pl.when(b < nused_ref[0])
    def _compute():
        x = xs_ref[...].astype(BF16)
        hdn = _silu(_dot(x, wg_ref[...])) * _dot(x, wu_ref[...])
        o_ref[...] = _dot(hdn.astype(BF16), wd_ref[...])

    @pl.when(b >= nused_ref[0])
    def _unused():
        o_ref[...] = jnp.zeros_like(o_ref)


def _experts(xs, blk_expert, n_used, wg, wu, wd, br):
    n_blocks = blk_expert.shape[0]
    used = lambda b, nu: jnp.maximum(jnp.minimum(b, nu[0] - 1), 0)
    grid_spec = pltpu.PrefetchScalarGridSpec(
        num_scalar_prefetch=2,
        grid=(n_blocks,),
        in_specs=[pl.BlockSpec((br, D_MODEL), lambda b, be, nu: (used(b, nu), 0)),
                  pl.BlockSpec((None, D_MODEL, D_EXPERT), lambda b, be, nu: (be[b], 0, 0)),
                  pl.BlockSpec((None, D_MODEL, D_EXPERT), lambda b, be, nu: (be[b], 0, 0)),
                  pl.BlockSpec((None, D_EXPERT, D_MODEL), lambda b, be, nu: (be[b], 0, 0))],
        out_specs=pl.BlockSpec((br, D_MODEL), lambda b, be, nu: (b, 0)),
    )
    return pl.pallas_call(
        _expert_kernel,
        grid_spec=grid_spec,
        out_shape=jax.ShapeDtypeStruct((n_blocks * br, D_MODEL), F32),
        compiler_params=_params(("arbitrary",)),
        name="experts",
    )(blk_expert, n_used, xs, wg, wu, wd)


def _ffn_out_kernel(dest_ref, x_ref, tw_ref, out_hbm, sg_ref, su_ref, sd_ref, g_ref, b_ref, o_ref, ybuf, sem,
                    *, alpha, tm):
    i = pl.program_id(0)
    slot = i % 2

    def gather(tile, into):
        for k in range(TOP_K):
            _start_row_gather(out_hbm, lambda j, k=k: dest_ref[k, tile * tm + j], ybuf.at[into, k],
                              sem.at[into], tm)

    @pl.when(i == 0)
    def _first():
        gather(0, 0)

    @pl.when(i + 1 < pl.num_programs(0))
    def _prefetch():
        gather(i + 1, 1 - slot)

    for k in range(TOP_K):
        _wait_row_gather(out_hbm, ybuf.at[slot, k], sem.at[slot])
    x = x_ref[...]
    xb = x.astype(BF16)
    hdn = _silu(_dot(xb, sg_ref[...])) * _dot(xb, su_ref[...])
    y = ybuf[slot, 0] * tw_ref[:, 0:1]
    for k in range(1, TOP_K):
        y = y + ybuf[slot, k] * tw_ref[:, k:k + 1]
    y = y + _dot(hdn.astype(BF16), sd_ref[...])
    o_ref[...] = _layer_norm(alpha * x + y, g_ref[...], b_ref[...])


def _ffn_out(x, expert_out, dest, top_w, sg, su, sd, g, b, tm, alpha):
    n = x.shape[0]
    rows = lambda w: pl.BlockSpec((tm, w), lambda i, d: (i, 0))
    full = lambda a: pl.BlockSpec(a.shape, lambda i, d: (0, 0))
    grid_spec = pltpu.PrefetchScalarGridSpec(
        num_scalar_prefetch=1,
        grid=(n // tm,),
        in_specs=[rows(D_MODEL), rows(TOP_K), pl.BlockSpec(memory_space=pl.ANY),
                  full(sg), full(su), full(sd), full(g), full(b)],
        out_specs=rows(D_MODEL),
        scratch_shapes=[pltpu.VMEM((2, TOP_K, tm, D_MODEL), F32), pltpu.SemaphoreType.DMA((2,))],
    )
    return pl.pallas_call(
        functools.partial(_ffn_out_kernel, alpha=alpha, tm=tm),
        grid_spec=grid_spec,
        out_shape=jax.ShapeDtypeStruct((n, D_MODEL), F32),
        compiler_params=_params(("arbitrary",)),
        name="ffn_out_ln",
    )(dest, x, top_w, expert_out, sg, su, sd, g, b)


def _moe(x1, f, tm, br, alpha):
    router_w, router_b, wg, wu, wd, sg, su, sd, ln_g, ln_b = f
    n = x1.shape[0]
    top_idx_t, top_w_t, rank_t, counts = _router(x1, router_w, router_b, tm)
    dest, blk_expert, n_used, pend, n_blocks = _dispatch_tables(top_idx_t, rank_t, counts, br)
    xs = _dispatch(x1, dest, pend, n_used, n_blocks, tm, br)
    out = _experts(xs, blk_expert, n_used, wg, wu, wd, br)
    return _ffn_out(x1, out, dest, top_w_t.T, sg, su, sd, ln_g, ln_b, min(tm, COMBINE_TM), alpha)


def _sb_dec_kernel(pt_ref, q_ref, *refs, pc):
    k_refs, v_refs = refs[:pc], refs[pc:2 * pc]
    tri_ref, o_ref, carry_ref, acc_ref = refs[2 * pc:]
    c = pl.program_id(1)

    @pl.when(c == 0)
    def _init():
        carry_ref[...] = jnp.zeros_like(carry_ref)
        acc_ref[...] = jnp.zeros_like(acc_ref)

    row = lax.broadcasted_iota(jnp.int32, (SUBLANES, W_SB), 0)
    col = lax.broadcasted_iota(jnp.int32, (SUBLANES, W_SB), 1)
    own = (col // D_SB) == row
    qbd = jnp.where(own, jnp.broadcast_to(q_ref[...], (SUBLANES, W_SB)), 0.0).astype(BF16)
    carry = carry_ref[...]
    acc = acc_ref[...]
    for p in range(pc):
        z = _dot(qbd, k_refs[p][...].astype(BF16)) * (D_SB ** -0.5)
        log_beta, log_keep = _log_sigmoid_pair(z)
        st = _split_dot(log_keep, tri_ref[...], terms=3)
        w = jnp.exp(log_beta + st[:, :LANES] + carry)
        acc = acc + _dot_nt(w.astype(BF16), v_refs[p][...].astype(BF16))
        carry = carry + st[:, LANES:]
    carry_ref[...] = carry
    acc_ref[...] = acc

    @pl.when(c == pl.num_programs(1) - 1)
    def _fin():
        o_ref[...] = jnp.sum(jnp.where(own, acc, 0.0), axis=0, keepdims=True)


def _sb_decode(q, cache_kt, cache_vt, layer, page_table, pc):
    nb, n_pages = page_table.shape

    def page_spec(p):
        def index(b, c, pt):
            return (layer, pt[b * n_pages + n_pages - 1 - (c * pc + p)], 0, 0)
        return pl.BlockSpec((None, None, W_SB, PAGE_SIZE), index)

    row = lambda w: pl.BlockSpec((None, 1, w), lambda b, c, pt: (b, 0, 0))
    grid_spec = pltpu.PrefetchScalarGridSpec(
        num_scalar_prefetch=1,
        grid=(nb, n_pages // pc),
        in_specs=([row(W_SB)]
                  + [page_spec(p) for p in range(pc)]
                  + [page_spec(p) for p in range(pc)]
                  + [pl.BlockSpec((LANES, 2 * LANES), lambda b, c, pt: (0, 0))]),
        out_specs=row(W_SB),
        scratch_shapes=[pltpu.VMEM((SUBLANES, LANES), F32), pltpu.VMEM((SUBLANES, W_SB), F32)],
    )
    out = pl.pallas_call(
        functools.partial(_sb_dec_kernel, pc=pc),
        grid_spec=grid_spec,
        out_shape=jax.ShapeDtypeStruct((nb, 1, W_SB), F32),
        compiler_params=_params(("parallel", "arbitrary")),
        name="sb_decode",
    )(page_table.reshape(-1), q.reshape(nb, 1, W_SB), *([cache_kt] * pc), *([cache_vt] * pc), _suffix_matrix())
    return out.reshape(nb, W_SB)


def _df_dec_kernel(pt_ref, q_ref, kn_ref, vn_ref, bias_ref, bself_ref, lam_ref, g_ref, spread_ref, *refs,
                   pc, n_pages, lam_init):
    k_refs, v_refs = refs[:pc], refs[pc:2 * pc]
    o_ref, s_ref, m_ref, aself_ref, acc_ref = refs[2 * pc:]
    c = pl.program_id(1)
    nc = n_pages // pc

    @pl.when(c == 0)
    def _init():
        m_ref[...] = jnp.full_like(m_ref, NEG_INF)
        acc_ref[...] = jnp.zeros_like(acc_ref)

    def page_cols(page):
        return pl.ds(pl.multiple_of(page * PAGE_SIZE, PAGE_SIZE), PAGE_SIZE)

    @pl.when(c < nc)
    def _scores():
        row = lax.broadcasted_iota(jnp.int32, (SUBLANES, W_DF), 0)
        col = lax.broadcasted_iota(jnp.int32, (SUBLANES, W_DF), 1)
        own = (col // D_DF) == 2 * (row % H_DF) + row // H_DF
        qbd = jnp.where(own, jnp.broadcast_to(q_ref[...], (SUBLANES, W_DF)), 0.0).astype(BF16)
        m = m_ref[...]
        for p in range(pc):
            cols = page_cols(c * pc + p)
            s = _dot(qbd, k_refs[p][...].astype(BF16)) * (D_DF ** -0.5) + bias_ref[:, cols]
            s_ref[:, cols] = s
            m = jnp.maximum(m, jnp.max(s, axis=-1, keepdims=True))
        m_ref[...] = m

        @pl.when(c == nc - 1)
        def _weights():
            kn = jnp.broadcast_to(_bf16_round(kn_ref[...]), (SUBLANES, W_DF))
            s_self = jnp.sum(qbd.astype(F32) * kn, axis=-1, keepdims=True) * (D_DF ** -0.5) + bself_ref[...]
            m_fin = jnp.maximum(m, s_self)
            pr = jnp.exp(s_ref[...] - m_fin)
            p_self = jnp.exp(s_self - m_fin)
            total = jnp.sum(pr, axis=-1, keepdims=True) + p_self
            lam = _lambda(lam_ref[0:1, :], lam_ref[1:2, :], lam_ref[2:3, :], lam_ref[3:4, :], lam_init)
            pn = pr / total
            pn_self = jnp.broadcast_to(p_self / total, (SUBLANES, LANES))
            s_ref[...] = pn - lam * pltpu.roll(pn, shift=H_DF, axis=0)
            aself_ref[...] = pn_self - lam * pltpu.roll(pn_self, shift=H_DF, axis=0)

    @pl.when(c >= nc)
    def _values():
        rows = PAGE_SIZE * H_DF
        row = lax.broadcasted_iota(jnp.int32, (SUBLANES, rows), 0)
        col = lax.broadcasted_iota(jnp.int32, (SUBLANES, rows), 1)
        own = (col % H_DF) == row
        acc = acc_ref[...]
        for p in range(pc):
            cols = page_cols((c - nc) * pc + p)
            spread = _dot(s_ref[:, cols].astype(BF16), spread_ref[...])
            acc = acc + _dot(jnp.where(own, spread, 0.0).astype(BF16), v_refs[p][...].astype(BF16))
        acc_ref[...] = acc

        @pl.when(c == 2 * nc - 1)
        def _fin():
            o = acc[0:H_DF] + _bf16_round(aself_ref[0:H_DF, 0:1]) * _bf16_round(vn_ref[...])
            o_ref[...] = _sub_norm(o, g_ref[...], 1.0 - lam_init)


def _df_decode(q, k_new, v_new, cache_kt, cache_v, layer, page_table, bias_past, bias_self, lam_vecs, subln_g,
               pc, lam_init):
    nb, n_pages = page_table.shape
    nc = n_pages // pc
    head_w = 2 * D_DF
    row = lambda w: pl.BlockSpec((None, 1, w), lambda b, c, pt: (b, 0, 0))
    heads = pl.BlockSpec((None, H_DF, head_w), lambda b, c, pt: (b, 0, 0))
    full = lambda a: pl.BlockSpec(a.shape, lambda b, c, pt: (0,) * a.ndim)

    def page_spec(p, second_pass):
        def index(b, c, pt):
            chunk = jnp.maximum(c - nc, 0) if second_pass else jnp.minimum(c, nc - 1)
            return (layer, pt[b * n_pages + chunk * pc + p], 0, 0)
        shape = (None, None, PAGE_SIZE * H_DF, head_w) if second_pass else (None, None, W_DF, PAGE_SIZE)
        return pl.BlockSpec(shape, index)

    r = lax.broadcasted_iota(jnp.int32, (PAGE_SIZE, PAGE_SIZE * H_DF), 0)
    c = lax.broadcasted_iota(jnp.int32, (PAGE_SIZE, PAGE_SIZE * H_DF), 1)
    spread = (c // H_DF == r).astype(BF16)
    grid_spec = pltpu.PrefetchScalarGridSpec(
        num_scalar_prefetch=1,
        grid=(nb, 2 * nc),
        in_specs=([row(W_DF), row(W_DF), heads, full(bias_past), full(bias_self), full(lam_vecs),
                   full(subln_g), full(spread)]
                  + [page_spec(p, False) for p in range(pc)]
                  + [page_spec(p, True) for p in range(pc)]),
        out_specs=heads,
        scratch_shapes=[pltpu.VMEM((SUBLANES, n_pages * PAGE_SIZE), F32), pltpu.VMEM((SUBLANES, 1), F32),
                        pltpu.VMEM((SUBLANES, LANES), F32), pltpu.VMEM((SUBLANES, head_w), F32)],
    )
    r3 = lambda a: a.reshape(nb, 1, W_DF)
    out = pl.pallas_call(
        functools.partial(_df_dec_kernel, pc=pc, n_pages=n_pages, lam_init=lam_init),
        grid_spec=grid_spec,
        out_shape=jax.ShapeDtypeStruct((nb, H_DF, head_w), F32),
        compiler_params=_params(("parallel", "arbitrary")),
        name="df_decode",
    )(page_table.reshape(-1), r3(q), r3(k_new), v_new.reshape(nb, H_DF, head_w), bias_past, bias_self, lam_vecs,
      subln_g, spread, *([cache_kt] * pc), *([cache_v] * pc))
    return out.reshape(nb, W_DF)


PROMPT_TM = 256
PROMPT_ATT_BLOCK = 256
PROMPT_CONV_TILE = 512
PROMPT_MOE_BLOCK = 128
COMBINE_TM = 128
SAMPLE_MOE_BLOCK = 32
DECODE_PAGES_PER_STEP = 8


def kernel(x_prompt, x_sample, cache_sb_k, cache_sb_v, cache_df_k, cache_df_v, state_conv, page_table,
           w_in, rel_bias_table, lam_q1, lam_k1, lam_q2, lam_k2, subln_g, conv_w, conv_b, conv_ln_g,
           conv_ln_b, w_sb_out, w_df_out, w_conv_out, w_o, ln1_g, ln1_b, router_w, router_bias, w_gate,
           w_up, w_down, sh_gate, sh_up, sh_down, ln2_g, ln2_b):
    depth = w_in.shape[0]
    batch, seq, _ = x_prompt.shape
    nb = x_sample.shape[0]
    n_pages = page_table.shape[1]
    past_len = n_pages * PAGE_SIZE
    alpha = (2 * depth) ** 0.25
    n_phys = cache_sb_k.shape[1]

    xp = x_prompt.reshape(batch * seq, D_MODEL)
    xs = x_sample.reshape(nb, D_MODEL)
    bias_tiles = _bias_tiles(rel_bias_table, PROMPT_ATT_BLOCK)
    dist = past_len - jnp.arange(past_len, dtype=jnp.int32)
    bias_past = jnp.tile(_bias_of_distance(rel_bias_table, dist).T, (2, 1))
    bias_self = jnp.tile(_bias_of_distance(rel_bias_table, jnp.zeros((1,), jnp.int32)).T, (2, 1))
    row = lambda a: a.reshape(1, -1)
    sb_kt = jnp.transpose(cache_sb_k, (0, 1, 3, 4, 2)).reshape(depth, n_phys, W_SB, PAGE_SIZE)
    sb_vt = jnp.transpose(cache_sb_v, (0, 1, 3, 4, 2)).reshape(depth, n_phys, W_SB, PAGE_SIZE)
    df_kt = jnp.transpose(cache_df_k, (0, 1, 3, 4, 5, 2)).reshape(depth, n_phys, W_DF, PAGE_SIZE)
    df_v = cache_df_v.reshape(depth, n_phys, PAGE_SIZE * H_DF, 2 * D_DF)

    new_p, new_s = [], []
    for l in range(depth):
        lam_init = 0.8 - 0.6 * math.exp(-0.3 * l)
        w_in_b = w_in[l].astype(BF16)
        lam_vecs = jnp.stack([lam_q1[l], lam_k1[l], lam_q2[l], lam_k2[l]])
        merge_w = (w_sb_out[l].astype(BF16), w_df_out[l].astype(BF16), w_conv_out[l].astype(BF16),
                   w_o[l].astype(BF16), row(ln1_g[l]), row(ln1_b[l]))
        ffn = (router_w[l].T.astype(BF16), router_bias[l].reshape(N_EXPERTS, 1),
               w_gate[l].astype(BF16), w_up[l].astype(BF16), w_down[l].astype(BF16),
               sh_gate[l].astype(BF16), sh_up[l].astype(BF16), sh_down[l].astype(BF16),
               row(ln2_g[l]), row(ln2_b[l]))
        conv_p = (conv_w[l], row(conv_b[l]), row(conv_ln_g[l]), row(conv_ln_b[l]))

        q_sb, k_sb, v_sb, q_df, k_df, v_df, glu, gates = _inproj(xp, w_in_b, PROMPT_TM)
        o_sb = _sb_attention(q_sb, k_sb, v_sb, batch, seq, PROMPT_ATT_BLOCK)
        o_df = _df_attention(q_df, k_df, v_df, bias_tiles, lam_vecs, row(subln_g[l]), batch, seq,
                             PROMPT_ATT_BLOCK, lam_init)
        o_c, p_conv = _conv_branch(glu, *conv_p, batch, seq, PROMPT_CONV_TILE)
        x1 = _merge(xp, o_sb, o_df, o_c, gates, *merge_w, PROMPT_TM, alpha)
        xp = _moe(x1, ffn, PROMPT_TM, PROMPT_MOE_BLOCK, alpha)
        new_p.append((k_sb, v_sb, k_df, v_df, p_conv))

        q_sb, k_sb, v_sb, q_df, k_df, v_df, glu, gates = _inproj(xs, w_in_b, nb)
        o_sb = _sb_decode(q_sb, sb_kt, sb_vt, l, page_table, DECODE_PAGES_PER_STEP)
        o_df = _df_decode(q_df, k_df, v_df, df_kt, df_v, l, page_table, bias_past, bias_self,
                          lam_vecs, row(subln_g[l]), DECODE_PAGES_PER_STEP, lam_init)
        o_c, u_new = _conv_decode(glu, jnp.transpose(state_conv[l], (1, 0, 2)), *conv_p)
        s_conv = jnp.concatenate([state_conv[l][:, 1:], u_new[:, None, :]], axis=1)
        x1 = _merge(xs, o_sb, o_df, o_c, gates, *merge_w, nb, alpha)
        xs = _moe(x1, ffn, nb, SAMPLE_MOE_BLOCK, alpha)
        new_s.append((k_sb, v_sb, k_df, v_df, s_conv))

    def stacked(rows, i, shape):
        return jnp.stack([r[i] for r in rows]).reshape((depth,) + shape)

    return (xp.reshape(batch, seq, D_MODEL),
            xs.reshape(nb, 1, D_MODEL),
            stacked(new_p, 0, (batch, seq, H_SB, D_SB)),
            stacked(new_p, 1, (batch, seq, H_SB, D_SB)),
            stacked(new_p, 2, (batch, seq, H_DF, 2, D_DF)),
            stacked(new_p, 3, (batch, seq, H_DF, 2 * D_DF)),
            stacked(new_p, 4, (batch, CONV_HIST, C_CONV)),
            stacked(new_s, 0, (nb, 1, H_SB, D_SB)),
            stacked(new_s, 1, (nb, 1, H_SB, D_SB)),
            stacked(new_s, 2, (nb, 1, H_DF, 2, D_DF)),
            stacked(new_s, 3, (nb, 1, H_DF, 2 * D_DF)),
            stacked(new_s, 4, (nb, CONV_HIST, C_CONV)))
```

```python
import functools
import math

import jax
import jax.numpy as jnp
from jax import lax
from jax.experimental import pallas as pl
from jax.experimental.pallas import tpu as pltpu

F32 = jnp.float32
BF16 = jnp.bfloat16

D_MODEL = 1024
H_SB, D_SB = 4, 64
H_DF, D_DF = 4, 64
C_CONV = D_MODEL // 4
CONV_W = 31
N_BUCKETS, MAX_DIST = 32, 128
N_EXPERTS, TOP_K, N_GROUPS, TOPK_GROUPS = 64, 8, 8, 4
GROUP_SIZE = N_EXPERTS // N_GROUPS
D_EXPERT = D_MODEL // 4
D_SHARED = D_MODEL // 4
ROUTE_SCALE = 2.5
LN_EPS = 1e-5
PAGE_SIZE = 128
W_SB = H_SB * D_SB
W_DF = H_DF * 2 * D_DF
IN_SIZES = (W_SB, W_SB, W_SB, W_DF, W_DF, W_DF, 2 * C_CONV, 3 * D_MODEL)
IN_WIDTH = sum(IN_SIZES)

LANES = 128
SUBLANES = 8
VMEM_LIMIT = 56 * 1024 * 1024
NEG_INF = float("-inf")
NT_DIMS = (((1,), (1,)), ((), ()))
QK_SCALE = D_SB ** -0.5
assert D_SB == D_DF and QK_SCALE == 0.125


def _params(semantics, vmem=VMEM_LIMIT):
    return pltpu.CompilerParams(dimension_semantics=semantics, vmem_limit_bytes=vmem)


def _dot(a, b):
    return jnp.dot(a, b, preferred_element_type=F32)


def _dot_nt(a, b):
    return lax.dot_general(a, b, NT_DIMS, preferred_element_type=F32)


def _layer_norm(y, g, b):
    mu = jnp.mean(y, axis=-1, keepdims=True)
    d = y - mu
    var = jnp.mean(d * d, axis=-1, keepdims=True)
    return d * lax.rsqrt(var + LN_EPS) * g + b


def _silu(x):
    return x * jax.nn.sigmoid(x)


def _inproj_kernel(x_ref, w_ref, *out_refs):
    x = x_ref[...].astype(BF16)
    off = 0
    for ref, width in zip(out_refs, IN_SIZES):
        for c in range(0, width, 512):
            cw = min(512, width - c)
            ref[:, c:c + cw] = _dot(x, w_ref[:, off + c:off + c + cw])
        off += width


def _inproj(x, w_bf16, tm):
    n = x.shape[0]
    return pl.pallas_call(
        _inproj_kernel,
        grid=(n // tm,),
        in_specs=[pl.BlockSpec((tm, D_MODEL), lambda i: (i, 0)),
                  pl.BlockSpec((D_MODEL, IN_WIDTH), lambda i: (0, 0), pipeline_mode=pl.Buffered(1))],
        out_specs=[pl.BlockSpec((tm, w), lambda i: (i, 0)) for w in IN_SIZES],
        out_shape=[jax.ShapeDtypeStruct((n, w), F32) for w in IN_SIZES],
        compiler_params=_params(("parallel",)),
        name="inproj",
    )(x, w_bf16)


def _causal_pairs(nq, descending):
    qi, kj = [], []
    for i in range(nq):
        ks = range(i, -1, -1) if descending else range(i + 1)
        for j in ks:
            qi.append(i)
            kj.append(j)
    return jnp.asarray(qi, jnp.int32), jnp.asarray(kj, jnp.int32)


def _suffix_matrix():
    r = lax.broadcasted_iota(jnp.int32, (LANES, LANES), 0)
    c = lax.broadcasted_iota(jnp.int32, (LANES, LANES), 1)
    u = (r > c).astype(BF16)
    return jnp.concatenate([u, jnp.ones((LANES, LANES), BF16)], axis=1)


def _split_dot(x, rhs_bf16, terms=2):
    out = None
    for _ in range(terms):
        part = x.astype(BF16)
        x = x - part.astype(F32)
        d = _dot(part, rhs_bf16)
        out = d if out is None else out + d
    return out


def _log_sigmoid_pair(z):
    sp = jnp.log1p(jnp.exp(-jnp.abs(z)))
    return jnp.minimum(z, 0.0) - sp, -jnp.maximum(z, 0.0) - sp


EXP_UNDERFLOW = -104.0


def _sb_kernel(qi_ref, kj_ref, q_ref, k_ref, v_ref, tri_ref, o_ref, carry_ref, acc_ref, live_ref, *, tq):
    p = pl.program_id(1)
    qi = qi_ref[p]
    kj = kj_ref[p]

    @pl.when(kj == qi)
    def _init():
        carry_ref[...] = jnp.zeros_like(carry_ref)
        acc_ref[...] = jnp.zeros_like(acc_ref)
        for h in range(H_SB):
            live_ref[h] = 1

    def head_step(h, diag):
        hs = slice(h * D_SB, (h + 1) * D_SB)
        q = (q_ref[:, hs] * QK_SCALE).astype(BF16)
        k = k_ref[:, hs].astype(BF16)
        z = _dot_nt(q, k)
        log_beta, log_keep = _log_sigmoid_pair(z)
        if diag:
            row = lax.broadcasted_iota(jnp.int32, (tq, tq), 0)
            col = lax.broadcasted_iota(jnp.int32, (tq, tq), 1)
            valid = col < row
            log_keep = jnp.where(valid, log_keep, 0.0)
        carry = carry_ref[h]
        acc = acc_ref[h]
        for c in reversed(range(tq // LANES)):
            sl = slice(c * LANES, (c + 1) * LANES)
            st = _split_dot(log_keep[:, sl], tri_ref[...])
            w = jnp.exp(log_beta[:, sl] + st[:, :LANES] + carry)
            if diag:
                w = jnp.where(valid[:, sl], w, 0.0)
            acc = acc + _dot(w.astype(BF16), v_ref[sl, hs].astype(BF16))
            carry = carry + st[:, LANES:]
        carry_ref[h] = carry
        acc_ref[h] = acc
        live_ref[h] = (jnp.max(carry) >= EXP_UNDERFLOW).astype(jnp.int32)

    def step(diag):
        for h in range(H_SB):
            @pl.when(live_ref[h] == 1)
            def _head(h=h):
                head_step(h, diag)

    @pl.when(kj == qi)
    def _diag():
        step(True)

    @pl.when(kj < qi)
    def _off():
        step(False)

    @pl.when(kj == 0)
    def _fin():
        for h in range(H_SB):
            o_ref[:, h * D_SB:(h + 1) * D_SB] = acc_ref[h]


def _sb_attention(q, k, v, batch, seq, tq):
    nq = seq // tq
    qi, kj = _causal_pairs(nq, descending=True)
    grid_spec = pltpu.PrefetchScalarGridSpec(
        num_scalar_prefetch=2,
        grid=(batch, int(qi.shape[0])),
        in_specs=[pl.BlockSpec((tq, W_SB), lambda b, p, qi, kj: (b * nq + qi[p], 0)),
                  pl.BlockSpec((tq, W_SB), lambda b, p, qi, kj: (b * nq + kj[p], 0)),
                  pl.BlockSpec((tq, W_SB), lambda b, p, qi, kj: (b * nq + kj[p], 0)),
                  pl.BlockSpec((LANES, 2 * LANES), lambda b, p, qi, kj: (0, 0))],
        out_specs=pl.BlockSpec((tq, W_SB), lambda b, p, qi, kj: (b * nq + qi[p], 0)),
        scratch_shapes=[pltpu.VMEM((H_SB, tq, LANES), F32), pltpu.VMEM((H_SB, tq, D_SB), F32),
                        pltpu.SMEM((H_SB,), jnp.int32)],
    )
    return pl.pallas_call(
        functools.partial(_sb_kernel, tq=tq),
        grid_spec=grid_spec,
        out_shape=jax.ShapeDtypeStruct((batch * seq, W_SB), F32),
        compiler_params=_params(("parallel", "arbitrary")),
        name="sb_attention",
    )(qi, kj, q, k, v, _suffix_matrix())


def _bias_of_distance(table, n):
    max_exact = N_BUCKETS // 2
    nf = jnp.maximum(n, 1).astype(F32)
    large = max_exact + (jnp.log(nf / max_exact) / math.log(MAX_DIST / max_exact)
                         * (N_BUCKETS - max_exact)).astype(jnp.int32)
    large = jnp.minimum(large, N_BUCKETS - 1)
    bucket = jnp.where(n < max_exact, n, large)
    out = jnp.zeros(n.shape + (H_DF,), F32)
    for b in range(N_BUCKETS):
        out = jnp.where((bucket == b)[..., None], table[b].astype(F32), out)
    return out


def _bias_tiles(table, tq):
    assert tq >= MAX_DIST, "blocks two or more behind must all fall in the last bucket"
    r = jnp.arange(tq, dtype=jnp.int32)[:, None]
    c = jnp.arange(tq, dtype=jnp.int32)[None, :]
    tiles = [_bias_of_distance(table, jnp.maximum(t * tq + r - c, 0)) for t in range(3)]
    return jnp.transpose(jnp.stack(tiles), (0, 3, 1, 2))


def _lambda(lq1, lk1, lq2, lk2, lam_init):
    return (jnp.exp(jnp.sum(lq1 * lk1, axis=-1, keepdims=True))
            - jnp.exp(jnp.sum(lq2 * lk2, axis=-1, keepdims=True)) + lam_init)


def _sub_norm(o, g, post_scale):
    ms = jnp.mean(o * o, axis=-1, keepdims=True)
    return o * lax.rsqrt(ms + LN_EPS) * g * post_scale


def _df_kernel(qi_ref, kj_ref, q_ref, k_ref, v_ref, bias_ref, lam_ref, g_ref, o_ref,
               m_ref, l_ref, acc_ref, *, tq, lam_init):
    p = pl.program_id(1)
    qi = qi_ref[p]
    kj = kj_ref[p]

    @pl.when(kj == 0)
    def _init():
        m_ref[...] = jnp.full_like(m_ref, NEG_INF)
        l_ref[...] = jnp.zeros_like(l_ref)
        acc_ref[...] = jnp.zeros_like(acc_ref)

    def step(diag):
        tile = jnp.minimum(qi - kj, 2)
        if diag:
            row = lax.broadcasted_iota(jnp.int32, (tq, tq), 0)
            col = lax.broadcasted_iota(jnp.int32, (tq, tq), 1)
            causal = col <= row
        for h in range(H_DF):
            vh = v_ref[:, h * 2 * D_DF:(h + 1) * 2 * D_DF].astype(BF16)
            bias = bias_ref[tile, h]
            for mp in range(2):
                r = 2 * h + mp
                rs = slice(r * D_DF, (r + 1) * D_DF)
                s = _dot_nt((q_ref[:, rs] * QK_SCALE).astype(BF16), k_ref[:, rs].astype(BF16)) + bias
                if diag:
                    s = jnp.where(causal, s, NEG_INF)
                m_old = m_ref[r]
                m_new = jnp.maximum(m_old, jnp.max(s, axis=-1, keepdims=True))
                pr = jnp.exp(s - m_new)
                alpha = jnp.exp(m_old - m_new)
                l_ref[r] = alpha * l_ref[r] + jnp.sum(pr, axis=-1, keepdims=True)
                acc_ref[r] = alpha * acc_ref[r] + _dot(pr.astype(BF16), vh)
                m_ref[r] = m_new

    @pl.when(kj == qi)
    def _diag():
        step(True)

    @pl.when(kj < qi)
    def _off():
        step(False)

    @pl.when(kj == qi)
    def _fin():
        lam = _lambda(lam_ref[0:1, :], lam_ref[1:2, :], lam_ref[2:3, :], lam_ref[3:4, :], lam_init)
        for h in range(H_DF):
            o = acc_ref[2 * h] / l_ref[2 * h] - lam * (acc_ref[2 * h + 1] / l_ref[2 * h + 1])
            o_ref[:, h * 2 * D_DF:(h + 1) * 2 * D_DF] = _sub_norm(o, g_ref[...], 1.0 - lam_init)


def _df_attention(q, k, v, bias_tiles, lam_vecs, subln_g, batch, seq, tq, lam_init):
    nq = seq // tq
    qi, kj = _causal_pairs(nq, descending=False)
    blk = lambda which: (lambda b, p, qi, kj: (b * nq + (qi if which == "q" else kj)[p], 0))
    grid_spec = pltpu.PrefetchScalarGridSpec(
        num_scalar_prefetch=2,
        grid=(batch, int(qi.shape[0])),
        in_specs=[pl.BlockSpec((tq, W_DF), blk("q")),
                  pl.BlockSpec((tq, W_DF), blk("k")),
                  pl.BlockSpec((tq, W_DF), blk("k")),
                  pl.BlockSpec((3, H_DF, tq, tq), lambda b, p, qi, kj: (0, 0, 0, 0)),
                  pl.BlockSpec((4, D_DF), lambda b, p, qi, kj: (0, 0)),
                  pl.BlockSpec((1, 2 * D_DF), lambda b, p, qi, kj: (0, 0))],
        out_specs=pl.BlockSpec((tq, W_DF), blk("q")),
        scratch_shapes=[pltpu.VMEM((2 * H_DF, tq, 1), F32), pltpu.VMEM((2 * H_DF, tq, 1), F32),
                        pltpu.VMEM((2 * H_DF, tq, 2 * D_DF), F32)],
    )
    return pl.pallas_call(
        functools.partial(_df_kernel, tq=tq, lam_init=lam_init),
        grid_spec=grid_spec,
        out_shape=jax.ShapeDtypeStruct((batch * seq, W_DF), F32),
        compiler_params=_params(("parallel", "arbitrary")),
        name="df_attention",
    )(qi, kj, q, k, v, bias_tiles, lam_vecs, subln_g)


CONV_HIST = CONV_W - 1
CONV_PAD = 32
CONV_CHUNK = 64


def _bf16_round(x):
    return x.astype(BF16).astype(F32)


def _conv_kernel(glu_ref, w_ref, cb_ref, g_ref, b_ref, o_ref, st_ref, ubuf_ref, tail_ref, *, tt):
    i = pl.program_id(1)

    @pl.when(i == 0)
    def _zero_history():
        ubuf_ref[0:CONV_PAD, :] = jnp.zeros((CONV_PAD, C_CONV), F32)

    @pl.when(i > 0)
    def _carry_history():
        ubuf_ref[0:CONV_PAD, :] = ubuf_ref[tt:tt + CONV_PAD, :]

    u = glu_ref[:, :C_CONV] * jax.nn.sigmoid(glu_ref[:, C_CONV:])
    ubuf_ref[CONV_PAD:CONV_PAD + tt, :] = _bf16_round(u)
    taps = [_bf16_round(w_ref[j:j + 1, :]) for j in range(CONV_W)]
    first = CONV_PAD - CONV_HIST
    for r0 in range(0, tt, CONV_CHUNK):
        acc = jnp.zeros((CONV_CHUNK, C_CONV), F32)
        for j in range(CONV_W):
            acc = acc + ubuf_ref[first + r0 + j:first + r0 + j + CONV_CHUNK, :] * taps[j]
        y = _layer_norm(acc + cb_ref[...], g_ref[...], b_ref[...])
        o_ref[r0:r0 + CONV_CHUNK, :] = _silu(y)

    @pl.when(i == pl.num_programs(1) - 1)
    def _final_state():
        tail_ref[...] = u[tt - CONV_PAD:, :]
        st_ref[...] = tail_ref[CONV_PAD - CONV_HIST:, :]


def _conv_branch(glu, conv_w, conv_b, g, b, batch, seq, tt):
    nt = seq // tt
    vec = lambda: pl.BlockSpec((1, C_CONV), lambda bb, i: (0, 0))
    return pl.pallas_call(
        functools.partial(_conv_kernel, tt=tt),
        grid=(batch, nt),
        in_specs=[pl.BlockSpec((tt, 2 * C_CONV), lambda bb, i: (bb * nt + i, 0)),
                  pl.BlockSpec((CONV_W, C_CONV), lambda bb, i: (0, 0)),
                  vec(), vec(), vec()],
        out_specs=[pl.BlockSpec((tt, C_CONV), lambda bb, i: (bb * nt + i, 0)),
                   pl.BlockSpec((None, CONV_HIST, C_CONV), lambda bb, i: (bb, 0, 0))],
        out_shape=[jax.ShapeDtypeStruct((batch * seq, C_CONV), F32),
                   jax.ShapeDtypeStruct((batch, CONV_HIST, C_CONV), F32)],
        scratch_shapes=[pltpu.VMEM((CONV_PAD + tt, C_CONV), F32), pltpu.VMEM((CONV_PAD, C_CONV), F32)],
        compiler_params=_params(("parallel", "arbitrary")),
        name="conv_branch",
    )(glu, conv_w, conv_b, g, b)


def _conv_dec_kernel(glu_ref, st_ref, w_ref, cb_ref, g_ref, b_ref, o_ref, u_ref):
    u = glu_ref[:, :C_CONV] * jax.nn.sigmoid(glu_ref[:, C_CONV:])
    acc = u * w_ref[CONV_HIST:CONV_W, :]
    for j in range(CONV_HIST):
        acc = acc + st_ref[j] * w_ref[j:j + 1, :]
    y = _layer_norm(acc + cb_ref[...], g_ref[...], b_ref[...])
    o_ref[...] = _silu(y)
    u_ref[...] = u


def _conv_decode(glu, state_t, conv_w, conv_b, g, b):
    n = glu.shape[0]
    return pl.pallas_call(
        _conv_dec_kernel,
        out_shape=[jax.ShapeDtypeStruct((n, C_CONV), F32), jax.ShapeDtypeStruct((n, C_CONV), F32)],
        name="conv_decode",
    )(glu, state_t, conv_w, conv_b, g, b)


def _merge_kernel(x_ref, osb_ref, odf_ref, oc_ref, gate_ref, wsb_ref, wdf_ref, wc_ref, wo_ref,
                  g_ref, b_ref, o_ref, *, alpha):
    merged = (jax.nn.sigmoid(gate_ref[:, 0:D_MODEL]) * _dot(osb_ref[...].astype(BF16), wsb_ref[...])
              + jax.nn.sigmoid(gate_ref[:, D_MODEL:2 * D_MODEL]) * _dot(odf_ref[...].astype(BF16), wdf_ref[...])
              + jax.nn.sigmoid(gate_ref[:, 2 * D_MODEL:]) * _dot(oc_ref[...].astype(BF16), wc_ref[...]))
    m = _dot(merged.astype(BF16), wo_ref[...])
    o_ref[...] = _layer_norm(alpha * x_ref[...] + m, g_ref[...], b_ref[...])


def _merge(x, o_sb, o_df, o_c, gates, w_sb, w_df, w_c, w_o, g, b, tm, alpha):
    n = x.shape[0]
    rows = lambda w: pl.BlockSpec((tm, w), lambda i: (i, 0))
    full = lambda a: pl.BlockSpec(a.shape, lambda i: (0, 0))
    return pl.pallas_call(
        functools.partial(_merge_kernel, alpha=alpha),
        grid=(n // tm,),
        in_specs=[rows(D_MODEL), rows(W_SB), rows(W_DF), rows(C_CONV), rows(3 * D_MODEL),
                  full(w_sb), full(w_df), full(w_c), full(w_o), full(g), full(b)],
        out_specs=rows(D_MODEL),
        out_shape=jax.ShapeDtypeStruct((n, D_MODEL), F32),
        compiler_params=_params(("parallel",)),
        name="merge_ln",
    )(x, o_sb, o_df, o_c, gates, w_sb, w_df, w_c, w_o, g, b)


def _first_argmax(vals, ids, axis, sentinel):
    mx = jnp.max(vals, axis=axis, keepdims=True)
    ix = jnp.min(jnp.where(vals == mx, ids, sentinel), axis=axis, keepdims=True)
    return mx, ix


def _router_kernel(x_ref, rw_ref, rb_ref, before_ref, idx_ref, w_ref, rank_ref, count_ref, seen_ref, *, tm):
    @pl.when(pl.program_id(0) == 0)
    def _init():
        seen_ref[...] = jnp.zeros_like(seen_ref)

    logits = _dot_nt(rw_ref[...], x_ref[...].astype(BF16))
    scores = jax.nn.sigmoid(logits)
    biased = scores + rb_ref[...]
    sub = lax.broadcasted_iota(jnp.int32, (GROUP_SIZE, tm), 0).astype(F32)
    gscore = jnp.zeros((N_GROUPS, tm), F32)
    for g in range(N_GROUPS):
        blk = biased[g * GROUP_SIZE:(g + 1) * GROUP_SIZE, :]
        m1, i1 = _first_argmax(blk, sub, 0, float(GROUP_SIZE))
        m2 = jnp.max(jnp.where(sub == i1, NEG_INF, blk), axis=0, keepdims=True)
        gscore = jnp.where(sub == float(g), m1 + m2, gscore)
    gmask = jnp.zeros((N_GROUPS, tm), F32)
    for _ in range(TOPK_GROUPS):
        _, ig = _first_argmax(gscore, sub, 0, float(N_GROUPS))
        sel = sub == ig
        gmask = jnp.where(sel, 1.0, gmask)
        gscore = jnp.where(sel, NEG_INF, gscore)
    eid = lax.broadcasted_iota(jnp.int32, (N_EXPERTS, tm), 0).astype(F32)
    cand = jnp.concatenate(
        [jnp.where(gmask[g:g + 1, :] > 0.0, biased[g * GROUP_SIZE:(g + 1) * GROUP_SIZE, :], NEG_INF)
         for g in range(N_GROUPS)], axis=0)
    total = jnp.zeros((1, tm), F32)
    picked, chosen = [], []
    member = jnp.zeros((N_EXPERTS, tm), F32)
    for k in range(TOP_K):
        _, ie = _first_argmax(cand, eid, 0, float(N_EXPERTS))
        sel = eid == ie
        wk = jnp.sum(jnp.where(sel, scores, 0.0), axis=0, keepdims=True)
        cand = jnp.where(sel, NEG_INF, cand)
        member = jnp.where(sel, 1.0, member)
        idx_ref[k:k + 1, :] = ie.astype(jnp.int32)
        picked.append(wk)
        chosen.append(ie)
        total = total + wk
    for k in range(TOP_K):
        w_ref[k:k + 1, :] = picked[k] / total * ROUTE_SCALE
    earlier = _dot(member.astype(BF16), before_ref[...]) + seen_ref[...]
    for k in range(TOP_K):
        rank_ref[k:k + 1, :] = jnp.sum(jnp.where(eid == chosen[k], earlier, 0.0), axis=0,
                                       keepdims=True).astype(jnp.int32)
    seen_ref[...] = seen_ref[...] + jnp.sum(member, axis=1, keepdims=True)
    count_ref[...] = seen_ref[...].astype(jnp.int32)


def _router(x, rw_t, rb_col, tm):
    n = x.shape[0]
    r = lax.broadcasted_iota(jnp.int32, (tm, tm), 0)
    c = lax.broadcasted_iota(jnp.int32, (tm, tm), 1)
    before = (r < c).astype(BF16)
    pairs = lambda: pl.BlockSpec((TOP_K, tm), lambda i: (0, i))
    return pl.pallas_call(
        functools.partial(_router_kernel, tm=tm),
        grid=(n // tm,),
        in_specs=[pl.BlockSpec((tm, D_MODEL), lambda i: (i, 0)),
                  pl.BlockSpec((N_EXPERTS, D_MODEL), lambda i: (0, 0)),
                  pl.BlockSpec((N_EXPERTS, 1), lambda i: (0, 0)),
                  pl.BlockSpec((tm, tm), lambda i: (0, 0))],
        out_specs=[pairs(), pairs(), pairs(), pl.BlockSpec((N_EXPERTS, 1), lambda i: (0, 0))],
        out_shape=[jax.ShapeDtypeStruct((TOP_K, n), jnp.int32),
                   jax.ShapeDtypeStruct((TOP_K, n), F32),
                   jax.ShapeDtypeStruct((TOP_K, n), jnp.int32),
                   jax.ShapeDtypeStruct((N_EXPERTS, 1), jnp.int32)],
        scratch_shapes=[pltpu.VMEM((N_EXPERTS, 1), F32)],
        compiler_params=_params(("arbitrary",)),
        name="router",
    )(x, rw_t, rb_col, before)


GATHER_UNROLL = 8


def _start_row_gather(src_hbm, row_of, dst_ref, sem, count):
    def body(j, carry):
        pltpu.make_async_copy(src_hbm.at[pl.ds(row_of(j), 1)], dst_ref.at[pl.ds(j, 1)], sem).start()
        return carry
    lax.fori_loop(0, count, body, 0, unroll=GATHER_UNROLL)


def _wait_row_gather(src_hbm, dst_ref, sem):
    pltpu.make_async_copy(src_hbm.at[pl.ds(0, dst_ref.shape[0])], dst_ref, sem).wait()


def _dispatch_tables(top_idx_t, rank_t, counts, br):
    k, n = top_idx_t.shape
    counts = counts.reshape(N_EXPERTS)
    padded = (counts + br - 1) // br * br
    pend = jnp.cumsum(padded)
    pstart = pend - padded
    experts = jnp.arange(N_EXPERTS, dtype=jnp.int32)
    dest = rank_t + jnp.sum(jnp.where(top_idx_t[:, :, None] == experts, pstart, 0), axis=-1)
    n_blocks = -(-(k * n) // br) + N_EXPERTS
    first_row = jnp.arange(n_blocks, dtype=jnp.int32)[:, None] * br
    blk_expert = jnp.minimum(jnp.sum((pend[None, :] <= first_row).astype(jnp.int32), axis=1), N_EXPERTS - 1)
    n_used = (pend[-1] // br).astype(jnp.int32).reshape(1)
    return dest.astype(jnp.int32), blk_expert.astype(jnp.int32), n_used, pend.astype(jnp.int32), n_blocks


def _dispatch_kernel(dest_ref, pend_ref, nused_ref, x_ref, xs_hbm, zbuf, sem, zsem, *, tm, br, n_blocks):
    i = pl.program_id(0)
    rows = TOP_K * tm

    @pl.when(i == 0)
    def _zero_fill():
        zbuf[...] = jnp.zeros_like(zbuf)

        def fill(first_row):
            return pltpu.make_async_copy(zbuf, xs_hbm.at[pl.ds(pl.multiple_of(first_row, br), br)], zsem)

        def has_rows(e):
            return pend_ref[e] > (pend_ref[e - 1] if e else 0)

        for e in range(N_EXPERTS):
            @pl.when(has_rows(e))
            def _start(e=e):
                fill(pend_ref[e] - br).start()

        def start_tail(b, carry):
            fill(b * br).start()
            return carry
        lax.fori_loop(nused_ref[0], n_blocks, start_tail, 0)

        for e in range(N_EXPERTS):
            @pl.when(has_rows(e))
            def _wait(e=e):
                fill(pend_ref[e] - br).wait()

        def wait_tail(b, carry):
            fill(b * br).wait()
            return carry
        lax.fori_loop(nused_ref[0], n_blocks, wait_tail, 0)

    def wait_one_tile():
        pltpu.make_async_copy(xs_hbm.at[pl.ds(0, rows)], xs_hbm.at[pl.ds(0, rows)], sem).wait()

    for k in range(TOP_K):
        def body(j, carry, k=k):
            row = dest_ref[k, i * tm + j]
            pltpu.make_async_copy(x_ref.at[pl.ds(j, 1)], xs_hbm.at[pl.ds(row, 1)], sem).start()
            return carry
        lax.fori_loop(0, tm, body, 0, unroll=GATHER_UNROLL)

    wait_one_tile()


def _dispatch(x, dest, pend, n_used, n_blocks, tm, br):
    n = x.shape[0]
    grid_spec = pltpu.PrefetchScalarGridSpec(
        num_scalar_prefetch=3,
        grid=(n // tm,),
        in_specs=[pl.BlockSpec((tm, D_MODEL), lambda i, d, pe, nu: (i, 0))],
        out_specs=pl.BlockSpec(memory_space=pl.ANY),
        scratch_shapes=[pltpu.VMEM((br, D_MODEL), F32), pltpu.SemaphoreType.DMA(()), pltpu.SemaphoreType.DMA(())],
    )
    return pl.pallas_call(
        functools.partial(_dispatch_kernel, tm=tm, br=br, n_blocks=n_blocks),
        grid_spec=grid_spec,
        out_shape=jax.ShapeDtypeStruct((n_blocks * br, D_MODEL), F32),
        compiler_params=_params(("arbitrary",)),
        name="dispatch",
    )(dest, pend, n_used, x)


def _expert_kernel(be_ref, nused_ref, xs_ref, wg_ref, wu_ref, wd_ref, o_ref):
    b = pl.program_id(0)

    @pl.when(b < nused_ref[0])
    def _compute():
        x = xs_ref[...].astype(BF16)
        hdn = _silu(_dot(x, wg_ref[...])) * _dot(x, wu_ref[...])
        o_ref[...] = _dot(hdn.astype(BF16), wd_ref[...])

    @pl.when(b >= nused_ref[0])
    def _unused():
        o_ref[...] = jnp.zeros_like(o_ref)


def _experts(xs, blk_expert, n_used, wg, wu, wd, br):
    n_blocks = blk_expert.shape[0]
    used = lambda b, nu: jnp.maximum(jnp.minimum(b, nu[0] - 1), 0)
    grid_spec = pltpu.PrefetchScalarGridSpec(
        num_scalar_prefetch=2,
        grid=(n_blocks,),
        in_specs=[pl.BlockSpec((br, D_MODEL), lambda b, be, nu: (used(b, nu), 0)),
                  pl.BlockSpec((None, D_MODEL, D_EXPERT), lambda b, be, nu: (be[b], 0, 0)),
                  pl.BlockSpec((None, D_MODEL, D_EXPERT), lambda b, be, nu: (be[b], 0, 0)),
                  pl.BlockSpec((None, D_EXPERT, D_MODEL), lambda b, be, nu: (be[b], 0, 0))],
        out_specs=pl.BlockSpec((br, D_MODEL), lambda b, be, nu: (b, 0)),
    )
    return pl.pallas_call(
        _expert_kernel,
        grid_spec=grid_spec,
        out_shape=jax.ShapeDtypeStruct((n_blocks * br, D_MODEL), F32),
        compiler_params=_params(("arbitrary",)),
        name="experts",
    )(blk_expert, n_used, xs, wg, wu, wd)


def _ffn_out_kernel(dest_ref, x_ref, tw_ref, out_hbm, sg_ref, su_ref, sd_ref, g_ref, b_ref, o_ref, ybuf, sem,
                    *, alpha, tm):
    i = pl.program_id(0)
    slot = i % 2

    def gather(tile, into):
        for k in range(TOP_K):
            _start_row_gather(out_hbm, lambda j, k=k: dest_ref[k, tile * tm + j], ybuf.at[into, k],
                              sem.at[into], tm)

    @pl.when(i == 0)
    def _first():
        gather(0, 0)

    @pl.when(i + 1 < pl.num_programs(0))
    def _prefetch():
        gather(i + 1, 1 - slot)

    for k in range(TOP_K):
        _wait_row_gather(out_hbm, ybuf.at[slot, k], sem.at[slot])
    x = x_ref[...]
    xb = x.astype(BF16)
    hdn = _silu(_dot(xb, sg_ref[...])) * _dot(xb, su_ref[...])
    y = ybuf[slot, 0] * tw_ref[:, 0:1]
    for k in range(1, TOP_K):
        y = y + ybuf[slot, k] * tw_ref[:, k:k + 1]
    y = y + _dot(hdn.astype(BF16), sd_ref[...])
    o_ref[...] = _layer_norm(alpha * x + y, g_ref[...], b_ref[...])


def _ffn_out(x, expert_out, dest, top_w, sg, su, sd, g, b, tm, alpha):
    n = x.shape[0]
    rows = lambda w: pl.BlockSpec((tm, w), lambda i, d: (i, 0))
    full = lambda a: pl.BlockSpec(a.shape, lambda i, d: (0, 0))
    grid_spec = pltpu.PrefetchScalarGridSpec(
        num_scalar_prefetch=1,
        grid=(n // tm,),
        in_specs=[rows(D_MODEL), rows(TOP_K), pl.BlockSpec(memory_space=pl.ANY),
                  full(sg), full(su), full(sd), full(g), full(b)],
        out_specs=rows(D_MODEL),
        scratch_shapes=[pltpu.VMEM((2, TOP_K, tm, D_MODEL), F32), pltpu.SemaphoreType.DMA((2,))],
    )
    return pl.pallas_call(
        functools.partial(_ffn_out_kernel, alpha=alpha, tm=tm),
        grid_spec=grid_spec,
        out_shape=jax.ShapeDtypeStruct((n, D_MODEL), F32),
        compiler_params=_params(("arbitrary",)),
        name="ffn_out_ln",
    )(dest, x, top_w, expert_out, sg, su, sd, g, b)


def _moe(x1, f, tm, br, alpha):
    router_w, router_b, wg, wu, wd, sg, su, sd, ln_g, ln_b = f
    n = x1.shape[0]
    top_idx_t, top_w_t, rank_t, counts = _router(x1, router_w, router_b, tm)
    dest, blk_expert, n_used, pend, n_blocks = _dispatch_tables(top_idx_t, rank_t, counts, br)
    xs = _dispatch(x1, dest, pend, n_used, n_blocks, tm, br)
    out = _experts(xs, blk_expert, n_used, wg, wu, wd, br)
    return _ffn_out(x1, out, dest, top_w_t.T, sg, su, sd, ln_g, ln_b, min(tm, COMBINE_TM), alpha)


def _sb_dec_kernel(pt_ref, q_ref, *refs, pc):
    k_refs, v_refs = refs[:pc], refs[pc:2 * pc]
    tri_ref, o_ref, carry_ref, acc_ref = refs[2 * pc:]
    c = pl.program_id(1)

    @pl.when(c == 0)
    def _init():
        carry_ref[...] = jnp.zeros_like(carry_ref)
        acc_ref[...] = jnp.zeros_like(acc_ref)

    row = lax.broadcasted_iota(jnp.int32, (SUBLANES, W_SB), 0)
    col = lax.broadcasted_iota(jnp.int32, (SUBLANES, W_SB), 1)
    own = (col // D_SB) == row
    qbd = jnp.where(own, jnp.broadcast_to(q_ref[...], (SUBLANES, W_SB)), 0.0).astype(BF16)
    carry = carry_ref[...]
    acc = acc_ref[...]
    for p in range(pc):
        z = _dot(qbd, k_refs[p][...].astype(BF16)) * (D_SB ** -0.5)
        log_beta, log_keep = _log_sigmoid_pair(z)
        st = _split_dot(log_keep, tri_ref[...], terms=3)
        w = jnp.exp(log_beta + st[:, :LANES] + carry)
        acc = acc + _dot_nt(w.astype(BF16), v_refs[p][...].astype(BF16))
        carry = carry + st[:, LANES:]
    carry_ref[...] = carry
    acc_ref[...] = acc

    @pl.when(c == pl.num_programs(1) - 1)
    def _fin():
        o_ref[...] = jnp.sum(jnp.where(own, acc, 0.0), axis=0, keepdims=True)


def _sb_decode(q, cache_kt, cache_vt, layer, page_table, pc):
    nb, n_pages = page_table.shape

    def page_spec(p):
        def index(b, c, pt):
            return (layer, pt[b * n_pages + n_pages - 1 - (c * pc + p)], 0, 0)
        return pl.BlockSpec((None, None, W_SB, PAGE_SIZE), index)

    row = lambda w: pl.BlockSpec((None, 1, w), lambda b, c, pt: (b, 0, 0))
    grid_spec = pltpu.PrefetchScalarGridSpec(
        num_scalar_prefetch=1,
        grid=(nb, n_pages // pc),
        in_specs=([row(W_SB)]
                  + [page_spec(p) for p in range(pc)]
                  + [page_spec(p) for p in range(pc)]
                  + [pl.BlockSpec((LANES, 2 * LANES), lambda b, c, pt: (0, 0))]),
        out_specs=row(W_SB),
        scratch_shapes=[pltpu.VMEM((SUBLANES, LANES), F32), pltpu.VMEM((SUBLANES, W_SB), F32)],
    )
    out = pl.pallas_call(
        functools.partial(_sb_dec_kernel, pc=pc),
        grid_spec=grid_spec,
        out_shape=jax.ShapeDtypeStruct((nb, 1, W_SB), F32),
        compiler_params=_params(("parallel", "arbitrary")),
        name="sb_decode",
    )(page_table.reshape(-1), q.reshape(nb, 1, W_SB), *([cache_kt] * pc), *([cache_vt] * pc), _suffix_matrix())
    return out.reshape(nb, W_SB)


def _df_dec_kernel(pt_ref, q_ref, kn_ref, vn_ref, bias_ref, bself_ref, lam_ref, g_ref, spread_ref, *refs,
                   pc, n_pages, lam_init):
    k_refs, v_refs = refs[:pc], refs[pc:2 * pc]
    o_ref, s_ref, m_ref, aself_ref, acc_ref = refs[2 * pc:]
    c = pl.program_id(1)
    nc = n_pages // pc

    @pl.when(c == 0)
    def _init():
        m_ref[...] = jnp.full_like(m_ref, NEG_INF)
        acc_ref[...] = jnp.zeros_like(acc_ref)

    def page_cols(page):
        return pl.ds(pl.multiple_of(page * PAGE_SIZE, PAGE_SIZE), PAGE_SIZE)

    @pl.when(c < nc)
    def _scores():
        row = lax.broadcasted_iota(jnp.int32, (SUBLANES, W_DF), 0)
        col = lax.broadcasted_iota(jnp.int32, (SUBLANES, W_DF), 1)
        own = (col // D_DF) == 2 * (row % H_DF) + row // H_DF
        qbd = jnp.where(own, jnp.broadcast_to(q_ref[...], (SUBLANES, W_DF)), 0.0).astype(BF16)
        m = m_ref[...]
        for p in range(pc):
            cols = page_cols(c * pc + p)
            s = _dot(qbd, k_refs[p][...].astype(BF16)) * (D_DF ** -0.5) + bias_ref[:, cols]
            s_ref[:, cols] = s
            m = jnp.maximum(m, jnp.max(s, axis=-1, keepdims=True))
        m_ref[...] = m

        @pl.when(c == nc - 1)
        def _weights():
            kn = jnp.broadcast_to(_bf16_round(kn_ref[...]), (SUBLANES, W_DF))
            s_self = jnp.sum(qbd.astype(F32) * kn, axis=-1, keepdims=True) * (D_DF ** -0.5) + bself_ref[...]
            m_fin = jnp.maximum(m, s_self)
            pr = jnp.exp(s_ref[...] - m_fin)
            p_self = jnp.exp(s_self - m_fin)
            total = jnp.sum(pr, axis=-1, keepdims=True) + p_self
            lam = _lambda(lam_ref[0:1, :], lam_ref[1:2, :], lam_ref[2:3, :], lam_ref[3:4, :], lam_init)
            pn = pr / total
            pn_self = jnp.broadcast_to(p_self / total, (SUBLANES, LANES))
            s_ref[...] = pn - lam * pltpu.roll(pn, shift=H_DF, axis=0)
            aself_ref[...] = pn_self - lam * pltpu.roll(pn_self, shift=H_DF, axis=0)

    @pl.when(c >= nc)
    def _values():
        rows = PAGE_SIZE * H_DF
        row = lax.broadcasted_iota(jnp.int32, (SUBLANES, rows), 0)
        col = lax.broadcasted_iota(jnp.int32, (SUBLANES, rows), 1)
        own = (col % H_DF) == row
        acc = acc_ref[...]
        for p in range(pc):
            cols = page_cols((c - nc) * pc + p)
            spread = _dot(s_ref[:, cols].astype(BF16), spread_ref[...])
            acc = acc + _dot(jnp.where(own, spread, 0.0).astype(BF16), v_refs[p][...].astype(BF16))
        acc_ref[...] = acc

        @pl.when(c == 2 * nc - 1)
        def _fin():
            o = acc[0:H_DF] + _bf16_round(aself_ref[0:H_DF, 0:1]) * _bf16_round(vn_ref[...])
            o_ref[...] = _sub_norm(o, g_ref[...], 1.0 - lam_init)


def _df_decode(q, k_new, v_new, cache_kt, cache_v, layer, page_table, bias_past, bias_self, lam_vecs, subln_g,
               pc, lam_init):
    nb, n_pages = page_table.shape
    nc = n_pages // pc
    head_w = 2 * D_DF
    row = lambda w: pl.BlockSpec((None, 1, w), lambda b, c, pt: (b, 0, 0))
    heads = pl.BlockSpec((None, H_DF, head_w), lambda b, c, pt: (b, 0, 0))
    full = lambda a: pl.BlockSpec(a.shape, lambda b, c, pt: (0,) * a.ndim)

    def page_spec(p, second_pass):
        def index(b, c, pt):
            chunk = jnp.maximum(c - nc, 0) if second_pass else jnp.minimum(c, nc - 1)
            return (layer, pt[b * n_pages + chunk * pc + p], 0, 0)
        shape = (None, None, PAGE_SIZE * H_DF, head_w) if second_pass else (None, None, W_DF, PAGE_SIZE)
        return pl.BlockSpec(shape, index)

    r = lax.broadcasted_iota(jnp.int32, (PAGE_SIZE, PAGE_SIZE * H_DF), 0)
    c = lax.broadcasted_iota(jnp.int32, (PAGE_SIZE, PAGE_SIZE * H_DF), 1)
    spread = (c // H_DF == r).astype(BF16)
    grid_spec = pltpu.PrefetchScalarGridSpec(
        num_scalar_prefetch=1,
        grid=(nb, 2 * nc),
        in_specs=([row(W_DF), row(W_DF), heads, full(bias_past), full(bias_self), full(lam_vecs),
                   full(subln_g), full(spread)]
                  + [page_spec(p, False) for p in range(pc)]
                  + [page_spec(p, True) for p in range(pc)]),
        out_specs=heads,
        scratch_shapes=[pltpu.VMEM((SUBLANES, n_pages * PAGE_SIZE), F32), pltpu.VMEM((SUBLANES, 1), F32),
                        pltpu.VMEM((SUBLANES, LANES), F32), pltpu.VMEM((SUBLANES, head_w), F32)],
    )
    r3 = lambda a: a.reshape(nb, 1, W_DF)
    out = pl.pallas_call(
        functools.partial(_df_dec_kernel, pc=pc, n_pages=n_pages, lam_init=lam_init),
        grid_spec=grid_spec,
        out_shape=jax.ShapeDtypeStruct((nb, H_DF, head_w), F32),
        compiler_params=_params(("parallel", "arbitrary")),
        name="df_decode",
    )(page_table.reshape(-1), r3(q), r3(k_new), v_new.reshape(nb, H_DF, head_w), bias_past, bias_self, lam_vecs,
      subln_g, spread, *([cache_kt] * pc), *([cache_v] * pc))
    return out.reshape(nb, W_DF)


PROMPT_TM = 256
PROMPT_ATT_BLOCK = 256
PROMPT_CONV_TILE = 512
PROMPT_MOE_BLOCK = 128
COMBINE_TM = 128
SAMPLE_MOE_BLOCK = 32
DECODE_PAGES_PER_STEP = 8


def kernel(x_prompt, x_sample, cache_sb_k, cache_sb_v, cache_df_k, cache_df_v, state_conv, page_table,
           w_in, rel_bias_table, lam_q1, lam_k1, lam_q2, lam_k2, subln_g, conv_w, conv_b, conv_ln_g,
           conv_ln_b, w_sb_out, w_df_out, w_conv_out, w_o, ln1_g, ln1_b, router_w, router_bias, w_gate,
           w_up, w_down, sh_gate, sh_up, sh_down, ln2_g, ln2_b):
    depth = w_in.shape[0]
    batch, seq, _ = x_prompt.shape
    nb = x_sample.shape[0]
    n_pages = page_table.shape[1]
    past_len = n_pages * PAGE_SIZE
    alpha = (2 * depth) ** 0.25
    n_phys = cache_sb_k.shape[1]

    xp = x_prompt.reshape(batch * seq, D_MODEL)
    xs = x_sample.reshape(nb, D_MODEL)
    bias_tiles = _bias_tiles(rel_bias_table, PROMPT_ATT_BLOCK)
    dist = past_len - jnp.arange(past_len, dtype=jnp.int32)
    bias_past = jnp.tile(_bias_of_distance(rel_bias_table, dist).T, (2, 1))
    bias_self = jnp.tile(_bias_of_distance(rel_bias_table, jnp.zeros((1,), jnp.int32)).T, (2, 1))
    row = lambda a: a.reshape(1, -1)
    sb_kt = jnp.transpose(cache_sb_k, (0, 1, 3, 4, 2)).reshape(depth, n_phys, W_SB, PAGE_SIZE)
    sb_vt = jnp.transpose(cache_sb_v, (0, 1, 3, 4, 2)).reshape(depth, n_phys, W_SB, PAGE_SIZE)
    df_kt = jnp.transpose(cache_df_k, (0, 1, 3, 4, 5, 2)).reshape(depth, n_phys, W_DF, PAGE_SIZE)
    df_v = cache_df_v.reshape(depth, n_phys, PAGE_SIZE * H_DF, 2 * D_DF)

    new_p, new_s = [], []
    for l in range(depth):
        lam_init = 0.8 - 0.6 * math.exp(-0.3 * l)
        w_in_b = w_in[l].astype(BF16)
        lam_vecs = jnp.stack([lam_q1[l], lam_k1[l], lam_q2[l], lam_k2[l]])
        merge_w = (w_sb_out[l].astype(BF16), w_df_out[l].astype(BF16), w_conv_out[l].astype(BF16),
                   w_o[l].astype(BF16), row(ln1_g[l]), row(ln1_b[l]))
        ffn = (router_w[l].T.astype(BF16), router_bias[l].reshape(N_EXPERTS, 1),
               w_gate[l].astype(BF16), w_up[l].astype(BF16), w_down[l].astype(BF16),
               sh_gate[l].astype(BF16), sh_up[l].astype(BF16), sh_down[l].astype(BF16),
               row(ln2_g[l]), row(ln2_b[l]))
        conv_p = (conv_w[l], row(conv_b[l]), row(conv_ln_g[l]), row(conv_ln_b[l]))

        q_sb, k_sb, v_sb, q_df, k_df, v_df, glu, gates = _inproj(xp, w_in_b, PROMPT_TM)
        o_sb = _sb_attention(q_sb, k_sb, v_sb, batch, seq, PROMPT_ATT_BLOCK)
        o_df = _df_attention(q_df, k_df, v_df, bias_tiles, lam_vecs, row(subln_g[l]), batch, seq,
                             PROMPT_ATT_BLOCK, lam_init)
        o_c, p_conv = _conv_branch(glu, *conv_p, batch, seq, PROMPT_CONV_TILE)
        x1 = _merge(xp, o_sb, o_df, o_c, gates, *merge_w, PROMPT_TM, alpha)
        xp = _moe(x1, ffn, PROMPT_TM, PROMPT_MOE_BLOCK, alpha)
        new_p.append((k_sb, v_sb, k_df, v_df, p_conv))

        q_sb, k_sb, v_sb, q_df, k_df, v_df, glu, gates = _inproj(xs, w_in_b, nb)
        o_sb = _sb_decode(q_sb, sb_kt, sb_vt, l, page_table, DECODE_PAGES_PER_STEP)
        o_df = _df_decode(q_df, k_df, v_df, df_kt, df_v, l, page_table, bias_past, bias_self,
                          lam_vecs, row(subln_g[l]), DECODE_PAGES_PER_STEP, lam_init)
        o_c, u_new = _conv_decode(glu, jnp.transpose(state_conv[l], (1, 0, 2)), *conv_p)
        s_conv = jnp.concatenate([state_conv[l][:, 1:], u_new[:, None, :]], axis=1)
        x1 = _merge(xs, o_sb, o_df, o_c, gates, *merge_w, nb, alpha)
        xs = _moe(x1, ffn, nb, SAMPLE_MOE_BLOCK, alpha)
        new_s.append((k_sb, v_sb, k_df, v_df, s_conv))

    def stacked(rows, i, shape):
        return jnp.stack([r[i] for r in rows]).reshape((depth,) + shape)

    return (xp.reshape(batch, seq, D_MODEL),
            xs.reshape(nb, 1, D_MODEL),
            stacked(new_p, 0, (batch, seq, H_SB, D_SB)),
            stacked(new_p, 1, (batch, seq, H_SB, D_SB)),
            stacked(new_p, 2, (batch, seq, H_DF, 2, D_DF)),
            stacked(new_p, 3, (batch, seq, H_DF, 2 * D_DF)),
            stacked(new_p, 4, (batch, CONV_HIST, C_CONV)),
            stacked(new_s, 0, (nb, 1, H_SB, D_SB)),
            stacked(new_s, 1, (nb, 1, H_SB, D_SB)),
            stacked(new_s, 2, (nb, 1, H_DF, 2, D_DF)),
            stacked(new_s, 3, (nb, 1, H_DF, 2 * D_DF)),
            stacked(new_s, 4, (nb, CONV_HIST, C_CONV)))
```

```python
import functools
import math

import jax
import jax.numpy as jnp
from jax import lax
from jax.experimental import pallas as pl
from jax.experimental.pallas import tpu as pltpu

F32 = jnp.float32
BF16 = jnp.bfloat16

D_MODEL = 1024
H_SB, D_SB = 4, 64
H_DF, D_DF = 4, 64
C_CONV = D_MODEL // 4
CONV_W = 31
N_BUCKETS, MAX_DIST = 32, 128
N_EXPERTS, TOP_K, N_GROUPS, TOPK_GROUPS = 64, 8, 8, 4
GROUP_SIZE = N_EXPERTS // N_GROUPS
D_EXPERT = D_MODEL // 4
D_SHARED = D_MODEL // 4
ROUTE_SCALE = 2.5
LN_EPS = 1e-5
PAGE_SIZE = 128
W_SB = H_SB * D_SB
W_DF = H_DF * 2 * D_DF
IN_SIZES = (W_SB, W_SB, W_SB, W_DF, W_DF, W_DF, 2 * C_CONV, 3 * D_MODEL)
IN_WIDTH = sum(IN_SIZES)

LANES = 128
SUBLANES = 8
VMEM_LIMIT = 56 * 1024 * 1024
NEG_INF = float("-inf")
NT_DIMS = (((1,), (1,)), ((), ()))
QK_SCALE = D_SB ** -0.5
assert D_SB == D_DF and QK_SCALE == 0.125


def _params(semantics, vmem=VMEM_LIMIT):
    return pltpu.CompilerParams(dimension_semantics=semantics, vmem_limit_bytes=vmem)


def _dot(a, b):
    return jnp.dot(a, b, preferred_element_type=F32)


def _dot_nt(a, b):
    return lax.dot_general(a, b, NT_DIMS, preferred_element_type=F32)


def _layer_norm(y, g, b):
    mu = jnp.mean(y, axis=-1, keepdims=True)
    d = y - mu
    var = jnp.mean(d * d, axis=-1, keepdims=True)
    return d * lax.rsqrt(var + LN_EPS) * g + b


def _silu(x):
    return x * jax.nn.sigmoid(x)


def _inproj_kernel(x_ref, w_ref, *out_refs):
    x = x_ref[...].astype(BF16)
    off = 0
    for ref, width in zip(out_refs, IN_SIZES):
        for c in range(0, width, 512):
            cw = min(512, width - c)
            ref[:, c:c + cw] = _dot(x, w_ref[:, off + c:off + c + cw])
        off += width


def _inproj(x, w_bf16, tm):
    n = x.shape[0]
    return pl.pallas_call(
        _inproj_kernel,
        grid=(n // tm,),
        in_specs=[pl.BlockSpec((tm, D_MODEL), lambda i: (i, 0)),
                  pl.BlockSpec((D_MODEL, IN_WIDTH), lambda i: (0, 0), pipeline_mode=pl.Buffered(1))],
        out_specs=[pl.BlockSpec((tm, w), lambda i: (i, 0)) for w in IN_SIZES],
        out_shape=[jax.ShapeDtypeStruct((n, w), F32) for w in IN_SIZES],
        compiler_params=_params(("parallel",)),
        name="inproj",
    )(x, w_bf16)


def _causal_pairs(nq, descending):
    qi, kj = [], []
    for i in range(nq):
        ks = range(i, -1, -1) if descending else range(i + 1)
        for j in ks:
            qi.append(i)
            kj.append(j)
    return jnp.asarray(qi, jnp.int32), jnp.asarray(kj, jnp.int32)


def _suffix_matrix():
    r = lax.broadcasted_iota(jnp.int32, (LANES, LANES), 0)
    c = lax.broadcasted_iota(jnp.int32, (LANES, LANES), 1)
    u = (r > c).astype(BF16)
    return jnp.concatenate([u, jnp.ones((LANES, LANES), BF16)], axis=1)


def _split_dot(x, rhs_bf16, terms=2):
    out = None
    for _ in range(terms):
        part = x.astype(BF16)
        x = x - part.astype(F32)
        d = _dot(part, rhs_bf16)
        out = d if out is None else out + d
    return out


def _log_sigmoid_pair(z):
    sp = jnp.log1p(jnp.exp(-jnp.abs(z)))
    return jnp.minimum(z, 0.0) - sp, -jnp.maximum(z, 0.0) - sp


EXP_UNDERFLOW = -104.0


def _sb_kernel(qi_ref, kj_ref, q_ref, k_ref, v_ref, tri_ref, o_ref, carry_ref, acc_ref, live_ref, *, tq):
    p = pl.program_id(1)
    qi = qi_ref[p]
    kj = kj_ref[p]

    @pl.when(kj == qi)
    def _init():
        carry_ref[...] = jnp.zeros_like(carry_ref)
        acc_ref[...] = jnp.zeros_like(acc_ref)
        for h in range(H_SB):
            live_ref[h] = 1

    def head_step(h, diag):
        hs = slice(h * D_SB, (h + 1) * D_SB)
        q = (q_ref[:, hs] * QK_SCALE).astype(BF16)
        k = k_ref[:, hs].astype(BF16)
        z = _dot_nt(q, k)
        log_beta, log_keep = _log_sigmoid_pair(z)
        if diag:
            row = lax.broadcasted_iota(jnp.int32, (tq, tq), 0)
            col = lax.broadcasted_iota(jnp.int32, (tq, tq), 1)
            valid = col < row
            log_keep = jnp.where(valid, log_keep, 0.0)
        carry = carry_ref[h]
        acc = acc_ref[h]
        for c in reversed(range(tq // LANES)):
            sl = slice(c * LANES, (c + 1) * LANES)
            st = _split_dot(log_keep[:, sl], tri_ref[...])
            w = jnp.exp(log_beta[:, sl] + st[:, :LANES] + carry)
            if diag:
                w = jnp.where(valid[:, sl], w, 0.0)
            acc = acc + _dot(w.astype(BF16), v_ref[sl, hs].astype(BF16))
            carry = carry + st[:, LANES:]
        carry_ref[h] = carry
        acc_ref[h] = acc
        live_ref[h] = (jnp.max(carry) >= EXP_UNDERFLOW).astype(jnp.int32)

    def step(diag):
        for h in range(H_SB):
            @pl.when(live_ref[h] == 1)
            def _head(h=h):
                head_step(h, diag)

    @pl.when(kj == qi)
    def _diag():
        step(True)

    @pl.when(kj < qi)
    def _off():
        step(False)

    @pl.when(kj == 0)
    def _fin():
        for h in range(H_SB):
            o_ref[:, h * D_SB:(h + 1) * D_SB] = acc_ref[h]


def _sb_attention(q, k, v, batch, seq, tq):
    nq = seq // tq
    qi, kj = _causal_pairs(nq, descending=True)
    grid_spec = pltpu.PrefetchScalarGridSpec(
        num_scalar_prefetch=2,
        grid=(batch, int(qi.shape[0])),
        in_specs=[pl.BlockSpec((tq, W_SB), lambda b, p, qi, kj: (b * nq + qi[p], 0)),
                  pl.BlockSpec((tq, W_SB), lambda b, p, qi, kj: (b * nq + kj[p], 0)),
                  pl.BlockSpec((tq, W_SB), lambda b, p, qi, kj: (b * nq + kj[p], 0)),
                  pl.BlockSpec((LANES, 2 * LANES), lambda b, p, qi, kj: (0, 0))],
        out_specs=pl.BlockSpec((tq, W_SB), lambda b, p, qi, kj: (b * nq + qi[p], 0)),
        scratch_shapes=[pltpu.VMEM((H_SB, tq, LANES), F32), pltpu.VMEM((H_SB, tq, D_SB), F32),
                        pltpu.SMEM((H_SB,), jnp.int32)],
    )
    return pl.pallas_call(
        functools.partial(_sb_kernel, tq=tq),
        grid_spec=grid_spec,
        out_shape=jax.ShapeDtypeStruct((batch * seq, W_SB), F32),
        compiler_params=_params(("parallel", "arbitrary")),
        name="sb_attention",
    )(qi, kj, q, k, v, _suffix_matrix())


def _bias_of_distance(table, n):
    max_exact = N_BUCKETS // 2
    nf = jnp.maximum(n, 1).astype(F32)
    large = max_exact + (jnp.log(nf / max_exact) / math.log(MAX_DIST / max_exact)
                         * (N_BUCKETS - max_exact)).astype(jnp.int32)
    large = jnp.minimum(large, N_BUCKETS - 1)
    bucket = jnp.where(n < max_exact, n, large)
    out = jnp.zeros(n.shape + (H_DF,), F32)
    for b in range(N_BUCKETS):
        out = jnp.where((bucket == b)[..., None], table[b].astype(F32), out)
    return out


def _bias_tiles(table, tq):
    assert tq >= MAX_DIST, "blocks two or more behind must all fall in the last bucket"
    key = jnp.arange(tq, dtype=jnp.int32)[:, None]
    query = jnp.arange(tq, dtype=jnp.int32)[None, :]
    tiles = [_bias_of_distance(table, jnp.maximum(t * tq + query - key, 0)) for t in range(3)]
    return jnp.transpose(jnp.stack(tiles), (0, 3, 1, 2))


def _lambda(lq1, lk1, lq2, lk2, lam_init):
    return (jnp.exp(jnp.sum(lq1 * lk1, axis=-1, keepdims=True))
            - jnp.exp(jnp.sum(lq2 * lk2, axis=-1, keepdims=True)) + lam_init)


def _sub_norm(o, g, post_scale):
    ms = jnp.mean(o * o, axis=-1, keepdims=True)
    return o * lax.rsqrt(ms + LN_EPS) * g * post_scale


def _df_kernel(qi_ref, kj_ref, q_ref, k_ref, v_ref, bias_ref, lam_ref, g_ref, o_ref,
               m_ref, l_ref, acc_ref, *, tq, lam_init):
    p = pl.program_id(1)
    qi = qi_ref[p]
    kj = kj_ref[p]

    @pl.when(kj == 0)
    def _init():
        m_ref[...] = jnp.full_like(m_ref, NEG_INF)
        l_ref[...] = jnp.zeros_like(l_ref)
        acc_ref[...] = jnp.zeros_like(acc_ref)

    def step(diag):
        tile = jnp.minimum(qi - kj, 2)
        if diag:
            key = lax.broadcasted_iota(jnp.int32, (tq, tq), 0)
            query = lax.broadcasted_iota(jnp.int32, (tq, tq), 1)
            causal = key <= query
        for h in range(H_DF):
            vt = v_ref[:, h * 2 * D_DF:(h + 1) * 2 * D_DF].T.astype(BF16)
            bias = bias_ref[tile, h]
            for mp in range(2):
                r = 2 * h + mp
                rs = slice(r * D_DF, (r + 1) * D_DF)
                s = _dot_nt(k_ref[:, rs].astype(BF16), (q_ref[:, rs] * QK_SCALE).astype(BF16)) + bias
                if diag:
                    s = jnp.where(causal, s, NEG_INF)
                m_old = m_ref[r]
                m_new = jnp.maximum(m_old, jnp.max(s, axis=0, keepdims=True))
                pr = jnp.exp(s - m_new)
                alpha = jnp.exp(m_old - m_new)
                l_ref[r] = alpha * l_ref[r] + jnp.sum(pr, axis=0, keepdims=True)
                acc_ref[r] = alpha * acc_ref[r] + _dot(vt, pr.astype(BF16))
                m_ref[r] = m_new

    @pl.when(kj == qi)
    def _diag():
        step(True)

    @pl.when(kj < qi)
    def _off():
        step(False)

    @pl.when(kj == qi)
    def _fin():
        lam = _lambda(lam_ref[0:1, :], lam_ref[1:2, :], lam_ref[2:3, :], lam_ref[3:4, :], lam_init)
        for h in range(H_DF):
            o = acc_ref[2 * h] / l_ref[2 * h] - lam * (acc_ref[2 * h + 1] / l_ref[2 * h + 1])
            ms = jnp.mean(o * o, axis=0, keepdims=True)
            o = o * lax.rsqrt(ms + LN_EPS) * g_ref[...] * (1.0 - lam_init)
            o_ref[:, h * 2 * D_DF:(h + 1) * 2 * D_DF] = o.T


def _df_attention(q, k, v, bias_tiles, lam_vecs, subln_g, batch, seq, tq, lam_init):
    nq = seq // tq
    qi, kj = _causal_pairs(nq, descending=False)
    blk = lambda which: (lambda b, p, qi, kj: (b * nq + (qi if which == "q" else kj)[p], 0))
    grid_spec = pltpu.PrefetchScalarGridSpec(
        num_scalar_prefetch=2,
        grid=(batch, int(qi.shape[0])),
        in_specs=[pl.BlockSpec((tq, W_DF), blk("q")),
                  pl.BlockSpec((tq, W_DF), blk("k")),
                  pl.BlockSpec((tq, W_DF), blk("k")),
                  pl.BlockSpec((3, H_DF, tq, tq), lambda b, p, qi, kj: (0, 0, 0, 0)),
                  pl.BlockSpec((4, D_DF), lambda b, p, qi, kj: (0, 0)),
                  pl.BlockSpec((2 * D_DF, 1), lambda b, p, qi, kj: (0, 0))],
        out_specs=pl.BlockSpec((tq, W_DF), blk("q")),
        scratch_shapes=[pltpu.VMEM((2 * H_DF, 1, tq), F32), pltpu.VMEM((2 * H_DF, 1, tq), F32),
                        pltpu.VMEM((2 * H_DF, 2 * D_DF, tq), F32)],
    )
    return pl.pallas_call(
        functools.partial(_df_kernel, tq=tq, lam_init=lam_init),
        grid_spec=grid_spec,
        out_shape=jax.ShapeDtypeStruct((batch * seq, W_DF), F32),
        compiler_params=_params(("parallel", "arbitrary")),
        name="df_attention",
    )(qi, kj, q, k, v, bias_tiles, lam_vecs, subln_g)


CONV_HIST = CONV_W - 1
CONV_PAD = 32
CONV_CHUNK = 64


def _bf16_round(x):
    return x.astype(BF16).astype(F32)


def _conv_kernel(glu_ref, w_ref, cb_ref, g_ref, b_ref, o_ref, st_ref, ubuf_ref, tail_ref, *, tt):
    i = pl.program_id(1)

    @pl.when(i == 0)
    def _zero_history():
        ubuf_ref[0:CONV_PAD, :] = jnp.zeros((CONV_PAD, C_CONV), F32)

    @pl.when(i > 0)
    def _carry_history():
        ubuf_ref[0:CONV_PAD, :] = ubuf_ref[tt:tt + CONV_PAD, :]

    u = glu_ref[:, :C_CONV] * jax.nn.sigmoid(glu_ref[:, C_CONV:])
    ubuf_ref[CONV_PAD:CONV_PAD + tt, :] = _bf16_round(u)
    taps = [_bf16_round(w_ref[j:j + 1, :]) for j in range(CONV_W)]
    first = CONV_PAD - CONV_HIST
    for r0 in range(0, tt, CONV_CHUNK):
        acc = jnp.zeros((CONV_CHUNK, C_CONV), F32)
        for j in range(CONV_W):
            acc = acc + ubuf_ref[first + r0 + j:first + r0 + j + CONV_CHUNK, :] * taps[j]
        y = _layer_norm(acc + cb_ref[...], g_ref[...], b_ref[...])
        o_ref[r0:r0 + CONV_CHUNK, :] = _silu(y)

    @pl.when(i == pl.num_programs(1) - 1)
    def _final_state():
        tail_ref[...] = u[tt - CONV_PAD:, :]
        st_ref[...] = tail_ref[CONV_PAD - CONV_HIST:, :]


def _conv_branch(glu, conv_w, conv_b, g, b, batch, seq, tt):
    nt = seq // tt
    vec = lambda: pl.BlockSpec((1, C_CONV), lambda bb, i: (0, 0))
    return pl.pallas_call(
        functools.partial(_conv_kernel, tt=tt),
        grid=(batch, nt),
        in_specs=[pl.BlockSpec((tt, 2 * C_CONV), lambda bb, i: (bb * nt + i, 0)),
                  pl.BlockSpec((CONV_W, C_CONV), lambda bb, i: (0, 0)),
                  vec(), vec(), vec()],
        out_specs=[pl.BlockSpec((tt, C_CONV), lambda bb, i: (bb * nt + i, 0)),
                   pl.BlockSpec((None, CONV_HIST, C_CONV), lambda bb, i: (bb, 0, 0))],
        out_shape=[jax.ShapeDtypeStruct((batch * seq, C_CONV), F32),
                   jax.ShapeDtypeStruct((batch, CONV_HIST, C_CONV), F32)],
        scratch_shapes=[pltpu.VMEM((CONV_PAD + tt, C_CONV), F32), pltpu.VMEM((CONV_PAD, C_CONV), F32)],
        compiler_params=_params(("parallel", "arbitrary")),
        name="conv_branch",
    )(glu, conv_w, conv_b, g, b)


def _conv_dec_kernel(glu_ref, st_ref, w_ref, cb_ref, g_ref, b_ref, o_ref, u_ref):
    u = glu_ref[:, :C_CONV] * jax.nn.sigmoid(glu_ref[:, C_CONV:])
    acc = u * w_ref[CONV_HIST:CONV_W, :]
    for j in range(CONV_HIST):
        acc = acc + st_ref[j] * w_ref[j:j + 1, :]
    y = _layer_norm(acc + cb_ref[...], g_ref[...], b_ref[...])
    o_ref[...] = _silu(y)
    u_ref[...] = u


def _conv_decode(glu, state_t, conv_w, conv_b, g, b):
    n = glu.shape[0]
    return pl.pallas_call(
        _conv_dec_kernel,
        out_shape=[jax.ShapeDtypeStruct((n, C_CONV), F32), jax.ShapeDtypeStruct((n, C_CONV), F32)],
        name="conv_decode",
    )(glu, state_t, conv_w, conv_b, g, b)


def _merge_kernel(x_ref, osb_ref, odf_ref, oc_ref, gate_ref, wsb_ref, wdf_ref, wc_ref, wo_ref,
                  g_ref, b_ref, o_ref, *, alpha):
    merged = (jax.nn.sigmoid(gate_ref[:, 0:D_MODEL]) * _dot(osb_ref[...].astype(BF16), wsb_ref[...])
              + jax.nn.sigmoid(gate_ref[:, D_MODEL:2 * D_MODEL]) * _dot(odf_ref[...].astype(BF16), wdf_ref[...])
              + jax.nn.sigmoid(gate_ref[:, 2 * D_MODEL:]) * _dot(oc_ref[...].astype(BF16), wc_ref[...]))
    m = _dot(merged.astype(BF16), wo_ref[...])
    o_ref[...] = _layer_norm(alpha * x_ref[...] + m, g_ref[...], b_ref[...])


def _merge(x, o_sb, o_df, o_c, gates, w_sb, w_df, w_c, w_o, g, b, tm, alpha):
    n = x.shape[0]
    rows = lambda w: pl.BlockSpec((tm, w), lambda i: (i, 0))
    full = lambda a: pl.BlockSpec(a.shape, lambda i: (0, 0))
    return pl.pallas_call(
        functools.partial(_merge_kernel, alpha=alpha),
        grid=(n // tm,),
        in_specs=[rows(D_MODEL), rows(W_SB), rows(W_DF), rows(C_CONV), rows(3 * D_MODEL),
                  full(w_sb), full(w_df), full(w_c), full(w_o), full(g), full(b)],
        out_specs=rows(D_MODEL),
        out_shape=jax.ShapeDtypeStruct((n, D_MODEL), F32),
        compiler_params=_params(("parallel",)),
        name="merge_ln",
    )(x, o_sb, o_df, o_c, gates, w_sb, w_df, w_c, w_o, g, b)


def _first_argmax(vals, ids, axis, sentinel):
    mx = jnp.max(vals, axis=axis, keepdims=True)
    ix = jnp.min(jnp.where(vals == mx, ids, sentinel), axis=axis, keepdims=True)
    return mx, ix


def _router_kernel(x_ref, rw_ref, rb_ref, before_ref, idx_ref, w_ref, rank_ref, count_ref, seen_ref, *, tm):
    @pl.when(pl.program_id(0) == 0)
    def _init():
        seen_ref[...] = jnp.zeros_like(seen_ref)

    logits = _dot_nt(rw_ref[...], x_ref[...].astype(BF16))
    scores = jax.nn.sigmoid(logits)
    biased = scores + rb_ref[...]
    sub = lax.broadcasted_iota(jnp.int32, (GROUP_SIZE, tm), 0).astype(F32)
    gscore = jnp.zeros((N_GROUPS, tm), F32)
    for g in range(N_GROUPS):
        blk = biased[g * GROUP_SIZE:(g + 1) * GROUP_SIZE, :]
        m1, i1 = _first_argmax(blk, sub, 0, float(GROUP_SIZE))
        m2 = jnp.max(jnp.where(sub == i1, NEG_INF, blk), axis=0, keepdims=True)
        gscore = jnp.where(sub == float(g), m1 + m2, gscore)
    gmask = jnp.zeros((N_GROUPS, tm), F32)
    for _ in range(TOPK_GROUPS):
        _, ig = _first_argmax(gscore, sub, 0, float(N_GROUPS))
        sel = sub == ig
        gmask = jnp.where(sel, 1.0, gmask)
        gscore = jnp.where(sel, NEG_INF, gscore)
    eid = lax.broadcasted_iota(jnp.int32, (N_EXPERTS, tm), 0).astype(F32)
    cand = jnp.concatenate(
        [jnp.where(gmask[g:g + 1, :] > 0.0, biased[g * GROUP_SIZE:(g + 1) * GROUP_SIZE, :], NEG_INF)
         for g in range(N_GROUPS)], axis=0)
    total = jnp.zeros((1, tm), F32)
    picked, chosen = [], []
    member = jnp.zeros((N_EXPERTS, tm), F32)
    for k in range(TOP_K):
        _, ie = _first_argmax(cand, eid, 0, float(N_EXPERTS))
        sel = eid == ie
        wk = jnp.sum(jnp.where(sel, scores, 0.0), axis=0, keepdims=True)
        cand = jnp.where(sel, NEG_INF, cand)
        member = jnp.where(sel, 1.0, member)
        idx_ref[k:k + 1, :] = ie.astype(jnp.int32)
        picked.append(wk)
        chosen.append(ie)
        total = total + wk
    for k in range(TOP_K):
        w_ref[k:k + 1, :] = picked[k] / total * ROUTE_SCALE
    earlier = _dot(member.astype(BF16), before_ref[...]) + seen_ref[...]
    for k in range(TOP_K):
        rank_ref[k:k + 1, :] = jnp.sum(jnp.where(eid == chosen[k], earlier, 0.0), axis=0,
                                       keepdims=True).astype(jnp.int32)
    seen_ref[...] = seen_ref[...] + jnp.sum(member, axis=1, keepdims=True)
    count_ref[...] = seen_ref[...].astype(jnp.int32)


def _router(x, rw_t, rb_col, tm):
    n = x.shape[0]
    r = lax.broadcasted_iota(jnp.int32, (tm, tm), 0)
    c = lax.broadcasted_iota(jnp.int32, (tm, tm), 1)
    before = (r < c).astype(BF16)
    pairs = lambda: pl.BlockSpec((TOP_K, tm), lambda i: (0, i))
    return pl.pallas_call(
        functools.partial(_router_kernel, tm=tm),
        grid=(n // tm,),
        in_specs=[pl.BlockSpec((tm, D_MODEL), lambda i: (i, 0)),
                  pl.BlockSpec((N_EXPERTS, D_MODEL), lambda i: (0, 0)),
                  pl.BlockSpec((N_EXPERTS, 1), lambda i: (0, 0)),
                  pl.BlockSpec((tm, tm), lambda i: (0, 0))],
        out_specs=[pairs(), pairs(), pairs(), pl.BlockSpec((N_EXPERTS, 1), lambda i: (0, 0))],
        out_shape=[jax.ShapeDtypeStruct((TOP_K, n), jnp.int32),
                   jax.ShapeDtypeStruct((TOP_K, n), F32),
                   jax.ShapeDtypeStruct((TOP_K, n), jnp.int32),
                   jax.ShapeDtypeStruct((N_EXPERTS, 1), jnp.int32)],
        scratch_shapes=[pltpu.VMEM((N_EXPERTS, 1), F32)],
        compiler_params=_params(("arbitrary",)),
        name="router",
    )(x, rw_t, rb_col, before)


GATHER_UNROLL = 8


def _start_row_gather(src_hbm, row_of, dst_ref, sem, count):
    def body(j, carry):
        pltpu.make_async_copy(src_hbm.at[pl.ds(row_of(j), 1)], dst_ref.at[pl.ds(j, 1)], sem).start()
        return carry
    lax.fori_loop(0, count, body, 0, unroll=GATHER_UNROLL)


def _wait_row_gather(src_hbm, dst_ref, sem):
    pltpu.make_async_copy(src_hbm.at[pl.ds(0, dst_ref.shape[0])], dst_ref, sem).wait()


def _dispatch_tables(top_idx_t, rank_t, counts, br):
    k, n = top_idx_t.shape
    counts = counts.reshape(N_EXPERTS)
    padded = (counts + br - 1) // br * br
    pend = jnp.cumsum(padded)
    pstart = pend - padded
    experts = jnp.arange(N_EXPERTS, dtype=jnp.int32)
    dest = rank_t + jnp.sum(jnp.where(top_idx_t[:, :, None] == experts, pstart, 0), axis=-1)
    n_blocks = -(-(k * n) // br) + N_EXPERTS
    first_row = jnp.arange(n_blocks, dtype=jnp.int32)[:, None] * br
    blk_expert = jnp.minimum(jnp.sum((pend[None, :] <= first_row).astype(jnp.int32), axis=1), N_EXPERTS - 1)
    n_used = (pend[-1] // br).astype(jnp.int32).reshape(1)
    return dest.astype(jnp.int32), blk_expert.astype(jnp.int32), n_used, pend.astype(jnp.int32), n_blocks


def _dispatch_kernel(dest_ref, pend_ref, nused_ref, x_ref, xs_hbm, zbuf, sem, zsem, *, tm, br, n_blocks):
    i = pl.program_id(0)
    rows = TOP_K * tm

    @pl.when(i == 0)
    def _zero_fill():
        zbuf[...] = jnp.zeros_like(zbuf)

        def fill(first_row):
            return pltpu.make_async_copy(zbuf, xs_hbm.at[pl.ds(pl.multiple_of(first_row, br), br)], zsem)

        def has_rows(e):
            return pend_ref[e] > (pend_ref[e - 1] if e else 0)

        for e in range(N_EXPERTS):
            @pl.when(has_rows(e))
            def _start(e=e):
                fill(pend_ref[e] - br).start()

        def start_tail(b, carry):
            fill(b * br).start()
            return carry
        lax.fori_loop(nused_ref[0], n_blocks, start_tail, 0)

        for e in range(N_EXPERTS):
            @pl.when(has_rows(e))
            def _wait(e=e):
                fill(pend_ref[e] - br).wait()

        def wait_tail(b, carry):
            fill(b * br).wait()
            return carry
        lax.fori_loop(nused_ref[0], n_blocks, wait_tail, 0)

    def wait_one_tile():
        pltpu.make_async_copy(xs_hbm.at[pl.ds(0, rows)], xs_hbm.at[pl.ds(0, rows)], sem).wait()

    for k in range(TOP_K):
        def body(j, carry, k=k):
            row = dest_ref[k, i * tm + j]
            pltpu.make_async_copy(x_ref.at[pl.ds(j, 1)], xs_hbm.at[pl.ds(row, 1)], sem).start()
            return carry
        lax.fori_loop(0, tm, body, 0, unroll=GATHER_UNROLL)

    wait_one_tile()


def _dispatch(x, dest, pend, n_used, n_blocks, tm, br):
    n = x.shape[0]
    grid_spec = pltpu.PrefetchScalarGridSpec(
        num_scalar_prefetch=3,
        grid=(n // tm,),
        in_specs=[pl.BlockSpec((tm, D_MODEL), lambda i, d, pe, nu: (i, 0))],
        out_specs=pl.BlockSpec(memory_space=pl.ANY),
        scratch_shapes=[pltpu.VMEM((br, D_MODEL), F32), pltpu.SemaphoreType.DMA(()), pltpu.SemaphoreType.DMA(())],
    )
    return pl.pallas_call(
        functools.partial(_dispatch_kernel, tm=tm, br=br, n_blocks=n_blocks),
        grid_spec=grid_spec,
        out_shape=jax.ShapeDtypeStruct((n_blocks * br, D_MODEL), F32),
        compiler_params=_params(("arbitrary",)),
        name="dispatch",
    )(dest, pend, n_used, x)


def _expert_kernel(be_ref, nused_ref, xs_ref, wg_ref, wu_ref, wd_ref, o_ref):
    b = pl.program_id(0)

    @pl.when(b < nused_ref[0])
    def _compute():
        x = xs_ref[...].astype(BF16)
        hdn = _silu(_dot(x, wg_ref[...])) * _dot(x, wu_ref[...])
        o_ref[...] = _dot(hdn.astype(BF16), wd_ref[...])

    @pl.when(b >= nused_ref[0])
    def _unused():
        o_ref[...] = jnp.zeros_like(o_ref)


def _experts(xs, blk_expert, n_used, wg, wu, wd, br):
    n_blocks = blk_expert.shape[0]
    used = lambda b, nu: jnp.maximum(jnp.minimum(b, nu[0] - 1), 0)
    grid_spec = pltpu.PrefetchScalarGridSpec(
        num_scalar_prefetch=2,
        grid=(n_blocks,),
        in_specs=[pl.BlockSpec((br, D_MODEL), lambda b, be, nu: (used(b, nu), 0)),
                  pl.BlockSpec((None, D_MODEL, D_EXPERT), lambda b, be, nu: (be[b], 0, 0)),
                  pl.BlockSpec((None, D_MODEL, D_EXPERT), lambda b, be, nu: (be[b], 0, 0)),
                  pl.BlockSpec((None, D_EXPERT, D_MODEL), lambda b, be, nu: (be[b], 0, 0))],
        out_specs=pl.BlockSpec((br, D_MODEL), lambda b, be, nu: (b, 0)),
    )
    return pl.pallas_call(
        _expert_kernel,
        grid_spec=grid_spec,
        out_shape=jax.ShapeDtypeStruct((n_blocks * br, D_MODEL), F32),
        compiler_params=_params(("arbitrary",)),
        name="experts",
    )(blk_expert, n_used, xs, wg, wu, wd)


def _ffn_out_kernel(dest_ref, x_ref, tw_ref, out_hbm, sg_ref, su_ref, sd_ref, g_ref, b_ref, o_ref, ybuf, sem,
                    *, alpha, tm):
    i = pl.program_id(0)
    slot = i % 2

    def gather(tile, into):
        for k in range(TOP_K):
            _start_row_gather(out_hbm, lambda j, k=k: dest_ref[k, tile * tm + j], ybuf.at[into, k],
                              sem.at[into], tm)

    @pl.when(i == 0)
    def _first():
        gather(0, 0)

    @pl.when(i + 1 < pl.num_programs(0))
    def _prefetch():
        gather(i + 1, 1 - slot)

    for k in range(TOP_K):
        _wait_row_gather(out_hbm, ybuf.at[slot, k], sem.at[slot])
    x = x_ref[...]
    xb = x.astype(BF16)
    hdn = _silu(_dot(xb, sg_ref[...])) * _dot(xb, su_ref[...])
    y = ybuf[slot, 0] * tw_ref[:, 0:1]
    for k in range(1, TOP_K):
        y = y + ybuf[slot, k] * tw_ref[:, k:k + 1]
    y = y + _dot(hdn.astype(BF16), sd_ref[...])
    o_ref[...] = _layer_norm(alpha * x + y, g_ref[...], b_ref[...])


def _ffn_out(x, expert_out, dest, top_w, sg, su, sd, g, b, tm, alpha):
    n = x.shape[0]
    rows = lambda w: pl.BlockSpec((tm, w), lambda i, d: (i, 0))
    full = lambda a: pl.BlockSpec(a.shape, lambda i, d: (0, 0))
    grid_spec = pltpu.PrefetchScalarGridSpec(
        num_scalar_prefetch=1,
        grid=(n // tm,),
        in_specs=[rows(D_MODEL), rows(TOP_K), pl.BlockSpec(memory_space=pl.ANY),
                  full(sg), full(su), full(sd), full(g), full(b)],
        out_specs=rows(D_MODEL),
        scratch_shapes=[pltpu.VMEM((2, TOP_K, tm, D_MODEL), F32), pltpu.SemaphoreType.DMA((2,))],
    )
    return pl.pallas_call(
        functools.partial(_ffn_out_kernel, alpha=alpha, tm=tm),
        grid_spec=grid_spec,
        out_shape=jax.ShapeDtypeStruct((n, D_MODEL), F32),
        compiler_params=_params(("arbitrary",)),
        name="ffn_out_ln",
    )(dest, x, top_w, expert_out, sg, su, sd, g, b)


def _moe(x1, f, tm, br, alpha):
    router_w, router_b, wg, wu, wd, sg, su, sd, ln_g, ln_b = f
    n = x1.shape[0]
    top_idx_t, top_w_t, rank_t, counts = _router(x1, router_w, router_b, tm)
    dest, blk_expert, n_used, pend, n_blocks = _dispatch_tables(top_idx_t, rank_t, counts, br)
    xs = _dispatch(x1, dest, pend, n_used, n_blocks, tm, br)
    out = _experts(xs, blk_expert, n_used, wg, wu, wd, br)
    return _ffn_out(x1, out, dest, top_w_t.T, sg, su, sd, ln_g, ln_b, min(tm, COMBINE_TM), alpha)


def _sb_dec_kernel(pt_ref, q_ref, *refs, pc):
    k_refs, v_refs = refs[:pc], refs[pc:2 * pc]
    tri_ref, o_ref, carry_ref, acc_ref = refs[2 * pc:]
    c = pl.program_id(1)

    @pl.when(c == 0)
    def _init():
        carry_ref[...] = jnp.zeros_like(carry_ref)
        acc_ref[...] = jnp.zeros_like(acc_ref)

    row = lax.broadcasted_iota(jnp.int32, (SUBLANES, W_SB), 0)
    col = lax.broadcasted_iota(jnp.int32, (SUBLANES, W_SB), 1)
    own = (col // D_SB) == row
    qbd = jnp.where(own, jnp.broadcast_to(q_ref[...], (SUBLANES, W_SB)), 0.0).astype(BF16)
    carry = carry_ref[...]
    acc = acc_ref[...]
    z = jnp.concatenate([_dot(qbd, k_refs[p][...].astype(BF16)) for p in range(pc)], axis=0) * QK_SCALE
    log_beta, log_keep = _log_sigmoid_pair(z)
    st = _split_dot(log_keep, tri_ref[...], terms=3)
    for p in range(pc):
        rows = slice(p * SUBLANES, (p + 1) * SUBLANES)
        w = jnp.exp(log_beta[rows] + st[rows, :LANES] + carry)
        acc = acc + _dot_nt(w.astype(BF16), v_refs[p][...].astype(BF16))
        carry = carry + st[rows, LANES:]
    carry_ref[...] = carry
    acc_ref[...] = acc

    @pl.when(c == pl.num_programs(1) - 1)
    def _fin():
        o_ref[...] = jnp.sum(jnp.where(own, acc, 0.0), axis=0, keepdims=True)


def _sb_decode(q, cache_kt, cache_vt, layer, page_table, pc):
    nb, n_pages = page_table.shape

    def page_spec(p):
        def index(b, c, pt):
            return (layer, pt[b * n_pages + n_pages - 1 - (c * pc + p)], 0, 0)
        return pl.BlockSpec((None, None, W_SB, PAGE_SIZE), index)

    row = lambda w: pl.BlockSpec((None, 1, w), lambda b, c, pt: (b, 0, 0))
    grid_spec = pltpu.PrefetchScalarGridSpec(
        num_scalar_prefetch=1,
        grid=(nb, n_pages // pc),
        in_specs=([row(W_SB)]
                  + [page_spec(p) for p in range(pc)]
                  + [page_spec(p) for p in range(pc)]
                  + [pl.BlockSpec((LANES, 2 * LANES), lambda b, c, pt: (0, 0))]),
        out_specs=row(W_SB),
        scratch_shapes=[pltpu.VMEM((SUBLANES, LANES), F32), pltpu.VMEM((SUBLANES, W_SB), F32)],
    )
    out = pl.pallas_call(
        functools.partial(_sb_dec_kernel, pc=pc),
        grid_spec=grid_spec,
        out_shape=jax.ShapeDtypeStruct((nb, 1, W_SB), F32),
        compiler_params=_params(("parallel", "arbitrary")),
        name="sb_decode",
    )(page_table.reshape(-1), q.reshape(nb, 1, W_SB), *([cache_kt] * pc), *([cache_vt] * pc), _suffix_matrix())
    return out.reshape(nb, W_SB)


def _df_dec_kernel(pt_ref, q_ref, kn_ref, vn_ref, bias_ref, bself_ref, lam_ref, g_ref, spread_ref, *refs,
                   pc, n_pages, lam_init):
    k_refs, v_refs = refs[:pc], refs[pc:2 * pc]
    o_ref, s_ref, m_ref, aself_ref, acc_ref = refs[2 * pc:]
    c = pl.program_id(1)
    nc = n_pages // pc

    @pl.when(c == 0)
    def _init():
        m_ref[...] = jnp.full_like(m_ref, NEG_INF)
        acc_ref[...] = jnp.zeros_like(acc_ref)

    def page_cols(page):
        return pl.ds(pl.multiple_of(page * PAGE_SIZE, PAGE_SIZE), PAGE_SIZE)

    @pl.when(c < nc)
    def _scores():
        row = lax.broadcasted_iota(jnp.int32, (SUBLANES, W_DF), 0)
        col = lax.broadcasted_iota(jnp.int32, (SUBLANES, W_DF), 1)
        own = (col // D_DF) == 2 * (row % H_DF) + row // H_DF
        qbd = jnp.where(own, jnp.broadcast_to(q_ref[...], (SUBLANES, W_DF)), 0.0).astype(BF16)
        m = m_ref[...]
        for p in range(pc):
            cols = page_cols(c * pc + p)
            s = _dot(qbd, k_refs[p][...].astype(BF16)) * (D_DF ** -0.5) + bias_ref[:, cols]
            s_ref[:, cols] = s
            m = jnp.maximum(m, jnp.max(s, axis=-1, keepdims=True))
        m_ref[...] = m

        @pl.when(c == nc - 1)
        def _weights():
            kn = jnp.broadcast_to(_bf16_round(kn_ref[...]), (SUBLANES, W_DF))
            s_self = jnp.sum(qbd.astype(F32) * kn, axis=-1, keepdims=True) * (D_DF ** -0.5) + bself_ref[...]
            m_fin = jnp.maximum(m, s_self)
            pr = jnp.exp(s_ref[...] - m_fin)
            p_self = jnp.exp(s_self - m_fin)
            total = jnp.sum(pr, axis=-1, keepdims=True) + p_self
            lam = _lambda(lam_ref[0:1, :], lam_ref[1:2, :], lam_ref[2:3, :], lam_ref[3:4, :], lam_init)
            pn = pr / total
            pn_self = jnp.broadcast_to(p_self / total, (SUBLANES, LANES))
            s_ref[...] = pn - lam * pltpu.roll(pn, shift=H_DF, axis=0)
            aself_ref[...] = pn_self - lam * pltpu.roll(pn_self, shift=H_DF, axis=0)

    @pl.when(c >= nc)
    def _values():
        rows = PAGE_SIZE * H_DF
        row = lax.broadcasted_iota(jnp.int32, (SUBLANES, rows), 0)
        col = lax.broadcasted_iota(jnp.int32, (SUBLANES, rows), 1)
        own = (col % H_DF) == row
        acc = acc_ref[...]
        weights = jnp.concatenate([s_ref[:, page_cols((c - nc) * pc + p)] for p in range(pc)], axis=0)
        spread = _dot(weights.astype(BF16), spread_ref[...])
        for p in range(pc):
            own_head = jnp.where(own, spread[p * SUBLANES:(p + 1) * SUBLANES], 0.0)
            acc = acc + _dot(own_head.astype(BF16), v_refs[p][...].astype(BF16))
        acc_ref[...] = acc

        @pl.when(c == 2 * nc - 1)
        def _fin():
            o = acc[0:H_DF] + _bf16_round(aself_ref[0:H_DF, 0:1]) * _bf16_round(vn_ref[...])
            o_ref[...] = _sub_norm(o, g_ref[...], 1.0 - lam_init)


def _df_decode(q, k_new, v_new, cache_kt, cache_v, layer, page_table, bias_past, bias_self, lam_vecs, subln_g,
               pc, lam_init):
    nb, n_pages = page_table.shape
    nc = n_pages // pc
    head_w = 2 * D_DF
    row = lambda w: pl.BlockSpec((None, 1, w), lambda b, c, pt: (b, 0, 0))
    heads = pl.BlockSpec((None, H_DF, head_w), lambda b, c, pt: (b, 0, 0))
    full = lambda a: pl.BlockSpec(a.shape, lambda b, c, pt: (0,) * a.ndim)

    def page_spec(p, second_pass):
        def index(b, c, pt):
            chunk = jnp.maximum(c - nc, 0) if second_pass else jnp.minimum(c, nc - 1)
            return (layer, pt[b * n_pages + chunk * pc + p], 0, 0)
        shape = (None, None, PAGE_SIZE * H_DF, head_w) if second_pass else (None, None, W_DF, PAGE_SIZE)
        return pl.BlockSpec(shape, index)

    r = lax.broadcasted_iota(jnp.int32, (PAGE_SIZE, PAGE_SIZE * H_DF), 0)
    c = lax.broadcasted_iota(jnp.int32, (PAGE_SIZE, PAGE_SIZE * H_DF), 1)
    spread = (c // H_DF == r).astype(BF16)
    grid_spec = pltpu.PrefetchScalarGridSpec(
        num_scalar_prefetch=1,
        grid=(nb, 2 * nc),
        in_specs=([row(W_DF), row(W_DF), heads, full(bias_past), full(bias_self), full(lam_vecs),
                   full(subln_g), full(spread)]
                  + [page_spec(p, False) for p in range(pc)]
                  + [page_spec(p, True) for p in range(pc)]),
        out_specs=heads,
        scratch_shapes=[pltpu.VMEM((SUBLANES, n_pages * PAGE_SIZE), F32), pltpu.VMEM((SUBLANES, 1), F32),
                        pltpu.VMEM((SUBLANES, LANES), F32), pltpu.VMEM((SUBLANES, head_w), F32)],
    )
    r3 = lambda a: a.reshape(nb, 1, W_DF)
    out = pl.pallas_call(
        functools.partial(_df_dec_kernel, pc=pc, n_pages=n_pages, lam_init=lam_init),
        grid_spec=grid_spec,
        out_shape=jax.ShapeDtypeStruct((nb, H_DF, head_w), F32),
        compiler_params=_params(("parallel", "arbitrary")),
        name="df_decode",
    )(page_table.reshape(-1), r3(q), r3(k_new), v_new.reshape(nb, H_DF, head_w), bias_past, bias_self, lam_vecs,
      subln_g, spread, *([cache_kt] * pc), *([cache_v] * pc))
    return out.reshape(nb, W_DF)


PROMPT_TM = 256
PROMPT_ATT_BLOCK = 256
PROMPT_CONV_TILE = 512
PROMPT_MOE_BLOCK = 256
COMBINE_TM = 128
SAMPLE_MOE_BLOCK = 32
DECODE_PAGES_PER_STEP = 8


def kernel(x_prompt, x_sample, cache_sb_k, cache_sb_v, cache_df_k, cache_df_v, state_conv, page_table,
           w_in, rel_bias_table, lam_q1, lam_k1, lam_q2, lam_k2, subln_g, conv_w, conv_b, conv_ln_g,
           conv_ln_b, w_sb_out, w_df_out, w_conv_out, w_o, ln1_g, ln1_b, router_w, router_bias, w_gate,
           w_up, w_down, sh_gate, sh_up, sh_down, ln2_g, ln2_b):
    depth = w_in.shape[0]
    batch, seq, _ = x_prompt.shape
    nb = x_sample.shape[0]
    n_pages = page_table.shape[1]
    past_len = n_pages * PAGE_SIZE
    alpha = (2 * depth) ** 0.25
    n_phys = cache_sb_k.shape[1]

    xp = x_prompt.reshape(batch * seq, D_MODEL)
    xs = x_sample.reshape(nb, D_MODEL)
    bias_tiles = _bias_tiles(rel_bias_table, PROMPT_ATT_BLOCK)
    dist = past_len - jnp.arange(past_len, dtype=jnp.int32)
    bias_past = jnp.tile(_bias_of_distance(rel_bias_table, dist).T, (2, 1))
    bias_self = jnp.tile(_bias_of_distance(rel_bias_table, jnp.zeros((1,), jnp.int32)).T, (2, 1))
    row = lambda a: a.reshape(1, -1)
    sb_kt = jnp.transpose(cache_sb_k, (0, 1, 3, 4, 2)).reshape(depth, n_phys, W_SB, PAGE_SIZE)
    sb_vt = jnp.transpose(cache_sb_v, (0, 1, 3, 4, 2)).reshape(depth, n_phys, W_SB, PAGE_SIZE)
    df_kt = jnp.transpose(cache_df_k, (0, 1, 3, 4, 5, 2)).reshape(depth, n_phys, W_DF, PAGE_SIZE)
    df_v = cache_df_v.reshape(depth, n_phys, PAGE_SIZE * H_DF, 2 * D_DF)

    new_p, new_s = [], []
    for l in range(depth):
        lam_init = 0.8 - 0.6 * math.exp(-0.3 * l)
        w_in_b = w_in[l].astype(BF16)
        lam_vecs = jnp.stack([lam_q1[l], lam_k1[l], lam_q2[l], lam_k2[l]])
        merge_w = (w_sb_out[l].astype(BF16), w_df_out[l].astype(BF16), w_conv_out[l].astype(BF16),
                   w_o[l].astype(BF16), row(ln1_g[l]), row(ln1_b[l]))
        ffn = (router_w[l].T.astype(BF16), router_bias[l].reshape(N_EXPERTS, 1),
               w_gate[l].astype(BF16), w_up[l].astype(BF16), w_down[l].astype(BF16),
               sh_gate[l].astype(BF16), sh_up[l].astype(BF16), sh_down[l].astype(BF16),
               row(ln2_g[l]), row(ln2_b[l]))
        conv_p = (conv_w[l], row(conv_b[l]), row(conv_ln_g[l]), row(conv_ln_b[l]))

        q_sb, k_sb, v_sb, q_df, k_df, v_df, glu, gates = _inproj(xp, w_in_b, PROMPT_TM)
        o_sb = _sb_attention(q_sb, k_sb, v_sb, batch, seq, PROMPT_ATT_BLOCK)
        o_df = _df_attention(q_df, k_df, v_df, bias_tiles, lam_vecs, subln_g[l].reshape(2 * D_DF, 1), batch, seq,
                             PROMPT_ATT_BLOCK, lam_init)
        o_c, p_conv = _conv_branch(glu, *conv_p, batch, seq, PROMPT_CONV_TILE)
        x1 = _merge(xp, o_sb, o_df, o_c, gates, *merge_w, PROMPT_TM, alpha)
        xp = _moe(x1, ffn, PROMPT_TM, PROMPT_MOE_BLOCK, alpha)
        new_p.append((k_sb, v_sb, k_df, v_df, p_conv))

        q_sb, k_sb, v_sb, q_df, k_df, v_df, glu, gates = _inproj(xs, w_in_b, nb)
        o_sb = _sb_decode(q_sb, sb_kt, sb_vt, l, page_table, DECODE_PAGES_PER_STEP)
        o_df = _df_decode(q_df, k_df, v_df, df_kt, df_v, l, page_table, bias_past, bias_self,
                          lam_vecs, row(subln_g[l]), DECODE_PAGES_PER_STEP, lam_init)
        o_c, u_new = _conv_decode(glu, jnp.transpose(state_conv[l], (1, 0, 2)), *conv_p)
        s_conv = jnp.concatenate([state_conv[l][:, 1:], u_new[:, None, :]], axis=1)
        x1 = _merge(xs, o_sb, o_df, o_c, gates, *merge_w, nb, alpha)
        xs = _moe(x1, ffn, nb, SAMPLE_MOE_BLOCK, alpha)
        new_s.append((k_sb, v_sb, k_df, v_df, s_conv))

    def stacked(rows, i, shape):
        return jnp.stack([r[i] for r in rows]).reshape((depth,) + shape)

    return (xp.reshape(batch, seq, D_MODEL),
            xs.reshape(nb, 1, D_MODEL),
            stacked(new_p, 0, (batch, seq, H_SB, D_SB)),
            stacked(new_p, 1, (batch, seq, H_SB, D_SB)),
            stacked(new_p, 2, (batch, seq, H_DF, 2, D_DF)),
            stacked(new_p, 3, (batch, seq, H_DF, 2 * D_DF)),
            stacked(new_p, 4, (batch, CONV_HIST, C_CONV)),
            stacked(new_s, 0, (nb, 1, H_SB, D_SB)),
            stacked(new_s, 1, (nb, 1, H_SB, D_SB)),
            stacked(new_s, 2, (nb, 1, H_DF, 2, D_DF)),
            stacked(new_s, 3, (nb, 1, H_DF, 2 * D_DF)),
            stacked(new_s, 4, (nb, CONV_HIST, C_CONV)))
```

```python
import functools
import math

import jax
import jax.numpy as jnp
from jax import lax
from jax.experimental import pallas as pl
from jax.experimental.pallas import tpu as pltpu

F32 = jnp.float32
BF16 = jnp.bfloat16

D_MODEL = 1024
H_SB, D_SB = 4, 64
H_DF, D_DF = 4, 64
C_CONV = D_MODEL // 4
CONV_W = 31
N_BUCKETS, MAX_DIST = 32, 128
N_EXPERTS, TOP_K, N_GROUPS, TOPK_GROUPS = 64, 8, 8, 4
GROUP_SIZE = N_EXPERTS // N_GROUPS
D_EXPERT = D_MODEL // 4
D_SHARED = D_MODEL // 4
ROUTE_SCALE = 2.5
LN_EPS = 1e-5
PAGE_SIZE = 128
W_SB = H_SB * D_SB
W_DF = H_DF * 2 * D_DF
IN_SIZES = (W_SB, W_SB, W_SB, W_DF, W_DF, W_DF, 2 * C_CONV, 3 * D_MODEL)
IN_WIDTH = sum(IN_SIZES)

LANES = 128
SUBLANES = 8
VMEM_LIMIT = 56 * 1024 * 1024
NEG_INF = float("-inf")
NT_DIMS = (((1,), (1,)), ((), ()))
QK_SCALE = D_SB ** -0.5
assert D_SB == D_DF and QK_SCALE == 0.125


def _params(semantics, vmem=VMEM_LIMIT):
    return pltpu.CompilerParams(dimension_semantics=semantics, vmem_limit_bytes=vmem)


def _dot(a, b):
    return jnp.dot(a, b, preferred_element_type=F32)


def _dot_nt(a, b):
    return lax.dot_general(a, b, NT_DIMS, preferred_element_type=F32)


def _layer_norm(y, g, b):
    mu = jnp.mean(y, axis=-1, keepdims=True)
    d = y - mu
    var = jnp.mean(d * d, axis=-1, keepdims=True)
    return d * lax.rsqrt(var + LN_EPS) * g + b


def _silu(x):
    return x * jax.nn.sigmoid(x)


def _inproj_kernel(x_ref, w_ref, *out_refs):
    x = x_ref[...].astype(BF16)
    off = 0
    for ref, width in zip(out_refs, IN_SIZES):
        for c in range(0, width, 512):
            cw = min(512, width - c)
            ref[:, c:c + cw] = _dot(x, w_ref[:, off + c:off + c + cw])
        off += width


def _inproj(x, w_bf16, tm):
    n = x.shape[0]
    return pl.pallas_call(
        _inproj_kernel,
        grid=(n // tm,),
        in_specs=[pl.BlockSpec((tm, D_MODEL), lambda i: (i, 0)),
                  pl.BlockSpec((D_MODEL, IN_WIDTH), lambda i: (0, 0), pipeline_mode=pl.Buffered(1))],
        out_specs=[pl.BlockSpec((tm, w), lambda i: (i, 0)) for w in IN_SIZES],
        out_shape=[jax.ShapeDtypeStruct((n, w), F32) for w in IN_SIZES],
        compiler_params=_params(("parallel",)),
        name="inproj",
    )(x, w_bf16)


def _causal_pairs(nq, descending):
    qi, kj = [], []
    for i in range(nq):
        ks = range(i, -1, -1) if descending else range(i + 1)
        for j in ks:
            qi.append(i)
            kj.append(j)
    return jnp.asarray(qi, jnp.int32), jnp.asarray(kj, jnp.int32)


def _suffix_matrix():
    r = lax.broadcasted_iota(jnp.int32, (LANES, LANES), 0)
    c = lax.broadcasted_iota(jnp.int32, (LANES, LANES), 1)
    u = (r > c).astype(BF16)
    return jnp.concatenate([u, jnp.ones((LANES, LANES), BF16)], axis=1)


def _split_dot(x, rhs_bf16, terms=2):
    out = None
    for _ in range(terms):
        part = x.astype(BF16)
        x = x - part.astype(F32)
        d = _dot(part, rhs_bf16)
        out = d if out is None else out + d
    return out


def _log_sigmoid_pair(z):
    sp = jnp.log1p(jnp.exp(-jnp.abs(z)))
    return jnp.minimum(z, 0.0) - sp, -jnp.maximum(z, 0.0) - sp


EXP_UNDERFLOW = -104.0


def _sb_kernel(qi_ref, kj_ref, q_ref, k_ref, v_ref, tri_ref, o_ref, carry_ref, acc_ref, live_ref, *, tq):
    p = pl.program_id(1)
    qi = qi_ref[p]
    kj = kj_ref[p]

    @pl.when(kj == qi)
    def _init():
        carry_ref[...] = jnp.zeros_like(carry_ref)
        acc_ref[...] = jnp.zeros_like(acc_ref)
        for h in range(H_SB):
            live_ref[h] = 1

    def head_step(h, diag):
        hs = slice(h * D_SB, (h + 1) * D_SB)
        q = (q_ref[:, hs] * QK_SCALE).astype(BF16)
        k = k_ref[:, hs].astype(BF16)
        z = _dot_nt(q, k)
        log_beta, log_keep = _log_sigmoid_pair(z)
        if diag:
            row = lax.broadcasted_iota(jnp.int32, (tq, tq), 0)
            col = lax.broadcasted_iota(jnp.int32, (tq, tq), 1)
            valid = col < row
            log_keep = jnp.where(valid, log_keep, 0.0)
        carry = carry_ref[h]
        acc = acc_ref[h]
        for c in reversed(range(tq // LANES)):
            sl = slice(c * LANES, (c + 1) * LANES)
            st = _split_dot(log_keep[:, sl], tri_ref[...])
            w = jnp.exp(log_beta[:, sl] + st[:, :LANES] + carry)
            if diag:
                w = jnp.where(valid[:, sl], w, 0.0)
            acc = acc + _dot(w.astype(BF16), v_ref[sl, hs].astype(BF16))
            carry = carry + st[:, LANES:]
        carry_ref[h] = carry
        acc_ref[h] = acc
        live_ref[h] = (jnp.max(carry) >= EXP_UNDERFLOW).astype(jnp.int32)

    def step(diag):
        for h in range(H_SB):
            @pl.when(live_ref[h] == 1)
            def _head(h=h):
                head_step(h, diag)

    @pl.when(kj == qi)
    def _diag():
        step(True)

    @pl.when(kj < qi)
    def _off():
        step(False)

    @pl.when(kj == 0)
    def _fin():
        for h in range(H_SB):
            o_ref[:, h * D_SB:(h + 1) * D_SB] = acc_ref[h]


def _sb_attention(q, k, v, batch, seq, tq):
    nq = seq // tq
    qi, kj = _causal_pairs(nq, descending=True)
    grid_spec = pltpu.PrefetchScalarGridSpec(
        num_scalar_prefetch=2,
        grid=(batch, int(qi.shape[0])),
        in_specs=[pl.BlockSpec((tq, W_SB), lambda b, p, qi, kj: (b * nq + qi[p], 0)),
                  pl.BlockSpec((tq, W_SB), lambda b, p, qi, kj: (b * nq + kj[p], 0)),
                  pl.BlockSpec((tq, W_SB), lambda b, p, qi, kj: (b * nq + kj[p], 0)),
                  pl.BlockSpec((LANES, 2 * LANES), lambda b, p, qi, kj: (0, 0))],
        out_specs=pl.BlockSpec((tq, W_SB), lambda b, p, qi, kj: (b * nq + qi[p], 0)),
        scratch_shapes=[pltpu.VMEM((H_SB, tq, LANES), F32), pltpu.VMEM((H_SB, tq, D_SB), F32),
                        pltpu.SMEM((H_SB,), jnp.int32)],
    )
    return pl.pallas_call(
        functools.partial(_sb_kernel, tq=tq),
        grid_spec=grid_spec,
        out_shape=jax.ShapeDtypeStruct((batch * seq, W_SB), F32),
        compiler_params=_params(("parallel", "arbitrary")),
        name="sb_attention",
    )(qi, kj, q, k, v, _suffix_matrix())


def _bias_of_distance(table, n):
    max_exact = N_BUCKETS // 2
    nf = jnp.maximum(n, 1).astype(F32)
    large = max_exact + (jnp.log(nf / max_exact) / math.log(MAX_DIST / max_exact)
                         * (N_BUCKETS - max_exact)).astype(jnp.int32)
    large = jnp.minimum(large, N_BUCKETS - 1)
    bucket = jnp.where(n < max_exact, n, large)
    out = jnp.zeros(n.shape + (H_DF,), F32)
    for b in range(N_BUCKETS):
        out = jnp.where((bucket == b)[..., None], table[b].astype(F32), out)
    return out


def _bias_tiles(table, tq):
    assert tq >= MAX_DIST, "blocks two or more behind must all fall in the last bucket"
    key = jnp.arange(tq, dtype=jnp.int32)[:, None]
    query = jnp.arange(tq, dtype=jnp.int32)[None, :]
    tiles = [_bias_of_distance(table, jnp.maximum(t * tq + query - key, 0)) for t in range(3)]
    return jnp.transpose(jnp.stack(tiles), (0, 3, 1, 2))


def _lambda(lq1, lk1, lq2, lk2, lam_init):
    return (jnp.exp(jnp.sum(lq1 * lk1, axis=-1, keepdims=True))
            - jnp.exp(jnp.sum(lq2 * lk2, axis=-1, keepdims=True)) + lam_init)


def _sub_norm(o, g, post_scale):
    ms = jnp.mean(o * o, axis=-1, keepdims=True)
    return o * lax.rsqrt(ms + LN_EPS) * g * post_scale


def _df_kernel(qi_ref, kj_ref, q_ref, k_ref, v_ref, bias_ref, lam_ref, g_ref, o_ref,
               m_ref, l_ref, acc_ref, *, tq, lam_init):
    p = pl.program_id(1)
    qi = qi_ref[p]
    kj = kj_ref[p]

    @pl.when(kj == 0)
    def _init():
        m_ref[...] = jnp.full_like(m_ref, NEG_INF)
        l_ref[...] = jnp.zeros_like(l_ref)
        acc_ref[...] = jnp.zeros_like(acc_ref)

    def step(diag):
        tile = jnp.minimum(qi - kj, 2)
        if diag:
            key = lax.broadcasted_iota(jnp.int32, (tq, tq), 0)
            query = lax.broadcasted_iota(jnp.int32, (tq, tq), 1)
            causal = key <= query
        for h in range(H_DF):
            vt = v_ref[:, h * 2 * D_DF:(h + 1) * 2 * D_DF].T.astype(BF16)
            bias = bias_ref[tile, h]
            for mp in range(2):
                r = 2 * h + mp
                rs = slice(r * D_DF, (r + 1) * D_DF)
                s = _dot_nt(k_ref[:, rs].astype(BF16), (q_ref[:, rs] * QK_SCALE).astype(BF16)) + bias
                if diag:
                    s = jnp.where(causal, s, NEG_INF)
                m_old = m_ref[r]
                m_new = jnp.maximum(m_old, jnp.max(s, axis=0, keepdims=True))
                pr = jnp.exp(s - m_new)
                alpha = jnp.exp(m_old - m_new)
                l_ref[r] = alpha * l_ref[r] + jnp.sum(pr, axis=0, keepdims=True)
                acc_ref[r] = alpha * acc_ref[r] + _dot(vt, pr.astype(BF16))
                m_ref[r] = m_new

    @pl.when(kj == qi)
    def _diag():
        step(True)

    @pl.when(kj < qi)
    def _off():
        step(False)

    @pl.when(kj == qi)
    def _fin():
        lam = _lambda(lam_ref[0:1, :], lam_ref[1:2, :], lam_ref[2:3, :], lam_ref[3:4, :], lam_init)
        for h in range(H_DF):
            o = acc_ref[2 * h] / l_ref[2 * h] - lam * (acc_ref[2 * h + 1] / l_ref[2 * h + 1])
            ms = jnp.mean(o * o, axis=0, keepdims=True)
            o = o * lax.rsqrt(ms + LN_EPS) * g_ref[...] * (1.0 - lam_init)
            o_ref[:, h * 2 * D_DF:(h + 1) * 2 * D_DF] = o.T


def _df_attention(q, k, v, bias_tiles, lam_vecs, subln_g, batch, seq, tq, lam_init):
    nq = seq // tq
    qi, kj = _causal_pairs(nq, descending=False)
    blk = lambda which: (lambda b, p, qi, kj: (b * nq + (qi if which == "q" else kj)[p], 0))
    grid_spec = pltpu.PrefetchScalarGridSpec(
        num_scalar_prefetch=2,
        grid=(batch, int(qi.shape[0])),
        in_specs=[pl.BlockSpec((tq, W_DF), blk("q")),
                  pl.BlockSpec((tq, W_DF), blk("k")),
                  pl.BlockSpec((tq, W_DF), blk("k")),
                  pl.BlockSpec((3, H_DF, tq, tq), lambda b, p, qi, kj: (0, 0, 0, 0)),
                  pl.BlockSpec((4, D_DF), lambda b, p, qi, kj: (0, 0)),
                  pl.BlockSpec((2 * D_DF, 1), lambda b, p, qi, kj: (0, 0))],
        out_specs=pl.BlockSpec((tq, W_DF), blk("q")),
        scratch_shapes=[pltpu.VMEM((2 * H_DF, 1, tq), F32), pltpu.VMEM((2 * H_DF, 1, tq), F32),
                        pltpu.VMEM((2 * H_DF, 2 * D_DF, tq), F32)],
    )
    return pl.pallas_call(
        functools.partial(_df_kernel, tq=tq, lam_init=lam_init),
        grid_spec=grid_spec,
        out_shape=jax.ShapeDtypeStruct((batch * seq, W_DF), F32),
        compiler_params=_params(("parallel", "arbitrary")),
        name="df_attention",
    )(qi, kj, q, k, v, bias_tiles, lam_vecs, subln_g)


CONV_HIST = CONV_W - 1
CONV_PAD = 32
CONV_CHUNK = 64


def _bf16_round(x):
    return x.astype(BF16).astype(F32)


def _conv_kernel(glu_ref, w_ref, cb_ref, g_ref, b_ref, o_ref, st_ref, ubuf_ref, tail_ref, *, tt):
    i = pl.program_id(1)

    @pl.when(i == 0)
    def _zero_history():
        ubuf_ref[0:CONV_PAD, :] = jnp.zeros((CONV_PAD, C_CONV), F32)

    @pl.when(i > 0)
    def _carry_history():
        ubuf_ref[0:CONV_PAD, :] = ubuf_ref[tt:tt + CONV_PAD, :]

    u = glu_ref[:, :C_CONV] * jax.nn.sigmoid(glu_ref[:, C_CONV:])
    ubuf_ref[CONV_PAD:CONV_PAD + tt, :] = _bf16_round(u)
    taps = [_bf16_round(w_ref[j:j + 1, :]) for j in range(CONV_W)]
    first = CONV_PAD - CONV_HIST
    for r0 in range(0, tt, CONV_CHUNK):
        acc = jnp.zeros((CONV_CHUNK, C_CONV), F32)
        for j in range(CONV_W):
            acc = acc + ubuf_ref[first + r0 + j:first + r0 + j + CONV_CHUNK, :] * taps[j]
        y = _layer_norm(acc + cb_ref[...], g_ref[...], b_ref[...])
        o_ref[r0:r0 + CONV_CHUNK, :] = _silu(y)

    @pl.when(i == pl.num_programs(1) - 1)
    def _final_state():
        tail_ref[...] = u[tt - CONV_PAD:, :]
        st_ref[...] = tail_ref[CONV_PAD - CONV_HIST:, :]


def _conv_branch(glu, conv_w, conv_b, g, b, batch, seq, tt):
    nt = seq // tt
    vec = lambda: pl.BlockSpec((1, C_CONV), lambda bb, i: (0, 0))
    return pl.pallas_call(
        functools.partial(_conv_kernel, tt=tt),
        grid=(batch, nt),
        in_specs=[pl.BlockSpec((tt, 2 * C_CONV), lambda bb, i: (bb * nt + i, 0)),
                  pl.BlockSpec((CONV_W, C_CONV), lambda bb, i: (0, 0)),
                  vec(), vec(), vec()],
        out_specs=[pl.BlockSpec((tt, C_CONV), lambda bb, i: (bb * nt + i, 0)),
                   pl.BlockSpec((None, CONV_HIST, C_CONV), lambda bb, i: (bb, 0, 0))],
        out_shape=[jax.ShapeDtypeStruct((batch * seq, C_CONV), F32),
                   jax.ShapeDtypeStruct((batch, CONV_HIST, C_CONV), F32)],
        scratch_shapes=[pltpu.VMEM((CONV_PAD + tt, C_CONV), F32), pltpu.VMEM((CONV_PAD, C_CONV), F32)],
        compiler_params=_params(("parallel", "arbitrary")),
        name="conv_branch",
    )(glu, conv_w, conv_b, g, b)


def _conv_dec_kernel(glu_ref, st_ref, w_ref, cb_ref, g_ref, b_ref, o_ref, u_ref):
    u = glu_ref[:, :C_CONV] * jax.nn.sigmoid(glu_ref[:, C_CONV:])
    acc = u * w_ref[CONV_HIST:CONV_W, :]
    for j in range(CONV_HIST):
        acc = acc + st_ref[j] * w_ref[j:j + 1, :]
    y = _layer_norm(acc + cb_ref[...], g_ref[...], b_ref[...])
    o_ref[...] = _silu(y)
    u_ref[...] = u


def _conv_decode(glu, state_t, conv_w, conv_b, g, b):
    n = glu.shape[0]
    return pl.pallas_call(
        _conv_dec_kernel,
        out_shape=[jax.ShapeDtypeStruct((n, C_CONV), F32), jax.ShapeDtypeStruct((n, C_CONV), F32)],
        name="conv_decode",
    )(glu, state_t, conv_w, conv_b, g, b)


def _merge_kernel(x_ref, osb_ref, odf_ref, oc_ref, gate_ref, wsb_ref, wdf_ref, wc_ref, wo_ref,
                  g_ref, b_ref, o_ref, *, alpha):
    merged = (jax.nn.sigmoid(gate_ref[:, 0:D_MODEL]) * _dot(osb_ref[...].astype(BF16), wsb_ref[...])
              + jax.nn.sigmoid(gate_ref[:, D_MODEL:2 * D_MODEL]) * _dot(odf_ref[...].astype(BF16), wdf_ref[...])
              + jax.nn.sigmoid(gate_ref[:, 2 * D_MODEL:]) * _dot(oc_ref[...].astype(BF16), wc_ref[...]))
    m = _dot(merged.astype(BF16), wo_ref[...])
    o_ref[...] = _layer_norm(alpha * x_ref[...] + m, g_ref[...], b_ref[...])


def _merge(x, o_sb, o_df, o_c, gates, w_sb, w_df, w_c, w_o, g, b, tm, alpha):
    n = x.shape[0]
    rows = lambda w: pl.BlockSpec((tm, w), lambda i: (i, 0))
    full = lambda a: pl.BlockSpec(a.shape, lambda i: (0, 0))
    return pl.pallas_call(
        functools.partial(_merge_kernel, alpha=alpha),
        grid=(n // tm,),
        in_specs=[rows(D_MODEL), rows(W_SB), rows(W_DF), rows(C_CONV), rows(3 * D_MODEL),
                  full(w_sb), full(w_df), full(w_c), full(w_o), full(g), full(b)],
        out_specs=rows(D_MODEL),
        out_shape=jax.ShapeDtypeStruct((n, D_MODEL), F32),
        compiler_params=_params(("parallel",)),
        name="merge_ln",
    )(x, o_sb, o_df, o_c, gates, w_sb, w_df, w_c, w_o, g, b)


def _first_argmax(vals, ids, axis, sentinel):
    mx = jnp.max(vals, axis=axis, keepdims=True)
    ix = jnp.min(jnp.where(vals == mx, ids, sentinel), axis=axis, keepdims=True)
    return mx, ix


def _router_kernel(x_ref, rw_ref, rb_ref, before_ref, idx_ref, w_ref, rank_ref, count_ref, seen_ref, *, tm):
    @pl.when(pl.program_id(0) == 0)
    def _init():
        seen_ref[...] = jnp.zeros_like(seen_ref)

    logits = _dot_nt(rw_ref[...], x_ref[...].astype(BF16))
    scores = jax.nn.sigmoid(logits)
    biased = scores + rb_ref[...]
    sub = lax.broadcasted_iota(jnp.int32, (GROUP_SIZE, tm), 0).astype(F32)
    gscore = jnp.zeros((N_GROUPS, tm), F32)
    for g in range(N_GROUPS):
        blk = biased[g * GROUP_SIZE:(g + 1) * GROUP_SIZE, :]
        m1, i1 = _first_argmax(blk, sub, 0, float(GROUP_SIZE))
        m2 = jnp.max(jnp.where(sub == i1, NEG_INF, blk), axis=0, keepdims=True)
        gscore = jnp.where(sub == float(g), m1 + m2, gscore)
    gmask = jnp.zeros((N_GROUPS, tm), F32)
    for _ in range(TOPK_GROUPS):
        _, ig = _first_argmax(gscore, sub, 0, float(N_GROUPS))
        sel = sub == ig
        gmask = jnp.where(sel, 1.0, gmask)
        gscore = jnp.where(sel, NEG_INF, gscore)
    eid = lax.broadcasted_iota(jnp.int32, (N_EXPERTS, tm), 0).astype(F32)
    cand = jnp.concatenate(
        [jnp.where(gmask[g:g + 1, :] > 0.0, biased[g * GROUP_SIZE:(g + 1) * GROUP_SIZE, :], NEG_INF)
         for g in range(N_GROUPS)], axis=0)
    total = jnp.zeros((1, tm), F32)
    picked, chosen = [], []
    member = jnp.zeros((N_EXPERTS, tm), F32)
    for k in range(TOP_K):
        _, ie = _first_argmax(cand, eid, 0, float(N_EXPERTS))
        sel = eid == ie
        wk = jnp.sum(jnp.where(sel, scores, 0.0), axis=0, keepdims=True)
        cand = jnp.where(sel, NEG_INF, cand)
        member = jnp.where(sel, 1.0, member)
        idx_ref[k:k + 1, :] = ie.astype(jnp.int32)
        picked.append(wk)
        chosen.append(ie)
        total = total + wk
    for k in range(TOP_K):
        w_ref[k:k + 1, :] = picked[k] / total * ROUTE_SCALE
    earlier = _dot(member.astype(BF16), before_ref[...]) + seen_ref[...]
    for k in range(TOP_K):
        rank_ref[k:k + 1, :] = jnp.sum(jnp.where(eid == chosen[k], earlier, 0.0), axis=0,
                                       keepdims=True).astype(jnp.int32)
    seen_ref[...] = seen_ref[...] + jnp.sum(member, axis=1, keepdims=True)
    count_ref[...] = seen_ref[...].astype(jnp.int32)


def _router(x, rw_t, rb_col, tm):
    n = x.shape[0]
    r = lax.broadcasted_iota(jnp.int32, (tm, tm), 0)
    c = lax.broadcasted_iota(jnp.int32, (tm, tm), 1)
    before = (r < c).astype(BF16)
    pairs = lambda: pl.BlockSpec((TOP_K, tm), lambda i: (0, i))
    return pl.pallas_call(
        functools.partial(_router_kernel, tm=tm),
        grid=(n // tm,),
        in_specs=[pl.BlockSpec((tm, D_MODEL), lambda i: (i, 0)),
                  pl.BlockSpec((N_EXPERTS, D_MODEL), lambda i: (0, 0)),
                  pl.BlockSpec((N_EXPERTS, 1), lambda i: (0, 0)),
                  pl.BlockSpec((tm, tm), lambda i: (0, 0))],
        out_specs=[pairs(), pairs(), pairs(), pl.BlockSpec((N_EXPERTS, 1), lambda i: (0, 0))],
        out_shape=[jax.ShapeDtypeStruct((TOP_K, n), jnp.int32),
                   jax.ShapeDtypeStruct((TOP_K, n), F32),
                   jax.ShapeDtypeStruct((TOP_K, n), jnp.int32),
                   jax.ShapeDtypeStruct((N_EXPERTS, 1), jnp.int32)],
        scratch_shapes=[pltpu.VMEM((N_EXPERTS, 1), F32)],
        compiler_params=_params(("arbitrary",)),
        name="router",
    )(x, rw_t, rb_col, before)


def _wait_row_gather(src_hbm, dst_ref, sem):
    pltpu.make_async_copy(src_hbm.at[pl.ds(0, dst_ref.shape[0])], dst_ref, sem).wait()


def _dispatch_tables(top_idx_t, rank_t, counts, br):
    k, n = top_idx_t.shape
    counts = counts.reshape(N_EXPERTS)
    padded = (counts + br - 1) // br * br
    pend = jnp.cumsum(padded)
    pstart = pend - padded
    experts = jnp.arange(N_EXPERTS, dtype=jnp.int32)
    dest = rank_t + jnp.sum(jnp.where(top_idx_t[:, :, None] == experts, pstart, 0), axis=-1)
    n_blocks = -(-(k * n) // br) + N_EXPERTS
    first_row = jnp.arange(n_blocks, dtype=jnp.int32)[:, None] * br
    blk_expert = jnp.minimum(jnp.sum((pend[None, :] <= first_row).astype(jnp.int32), axis=1), N_EXPERTS - 1)
    n_used = (pend[-1] // br).astype(jnp.int32).reshape(1)
    return dest.astype(jnp.int32), blk_expert.astype(jnp.int32), n_used, pend.astype(jnp.int32), n_blocks


def _dispatch_kernel(dest_ref, pend_ref, nused_ref, x_ref, xs_hbm, zbuf, sem, zsem, *, tm, br, n_blocks):
    i = pl.program_id(0)
    rows = TOP_K * tm

    @pl.when(i == 0)
    def _zero_fill():
        zbuf[...] = jnp.zeros_like(zbuf)

        def fill(first_row):
            return pltpu.make_async_copy(zbuf, xs_hbm.at[pl.ds(pl.multiple_of(first_row, br), br)], zsem)

        def has_rows(e):
            return pend_ref[e] > (pend_ref[e - 1] if e else 0)

        for e in range(N_EXPERTS):
            @pl.when(has_rows(e))
            def _start(e=e):
                fill(pend_ref[e] - br).start()

        def start_tail(b, carry):
            fill(b * br).start()
            return carry
        lax.fori_loop(nused_ref[0], n_blocks, start_tail, 0)

        for e in range(N_EXPERTS):
            @pl.when(has_rows(e))
            def _wait(e=e):
                fill(pend_ref[e] - br).wait()

        def wait_tail(b, carry):
            fill(b * br).wait()
            return carry
        lax.fori_loop(nused_ref[0], n_blocks, wait_tail, 0)

    def wait_one_tile():
        pltpu.make_async_copy(xs_hbm.at[pl.ds(0, rows)], xs_hbm.at[pl.ds(0, rows)], sem).wait()

    def group(g, carry):
        base = pl.multiple_of(g * SUBLANES, SUBLANES)
        for u in range(SUBLANES):
            for k in range(TOP_K):
                row = dest_ref[k, i * tm + base + u]
                pltpu.make_async_copy(x_ref.at[pl.ds(base + u, 1)], xs_hbm.at[pl.ds(row, 1)], sem).start()
        return carry
    lax.fori_loop(0, tm // SUBLANES, group, 0)

    wait_one_tile()


def _dispatch(x, dest, pend, n_used, n_blocks, tm, br):
    n = x.shape[0]
    grid_spec = pltpu.PrefetchScalarGridSpec(
        num_scalar_prefetch=3,
        grid=(n // tm,),
        in_specs=[pl.BlockSpec((tm, D_MODEL), lambda i, d, pe, nu: (i, 0))],
        out_specs=pl.BlockSpec(memory_space=pl.ANY),
        scratch_shapes=[pltpu.VMEM((br, D_MODEL), F32), pltpu.SemaphoreType.DMA(()), pltpu.SemaphoreType.DMA(())],
    )
    return pl.pallas_call(
        functools.partial(_dispatch_kernel, tm=tm, br=br, n_blocks=n_blocks),
        grid_spec=grid_spec,
        out_shape=jax.ShapeDtypeStruct((n_blocks * br, D_MODEL), F32),
        compiler_params=_params(("arbitrary",)),
        name="dispatch",
    )(dest, pend, n_used, x)


def _expert_kernel(be_ref, nused_ref, xs_ref, wg_ref, wu_ref, wd_ref, o_ref, wg_b, wu_b, wd_b):
    b = pl.program_id(0)
    new_expert = (b == 0) | (be_ref[b] != be_ref[jnp.maximum(b - 1, 0)])

    @pl.when(new_expert & (b < nused_ref[0]))
    def _round_weights():
        wg_b[...] = wg_ref[...].astype(BF16)
        wu_b[...] = wu_ref[...].astype(BF16)
        wd_b[...] = wd_ref[...].astype(BF16)

    @pl.when(b < nused_ref[0])
    def _compute():
        x = xs_ref[...].astype(BF16)
        hdn = _silu(_dot(x, wg_b[...])) * _dot(x, wu_b[...])
        o_ref[...] = _dot(hdn.astype(BF16), wd_b[...])

    @pl.when(b >= nused_ref[0])
    def _unused():
        o_ref[...] = jnp.zeros_like(o_ref)


def _experts(xs, blk_expert, n_used, expert_weights, br):
    wg, wu, wd, layer = expert_weights
    n_blocks = blk_expert.shape[0]
    used = lambda b, nu: jnp.maximum(jnp.minimum(b, nu[0] - 1), 0)
    grid_spec = pltpu.PrefetchScalarGridSpec(
        num_scalar_prefetch=2,
        grid=(n_blocks,),
        in_specs=[pl.BlockSpec((br, D_MODEL), lambda b, be, nu: (used(b, nu), 0)),
                  pl.BlockSpec((None, None, D_MODEL, D_EXPERT), lambda b, be, nu: (layer, be[b], 0, 0)),
                  pl.BlockSpec((None, None, D_MODEL, D_EXPERT), lambda b, be, nu: (layer, be[b], 0, 0)),
                  pl.BlockSpec((None, None, D_EXPERT, D_MODEL), lambda b, be, nu: (layer, be[b], 0, 0))],
        out_specs=pl.BlockSpec((br, D_MODEL), lambda b, be, nu: (b, 0)),
        scratch_shapes=[pltpu.VMEM((D_MODEL, D_EXPERT), BF16), pltpu.VMEM((D_MODEL, D_EXPERT), BF16),
                        pltpu.VMEM((D_EXPERT, D_MODEL), BF16)],
    )
    return pl.pallas_call(
        _expert_kernel,
        grid_spec=grid_spec,
        out_shape=jax.ShapeDtypeStruct((n_blocks * br, D_MODEL), F32),
        compiler_params=_params(("arbitrary",)),
        name="experts",
    )(blk_expert, n_used, xs, wg, wu, wd)


def _ffn_out_kernel(dest_ref, x_ref, tw_ref, out_hbm, sg_ref, su_ref, sd_ref, g_ref, b_ref, o_ref, ybuf, sem,
                    *, alpha, tm):
    i = pl.program_id(0)
    slot = i % 2

    def gather(tile, into):
        def group(g, carry):
            base = pl.multiple_of(g * SUBLANES, SUBLANES)
            for u in range(SUBLANES):
                for k in range(TOP_K):
                    row = dest_ref[k, tile * tm + base + u]
                    pltpu.make_async_copy(out_hbm.at[pl.ds(row, 1)], ybuf.at[into, k, pl.ds(base + u, 1)],
                                          sem.at[into]).start()
            return carry
        lax.fori_loop(0, tm // SUBLANES, group, 0)

    @pl.when(i == 0)
    def _first():
        gather(0, 0)

    @pl.when(i + 1 < pl.num_programs(0))
    def _prefetch():
        gather(i + 1, 1 - slot)

    for k in range(TOP_K):
        _wait_row_gather(out_hbm, ybuf.at[slot, k], sem.at[slot])
    x = x_ref[...]
    xb = x.astype(BF16)
    hdn = _silu(_dot(xb, sg_ref[...])) * _dot(xb, su_ref[...])
    y = ybuf[slot, 0] * tw_ref[:, 0:1]
    for k in range(1, TOP_K):
        y = y + ybuf[slot, k] * tw_ref[:, k:k + 1]
    y = y + _dot(hdn.astype(BF16), sd_ref[...])
    o_ref[...] = _layer_norm(alpha * x + y, g_ref[...], b_ref[...])


def _ffn_out(x, expert_out, dest, top_w, sg, su, sd, g, b, tm, alpha):
    n = x.shape[0]
    rows = lambda w: pl.BlockSpec((tm, w), lambda i, d: (i, 0))
    full = lambda a: pl.BlockSpec(a.shape, lambda i, d: (0, 0))
    grid_spec = pltpu.PrefetchScalarGridSpec(
        num_scalar_prefetch=1,
        grid=(n // tm,),
        in_specs=[rows(D_MODEL), rows(TOP_K), pl.BlockSpec(memory_space=pl.ANY),
                  full(sg), full(su), full(sd), full(g), full(b)],
        out_specs=rows(D_MODEL),
        scratch_shapes=[pltpu.VMEM((2, TOP_K, tm, D_MODEL), F32), pltpu.SemaphoreType.DMA((2,))],
    )
    return pl.pallas_call(
        functools.partial(_ffn_out_kernel, alpha=alpha, tm=tm),
        grid_spec=grid_spec,
        out_shape=jax.ShapeDtypeStruct((n, D_MODEL), F32),
        compiler_params=_params(("arbitrary",)),
        name="ffn_out_ln",
    )(dest, x, top_w, expert_out, sg, su, sd, g, b)


def _moe(x1, f, tm, br, alpha):
    router_w, router_b, expert_weights, sg, su, sd, ln_g, ln_b = f
    top_idx_t, top_w_t, rank_t, counts = _router(x1, router_w, router_b, tm)
    dest, blk_expert, n_used, pend, n_blocks = _dispatch_tables(top_idx_t, rank_t, counts, br)
    xs = _dispatch(x1, dest, pend, n_used, n_blocks, tm, br)
    out = _experts(xs, blk_expert, n_used, expert_weights, br)
    return _ffn_out(x1, out, dest, top_w_t.T, sg, su, sd, ln_g, ln_b, min(tm, COMBINE_TM), alpha)


def _sb_dec_kernel(pt_ref, q_ref, *refs, pc):
    k_refs, v_refs = refs[:pc], refs[pc:2 * pc]
    tri_ref, o_ref, carry_ref, acc_ref = refs[2 * pc:]
    c = pl.program_id(1)

    @pl.when(c == 0)
    def _init():
        carry_ref[...] = jnp.zeros_like(carry_ref)
        acc_ref[...] = jnp.zeros_like(acc_ref)

    row = lax.broadcasted_iota(jnp.int32, (SUBLANES, W_SB), 0)
    col = lax.broadcasted_iota(jnp.int32, (SUBLANES, W_SB), 1)
    own = (col // D_SB) == row
    qbd = jnp.where(own, jnp.broadcast_to(q_ref[...], (SUBLANES, W_SB)), 0.0).astype(BF16)
    carry = carry_ref[...]
    acc = acc_ref[...]
    z = jnp.concatenate([_dot(qbd, k_refs[p][...].astype(BF16)) for p in range(pc)], axis=0) * QK_SCALE
    log_beta, log_keep = _log_sigmoid_pair(z)
    st = _split_dot(log_keep, tri_ref[...], terms=3)
    for p in range(pc):
        rows = slice(p * SUBLANES, (p + 1) * SUBLANES)
        w = jnp.exp(log_beta[rows] + st[rows, :LANES] + carry)
        acc = acc + _dot_nt(w.astype(BF16), v_refs[p][...].astype(BF16))
        carry = carry + st[rows, LANES:]
    carry_ref[...] = carry
    acc_ref[...] = acc

    @pl.when(c == pl.num_programs(1) - 1)
    def _fin():
        o_ref[...] = jnp.sum(jnp.where(own, acc, 0.0), axis=0, keepdims=True)


def _sb_decode(q, cache_kt, cache_vt, layer, page_table, pc):
    nb, n_pages = page_table.shape

    def page_spec(p):
        def index(b, c, pt):
            return (layer, pt[b * n_pages + n_pages - 1 - (c * pc + p)], 0, 0)
        return pl.BlockSpec((None, None, W_SB, PAGE_SIZE), index)

    row = lambda w: pl.BlockSpec((None, 1, w), lambda b, c, pt: (b, 0, 0))
    grid_spec = pltpu.PrefetchScalarGridSpec(
        num_scalar_prefetch=1,
        grid=(nb, n_pages // pc),
        in_specs=([row(W_SB)]
                  + [page_spec(p) for p in range(pc)]
                  + [page_spec(p) for p in range(pc)]
                  + [pl.BlockSpec((LANES, 2 * LANES), lambda b, c, pt: (0, 0))]),
        out_specs=row(W_SB),
        scratch_shapes=[pltpu.VMEM((SUBLANES, LANES), F32), pltpu.VMEM((SUBLANES, W_SB), F32)],
    )
    out = pl.pallas_call(
        functools.partial(_sb_dec_kernel, pc=pc),
        grid_spec=grid_spec,
        out_shape=jax.ShapeDtypeStruct((nb, 1, W_SB), F32),
        compiler_params=_params(("parallel", "arbitrary")),
        name="sb_decode",
    )(page_table.reshape(-1), q.reshape(nb, 1, W_SB), *([cache_kt] * pc), *([cache_vt] * pc), _suffix_matrix())
    return out.reshape(nb, W_SB)


def _df_dec_kernel(pt_ref, q_ref, kn_ref, vn_ref, bias_ref, bself_ref, lam_ref, g_ref, spread_ref, *refs,
                   pc, n_pages, lam_init):
    k_refs, v_refs = refs[:pc], refs[pc:2 * pc]
    o_ref, s_ref, m_ref, aself_ref, acc_ref = refs[2 * pc:]
    c = pl.program_id(1)
    nc = n_pages // pc

    @pl.when(c == 0)
    def _init():
        m_ref[...] = jnp.full_like(m_ref, NEG_INF)
        acc_ref[...] = jnp.zeros_like(acc_ref)

    def page_cols(page):
        return pl.ds(pl.multiple_of(page * PAGE_SIZE, PAGE_SIZE), PAGE_SIZE)

    @pl.when(c < nc)
    def _scores():
        row = lax.broadcasted_iota(jnp.int32, (SUBLANES, W_DF), 0)
        col = lax.broadcasted_iota(jnp.int32, (SUBLANES, W_DF), 1)
        own = (col // D_DF) == 2 * (row % H_DF) + row // H_DF
        qbd = jnp.where(own, jnp.broadcast_to(q_ref[...], (SUBLANES, W_DF)), 0.0).astype(BF16)
        m = m_ref[...]
        for p in range(pc):
            cols = page_cols(c * pc + p)
            s = _dot(qbd, k_refs[p][...].astype(BF16)) * (D_DF ** -0.5) + bias_ref[:, cols]
            s_ref[:, cols] = s
            m = jnp.maximum(m, jnp.max(s, axis=-1, keepdims=True))
        m_ref[...] = m

        @pl.when(c == nc - 1)
        def _weights():
            kn = jnp.broadcast_to(_bf16_round(kn_ref[...]), (SUBLANES, W_DF))
            s_self = jnp.sum(qbd.astype(F32) * kn, axis=-1, keepdims=True) * (D_DF ** -0.5) + bself_ref[...]
            m_fin = jnp.maximum(m, s_self)
            pr = jnp.exp(s_ref[...] - m_fin)
            p_self = jnp.exp(s_self - m_fin)
            total = jnp.sum(pr, axis=-1, keepdims=True) + p_self
            lam = _lambda(lam_ref[0:1, :], lam_ref[1:2, :], lam_ref[2:3, :], lam_ref[3:4, :], lam_init)
            pn = pr / total
            pn_self = jnp.broadcast_to(p_self / total, (SUBLANES, LANES))
            s_ref[...] = pn - lam * pltpu.roll(pn, shift=H_DF, axis=0)
            aself_ref[...] = pn_self - lam * pltpu.roll(pn_self, shift=H_DF, axis=0)

    @pl.when(c >= nc)
    def _values():
        rows = PAGE_SIZE * H_DF
        row = lax.broadcasted_iota(jnp.int32, (SUBLANES, rows), 0)
        col = lax.broadcasted_iota(jnp.int32, (SUBLANES, rows), 1)
        own = (col % H_DF) == row
        acc = acc_ref[...]
        weights = jnp.concatenate([s_ref[:, page_cols((c - nc) * pc + p)] for p in range(pc)], axis=0)
        spread = _dot(weights.astype(BF16), spread_ref[...])
        for p in range(pc):
            own_head = jnp.where(own, spread[p * SUBLANES:(p + 1) * SUBLANES], 0.0)
            acc = acc + _dot(own_head.astype(BF16), v_refs[p][...].astype(BF16))
        acc_ref[...] = acc

        @pl.when(c == 2 * nc - 1)
        def _fin():
            o = acc[0:H_DF] + _bf16_round(aself_ref[0:H_DF, 0:1]) * _bf16_round(vn_ref[...])
            o_ref[...] = _sub_norm(o, g_ref[...], 1.0 - lam_init)


def _df_decode(q, k_new, v_new, cache_kt, cache_v, layer, page_table, bias_past, bias_self, lam_vecs, subln_g,
               pc, lam_init):
    nb, n_pages = page_table.shape
    nc = n_pages // pc
    head_w = 2 * D_DF
    row = lambda w: pl.BlockSpec((None, 1, w), lambda b, c, pt: (b, 0, 0))
    heads = pl.BlockSpec((None, H_DF, head_w), lambda b, c, pt: (b, 0, 0))
    full = lambda a: pl.BlockSpec(a.shape, lambda b, c, pt: (0,) * a.ndim)

    def page_spec(p, second_pass):
        def index(b, c, pt):
            chunk = jnp.maximum(c - nc, 0) if second_pass else jnp.minimum(c, nc - 1)
            return (layer, pt[b * n_pages + chunk * pc + p], 0, 0)
        shape = (None, None, PAGE_SIZE * H_DF, head_w) if second_pass else (None, None, W_DF, PAGE_SIZE)
        return pl.BlockSpec(shape, index)

    r = lax.broadcasted_iota(jnp.int32, (PAGE_SIZE, PAGE_SIZE * H_DF), 0)
    c = lax.broadcasted_iota(jnp.int32, (PAGE_SIZE, PAGE_SIZE * H_DF), 1)
    spread = (c // H_DF == r).astype(BF16)
    grid_spec = pltpu.PrefetchScalarGridSpec(
        num_scalar_prefetch=1,
        grid=(nb, 2 * nc),
        in_specs=([row(W_DF), row(W_DF), heads, full(bias_past), full(bias_self), full(lam_vecs),
                   full(subln_g), full(spread)]
                  + [page_spec(p, False) for p in range(pc)]
                  + [page_spec(p, True) for p in range(pc)]),
        out_specs=heads,
        scratch_shapes=[pltpu.VMEM((SUBLANES, n_pages * PAGE_SIZE), F32), pltpu.VMEM((SUBLANES, 1), F32),
                        pltpu.VMEM((SUBLANES, LANES), F32), pltpu.VMEM((SUBLANES, head_w), F32)],
    )
    r3 = lambda a: a.reshape(nb, 1, W_DF)
    out = pl.pallas_call(
        functools.partial(_df_dec_kernel, pc=pc, n_pages=n_pages, lam_init=lam_init),
        grid_spec=grid_spec,
        out_shape=jax.ShapeDtypeStruct((nb, H_DF, head_w), F32),
        compiler_params=_params(("parallel", "arbitrary")),
        name="df_decode",
    )(page_table.reshape(-1), r3(q), r3(k_new), v_new.reshape(nb, H_DF, head_w), bias_past, bias_self, lam_vecs,
      subln_g, spread, *([cache_kt] * pc), *([cache_v] * pc))
    return out.reshape(nb, W_DF)


PROMPT_TM = 256
PROMPT_ATT_BLOCK = 256
PROMPT_CONV_TILE = 512
PROMPT_MOE_BLOCK = 256
COMBINE_TM = 128
SAMPLE_MOE_BLOCK = 32
DECODE_PAGES_PER_STEP = 8


def kernel(x_prompt, x_sample, cache_sb_k, cache_sb_v, cache_df_k, cache_df_v, state_conv, page_table,
           w_in, rel_bias_table, lam_q1, lam_k1, lam_q2, lam_k2, subln_g, conv_w, conv_b, conv_ln_g,
           conv_ln_b, w_sb_out, w_df_out, w_conv_out, w_o, ln1_g, ln1_b, router_w, router_bias, w_gate,
           w_up, w_down, sh_gate, sh_up, sh_down, ln2_g, ln2_b):
    depth = w_in.shape[0]
    batch, seq, _ = x_prompt.shape
    nb = x_sample.shape[0]
    n_pages = page_table.shape[1]
    past_len = n_pages * PAGE_SIZE
    alpha = (2 * depth) ** 0.25
    n_phys = cache_sb_k.shape[1]

    xp = x_prompt.reshape(batch * seq, D_MODEL)
    xs = x_sample.reshape(nb, D_MODEL)
    bias_tiles = _bias_tiles(rel_bias_table, PROMPT_ATT_BLOCK)
    dist = past_len - jnp.arange(past_len, dtype=jnp.int32)
    bias_past = jnp.tile(_bias_of_distance(rel_bias_table, dist).T, (2, 1))
    bias_self = jnp.tile(_bias_of_distance(rel_bias_table, jnp.zeros((1,), jnp.int32)).T, (2, 1))
    row = lambda a: a.reshape(1, -1)
    sb_kt = jnp.transpose(cache_sb_k, (0, 1, 3, 4, 2)).reshape(depth, n_phys, W_SB, PAGE_SIZE)
    sb_vt = jnp.transpose(cache_sb_v, (0, 1, 3, 4, 2)).reshape(depth, n_phys, W_SB, PAGE_SIZE)
    df_kt = jnp.transpose(cache_df_k, (0, 1, 3, 4, 5, 2)).reshape(depth, n_phys, W_DF, PAGE_SIZE)
    df_v = cache_df_v.reshape(depth, n_phys, PAGE_SIZE * H_DF, 2 * D_DF)

    new_p, new_s = [], []
    for l in range(depth):
        lam_init = 0.8 - 0.6 * math.exp(-0.3 * l)
        w_in_b = w_in[l].astype(BF16)
        lam_vecs = jnp.stack([lam_q1[l], lam_k1[l], lam_q2[l], lam_k2[l]])
        merge_w = (w_sb_out[l].astype(BF16), w_df_out[l].astype(BF16), w_conv_out[l].astype(BF16),
                   w_o[l].astype(BF16), row(ln1_g[l]), row(ln1_b[l]))
        ffn = (router_w[l].T.astype(BF16), router_bias[l].reshape(N_EXPERTS, 1),
               (w_gate, w_up, w_down, l),
               sh_gate[l].astype(BF16), sh_up[l].astype(BF16), sh_down[l].astype(BF16),
               row(ln2_g[l]), row(ln2_b[l]))
        conv_p = (conv_w[l], row(conv_b[l]), row(conv_ln_g[l]), row(conv_ln_b[l]))

        q_sb, k_sb, v_sb, q_df, k_df, v_df, glu, gates = _inproj(xp, w_in_b, PROMPT_TM)
        o_sb = _sb_attention(q_sb, k_sb, v_sb, batch, seq, PROMPT_ATT_BLOCK)
        o_df = _df_attention(q_df, k_df, v_df, bias_tiles, lam_vecs, subln_g[l].reshape(2 * D_DF, 1), batch, seq,
                             PROMPT_ATT_BLOCK, lam_init)
        o_c, p_conv = _conv_branch(glu, *conv_p, batch, seq, PROMPT_CONV_TILE)
        x1 = _merge(xp, o_sb, o_df, o_c, gates, *merge_w, PROMPT_TM, alpha)
        xp = _moe(x1, ffn, PROMPT_TM, PROMPT_MOE_BLOCK, alpha)
        new_p.append((k_sb, v_sb, k_df, v_df, p_conv))

        q_sb, k_sb, v_sb, q_df, k_df, v_df, glu, gates = _inproj(xs, w_in_b, nb)
        o_sb = _sb_decode(q_sb, sb_kt, sb_vt, l, page_table, DECODE_PAGES_PER_STEP)
        o_df = _df_decode(q_df, k_df, v_df, df_kt, df_v, l, page_table, bias_past, bias_self,
                          lam_vecs, row(subln_g[l]), DECODE_PAGES_PER_STEP, lam_init)
        o_c, u_new = _conv_decode(glu, jnp.transpose(state_conv[l], (1, 0, 2)), *conv_p)
        s_conv = jnp.concatenate([state_conv[l][:, 1:], u_new[:, None, :]], axis=1)
        x1 = _merge(xs, o_sb, o_df, o_c, gates, *merge_w, nb, alpha)
        xs = _moe(x1, ffn, nb, SAMPLE_MOE_BLOCK, alpha)
        new_s.append((k_sb, v_sb, k_df, v_df, s_conv))

    def stacked(rows, i, shape):
        return jnp.stack([r[i] for r in rows]).reshape((depth,) + shape)

    return (xp.reshape(batch, seq, D_MODEL),
            xs.reshape(nb, 1, D_MODEL),
            stacked(new_p, 0, (batch, seq, H_SB, D_SB)),
            stacked(new_p, 1, (batch, seq, H_SB, D_SB)),
            stacked(new_p, 2, (batch, seq, H_DF, 2, D_DF)),
            stacked(new_p, 3, (batch, seq, H_DF, 2 * D_DF)),
            stacked(new_p, 4, (batch, CONV_HIST, C_CONV)),
            stacked(new_s, 0, (nb, 1, H_SB, D_SB)),
            stacked(new_s, 1, (nb, 1, H_SB, D_SB)),
            stacked(new_s, 2, (nb, 1, H_DF, 2, D_DF)),
            stacked(new_s, 3, (nb, 1, H_DF, 2 * D_DF)),
            stacked(new_s, 4, (nb, CONV_HIST, C_CONV)))
```

```python
import functools
import math

import jax
import jax.numpy as jnp
from jax import lax
from jax.experimental import pallas as pl
from jax.experimental.pallas import tpu as pltpu

F32 = jnp.float32
BF16 = jnp.bfloat16

D_MODEL = 1024
H_SB, D_SB = 4, 64
H_DF, D_DF = 4, 64
C_CONV = D_MODEL // 4
CONV_W = 31
N_BUCKETS, MAX_DIST = 32, 128
N_EXPERTS, TOP_K, N_GROUPS, TOPK_GROUPS = 64, 8, 8, 4
GROUP_SIZE = N_EXPERTS // N_GROUPS
D_EXPERT = D_MODEL // 4
D_SHARED = D_MODEL // 4
ROUTE_SCALE = 2.5
LN_EPS = 1e-5
PAGE_SIZE = 128
W_SB = H_SB * D_SB
W_DF = H_DF * 2 * D_DF
IN_SIZES = (W_SB, W_SB, W_SB, W_DF, W_DF, W_DF, 2 * C_CONV, 3 * D_MODEL)
IN_WIDTH = sum(IN_SIZES)

LANES = 128
SUBLANES = 8
VMEM_LIMIT = 56 * 1024 * 1024
NEG_INF = float("-inf")
NT_DIMS = (((1,), (1,)), ((), ()))
QK_SCALE = D_SB ** -0.5
assert D_SB == D_DF and QK_SCALE == 0.125


def _params(semantics, vmem=VMEM_LIMIT):
    return pltpu.CompilerParams(dimension_semantics=semantics, vmem_limit_bytes=vmem)


def _dot(a, b):
    return jnp.dot(a, b, preferred_element_type=F32)


def _dot_nt(a, b):
    return lax.dot_general(a, b, NT_DIMS, preferred_element_type=F32)


def _layer_norm(y, g, b):
    mu = jnp.mean(y, axis=-1, keepdims=True)
    d = y - mu
    var = jnp.mean(d * d, axis=-1, keepdims=True)
    return d * lax.rsqrt(var + LN_EPS) * g + b


def _silu(x):
    return x * jax.nn.sigmoid(x)


def _inproj_kernel(x_ref, w_ref, *out_refs):
    x = x_ref[...].astype(BF16)
    off = 0
    for ref, width in zip(out_refs, IN_SIZES):
        for c in range(0, width, 512):
            cw = min(512, width - c)
            ref[:, c:c + cw] = _dot(x, w_ref[:, off + c:off + c + cw])
        off += width


def _inproj(x, w_bf16, tm):
    n = x.shape[0]
    return pl.pallas_call(
        _inproj_kernel,
        grid=(n // tm,),
        in_specs=[pl.BlockSpec((tm, D_MODEL), lambda i: (i, 0)),
                  pl.BlockSpec((D_MODEL, IN_WIDTH), lambda i: (0, 0), pipeline_mode=pl.Buffered(1))],
        out_specs=[pl.BlockSpec((tm, w), lambda i: (i, 0)) for w in IN_SIZES],
        out_shape=[jax.ShapeDtypeStruct((n, w), F32) for w in IN_SIZES],
        compiler_params=_params(("parallel",)),
        name="inproj",
    )(x, w_bf16)


def _causal_pairs(nq, descending):
    qi, kj = [], []
    for i in range(nq):
        ks = range(i, -1, -1) if descending else range(i + 1)
        for j in ks:
            qi.append(i)
            kj.append(j)
    return jnp.asarray(qi, jnp.int32), jnp.asarray(kj, jnp.int32)


def _suffix_matrix():
    r = lax.broadcasted_iota(jnp.int32, (LANES, LANES), 0)
    c = lax.broadcasted_iota(jnp.int32, (LANES, LANES), 1)
    u = (r > c).astype(BF16)
    return jnp.concatenate([u, jnp.ones((LANES, LANES), BF16)], axis=1)


def _split_dot(x, rhs_bf16, terms=2):
    out = None
    for _ in range(terms):
        part = x.astype(BF16)
        x = x - part.astype(F32)
        d = _dot(part, rhs_bf16)
        out = d if out is None else out + d
    return out


def _log_sigmoid_pair(z):
    sp = jnp.log1p(jnp.exp(-jnp.abs(z)))
    return jnp.minimum(z, 0.0) - sp, -jnp.maximum(z, 0.0) - sp


EXP_UNDERFLOW = -104.0


def _sb_kernel(qi_ref, kj_ref, q_ref, k_ref, v_ref, tri_ref, o_ref, carry_ref, acc_ref, live_ref,
               lb_ref, lk_ref, st_ref, w_ref, *, tq):
    p = pl.program_id(1)
    qi = qi_ref[p]
    kj = kj_ref[p]

    @pl.when(kj == qi)
    def _init():
        carry_ref[...] = jnp.zeros_like(carry_ref)
        acc_ref[...] = jnp.zeros_like(acc_ref)
        live_ref[0] = 1

    def live_step(diag):
        if diag:
            row = lax.broadcasted_iota(jnp.int32, (tq, tq), 0)
            col = lax.broadcasted_iota(jnp.int32, (tq, tq), 1)
            valid = col < row
        chunks = tq // LANES
        for h in range(H_SB):
            hs = slice(h * D_SB, (h + 1) * D_SB)
            z = _dot_nt((q_ref[:, hs] * QK_SCALE).astype(BF16), k_ref[:, hs].astype(BF16))
            log_beta, log_keep = _log_sigmoid_pair(z)
            if diag:
                log_keep = jnp.where(valid, log_keep, 0.0)
            lb_ref[h] = log_beta
            lk_ref[h] = log_keep
        for h in range(H_SB):
            for c in range(chunks):
                st_ref[h, c] = _split_dot(lk_ref[h, :, c * LANES:(c + 1) * LANES], tri_ref[...])
        worst = None
        for h in range(H_SB):
            carry = carry_ref[h]
            for c in reversed(range(chunks)):
                sl = slice(c * LANES, (c + 1) * LANES)
                w = jnp.exp(lb_ref[h, :, sl] + st_ref[h, c, :, :LANES] + carry)
                if diag:
                    w = jnp.where(valid[:, sl], w, 0.0)
                w_ref[h, :, sl] = w.astype(BF16)
                carry = carry + st_ref[h, c, :, LANES:]
            carry_ref[h] = carry
            top = jnp.max(carry)
            worst = top if worst is None else jnp.maximum(worst, top)
        for h in range(H_SB):
            hs = slice(h * D_SB, (h + 1) * D_SB)
            acc_ref[h] = acc_ref[h] + _dot(w_ref[h], v_ref[:, hs].astype(BF16))
        live_ref[0] = (worst >= EXP_UNDERFLOW).astype(jnp.int32)

    def step(diag):
        @pl.when(live_ref[0] == 1)
        def _live():
            live_step(diag)

    @pl.when(kj == qi)
    def _diag():
        step(True)

    @pl.when(kj < qi)
    def _off():
        step(False)

    @pl.when(kj == 0)
    def _fin():
        for h in range(H_SB):
            o_ref[:, h * D_SB:(h + 1) * D_SB] = acc_ref[h]


def _sb_attention(q, k, v, batch, seq, tq):
    nq = seq // tq
    qi, kj = _causal_pairs(nq, descending=True)
    grid_spec = pltpu.PrefetchScalarGridSpec(
        num_scalar_prefetch=2,
        grid=(batch, int(qi.shape[0])),
        in_specs=[pl.BlockSpec((tq, W_SB), lambda b, p, qi, kj: (b * nq + qi[p], 0)),
                  pl.BlockSpec((tq, W_SB), lambda b, p, qi, kj: (b * nq + kj[p], 0)),
                  pl.BlockSpec((tq, W_SB), lambda b, p, qi, kj: (b * nq + kj[p], 0)),
                  pl.BlockSpec((LANES, 2 * LANES), lambda b, p, qi, kj: (0, 0))],
        out_specs=pl.BlockSpec((tq, W_SB), lambda b, p, qi, kj: (b * nq + qi[p], 0)),
        scratch_shapes=[pltpu.VMEM((H_SB, tq, LANES), F32), pltpu.VMEM((H_SB, tq, D_SB), F32),
                        pltpu.SMEM((1,), jnp.int32),
                        pltpu.VMEM((H_SB, tq, tq), F32), pltpu.VMEM((H_SB, tq, tq), F32),
                        pltpu.VMEM((H_SB, tq // LANES, tq, 2 * LANES), F32), pltpu.VMEM((H_SB, tq, tq), BF16)],
    )
    return pl.pallas_call(
        functools.partial(_sb_kernel, tq=tq),
        grid_spec=grid_spec,
        out_shape=jax.ShapeDtypeStruct((batch * seq, W_SB), F32),
        compiler_params=_params(("parallel", "arbitrary")),
        name="sb_attention",
    )(qi, kj, q, k, v, _suffix_matrix())


def _bias_of_distance(table, n):
    max_exact = N_BUCKETS // 2
    nf = jnp.maximum(n, 1).astype(F32)
    large = max_exact + (jnp.log(nf / max_exact) / math.log(MAX_DIST / max_exact)
                         * (N_BUCKETS - max_exact)).astype(jnp.int32)
    large = jnp.minimum(large, N_BUCKETS - 1)
    bucket = jnp.where(n < max_exact, n, large)
    out = jnp.zeros(n.shape + (H_DF,), F32)
    for b in range(N_BUCKETS):
        out = jnp.where((bucket == b)[..., None], table[b].astype(F32), out)
    return out


def _bias_tiles(table, tq):
    assert tq >= MAX_DIST, "blocks two or more behind must all fall in the last bucket"
    key = jnp.arange(tq, dtype=jnp.int32)[:, None]
    query = jnp.arange(tq, dtype=jnp.int32)[None, :]
    tiles = [_bias_of_distance(table, jnp.maximum(t * tq + query - key, 0)) for t in range(3)]
    return jnp.transpose(jnp.stack(tiles), (0, 3, 1, 2))


def _lambda(lq1, lk1, lq2, lk2, lam_init):
    return (jnp.exp(jnp.sum(lq1 * lk1, axis=-1, keepdims=True))
            - jnp.exp(jnp.sum(lq2 * lk2, axis=-1, keepdims=True)) + lam_init)


def _sub_norm(o, g, post_scale):
    ms = jnp.mean(o * o, axis=-1, keepdims=True)
    return o * lax.rsqrt(ms + LN_EPS) * g * post_scale


def _df_kernel(qi_ref, kj_ref, q_ref, k_ref, v_ref, bias_ref, lam_ref, g_ref, o_ref,
               m_ref, l_ref, acc_ref, s_ref, p_ref, *, tq, lam_init):
    p = pl.program_id(1)
    qi = qi_ref[p]
    kj = kj_ref[p]

    @pl.when(kj == 0)
    def _init():
        m_ref[...] = jnp.full_like(m_ref, NEG_INF)
        l_ref[...] = jnp.zeros_like(l_ref)
        acc_ref[...] = jnp.zeros_like(acc_ref)

    def step(diag):
        tile = jnp.minimum(qi - kj, 2)
        if diag:
            key = lax.broadcasted_iota(jnp.int32, (tq, tq), 0)
            query = lax.broadcasted_iota(jnp.int32, (tq, tq), 1)
            causal = key <= query
        for h in range(H_DF):
            bias = bias_ref[tile, h]
            for mp in range(2):
                r = 2 * h + mp
                rs = slice(r * D_DF, (r + 1) * D_DF)
                s = _dot_nt(k_ref[:, rs].astype(BF16), (q_ref[:, rs] * QK_SCALE).astype(BF16)) + bias
                if diag:
                    s = jnp.where(causal, s, NEG_INF)
                s_ref[r] = s
        alphas = []
        for r in range(2 * H_DF):
            s = s_ref[r]
            m_old = m_ref[r]
            m_new = jnp.maximum(m_old, jnp.max(s, axis=0, keepdims=True))
            pr = jnp.exp(s - m_new)
            alphas.append(jnp.exp(m_old - m_new))
            l_ref[r] = alphas[r] * l_ref[r] + jnp.sum(pr, axis=0, keepdims=True)
            m_ref[r] = m_new
            p_ref[r] = pr.astype(BF16)
        for h in range(H_DF):
            vt = v_ref[:, h * 2 * D_DF:(h + 1) * 2 * D_DF].T.astype(BF16)
            for mp in range(2):
                r = 2 * h + mp
                acc_ref[r] = alphas[r] * acc_ref[r] + _dot(vt, p_ref[r])

    @pl.when(kj == qi)
    def _diag():
        step(True)

    @pl.when(kj < qi)
    def _off():
        step(False)

    @pl.when(kj == qi)
    def _fin():
        lam = _lambda(lam_ref[0:1, :], lam_ref[1:2, :], lam_ref[2:3, :], lam_ref[3:4, :], lam_init)
        for h in range(H_DF):
            o = acc_ref[2 * h] / l_ref[2 * h] - lam * (acc_ref[2 * h + 1] / l_ref[2 * h + 1])
            ms = jnp.mean(o * o, axis=0, keepdims=True)
            o = o * lax.rsqrt(ms + LN_EPS) * g_ref[...] * (1.0 - lam_init)
            o_ref[:, h * 2 * D_DF:(h + 1) * 2 * D_DF] = o.T


def _df_attention(q, k, v, bias_tiles, lam_vecs, subln_g, batch, seq, tq, lam_init):
    nq = seq // tq
    qi, kj = _causal_pairs(nq, descending=False)
    blk = lambda which: (lambda b, p, qi, kj: (b * nq + (qi if which == "q" else kj)[p], 0))
    grid_spec = pltpu.PrefetchScalarGridSpec(
        num_scalar_prefetch=2,
        grid=(batch, int(qi.shape[0])),
        in_specs=[pl.BlockSpec((tq, W_DF), blk("q")),
                  pl.BlockSpec((tq, W_DF), blk("k")),
                  pl.BlockSpec((tq, W_DF), blk("k")),
                  pl.BlockSpec((3, H_DF, tq, tq), lambda b, p, qi, kj: (0, 0, 0, 0)),
                  pl.BlockSpec((4, D_DF), lambda b, p, qi, kj: (0, 0)),
                  pl.BlockSpec((2 * D_DF, 1), lambda b, p, qi, kj: (0, 0))],
        out_specs=pl.BlockSpec((tq, W_DF), blk("q")),
        scratch_shapes=[pltpu.VMEM((2 * H_DF, 1, tq), F32), pltpu.VMEM((2 * H_DF, 1, tq), F32),
                        pltpu.VMEM((2 * H_DF, 2 * D_DF, tq), F32),
                        pltpu.VMEM((2 * H_DF, tq, tq), F32), pltpu.VMEM((2 * H_DF, tq, tq), BF16)],
    )
    return pl.pallas_call(
        functools.partial(_df_kernel, tq=tq, lam_init=lam_init),
        grid_spec=grid_spec,
        out_shape=jax.ShapeDtypeStruct((batch * seq, W_DF), F32),
        compiler_params=_params(("parallel", "arbitrary")),
        name="df_attention",
    )(qi, kj, q, k, v, bias_tiles, lam_vecs, subln_g)


CONV_HIST = CONV_W - 1
CONV_PAD = 32
CONV_CHUNK = 64


def _bf16_round(x):
    return x.astype(BF16).astype(F32)


def _conv_kernel(glu_ref, w_ref, cb_ref, g_ref, b_ref, o_ref, st_ref, ubuf_ref, tail_ref, *, tt):
    i = pl.program_id(1)

    @pl.when(i == 0)
    def _zero_history():
        ubuf_ref[0:CONV_PAD, :] = jnp.zeros((CONV_PAD, C_CONV), F32)

    @pl.when(i > 0)
    def _carry_history():
        ubuf_ref[0:CONV_PAD, :] = ubuf_ref[tt:tt + CONV_PAD, :]

    u = glu_ref[:, :C_CONV] * jax.nn.sigmoid(glu_ref[:, C_CONV:])
    ubuf_ref[CONV_PAD:CONV_PAD + tt, :] = _bf16_round(u)
    taps = [_bf16_round(w_ref[j:j + 1, :]) for j in range(CONV_W)]
    first = CONV_PAD - CONV_HIST
    for r0 in range(0, tt, CONV_CHUNK):
        acc = jnp.zeros((CONV_CHUNK, C_CONV), F32)
        for j in range(CONV_W):
            acc = acc + ubuf_ref[first + r0 + j:first + r0 + j + CONV_CHUNK, :] * taps[j]
        y = _layer_norm(acc + cb_ref[...], g_ref[...], b_ref[...])
        o_ref[r0:r0 + CONV_CHUNK, :] = _silu(y)

    @pl.when(i == pl.num_programs(1) - 1)
    def _final_state():
        tail_ref[...] = u[tt - CONV_PAD:, :]
        st_ref[...] = tail_ref[CONV_PAD - CONV_HIST:, :]


def _conv_branch(glu, conv_w, conv_b, g, b, batch, seq, tt):
    nt = seq // tt
    vec = lambda: pl.BlockSpec((1, C_CONV), lambda bb, i: (0, 0))
    return pl.pallas_call(
        functools.partial(_conv_kernel, tt=tt),
        grid=(batch, nt),
        in_specs=[pl.BlockSpec((tt, 2 * C_CONV), lambda bb, i: (bb * nt + i, 0)),
                  pl.BlockSpec((CONV_W, C_CONV), lambda bb, i: (0, 0)),
                  vec(), vec(), vec()],
        out_specs=[pl.BlockSpec((tt, C_CONV), lambda bb, i: (bb * nt + i, 0)),
                   pl.BlockSpec((None, CONV_HIST, C_CONV), lambda bb, i: (bb, 0, 0))],
        out_shape=[jax.ShapeDtypeStruct((batch * seq, C_CONV), F32),
                   jax.ShapeDtypeStruct((batch, CONV_HIST, C_CONV), F32)],
        scratch_shapes=[pltpu.VMEM((CONV_PAD + tt, C_CONV), F32), pltpu.VMEM((CONV_PAD, C_CONV), F32)],
        compiler_params=_params(("parallel", "arbitrary")),
        name="conv_branch",
    )(glu, conv_w, conv_b, g, b)


def _conv_dec_kernel(glu_ref, st_ref, w_ref, cb_ref, g_ref, b_ref, o_ref, u_ref):
    u = glu_ref[:, :C_CONV] * jax.nn.sigmoid(glu_ref[:, C_CONV:])
    acc = u * w_ref[CONV_HIST:CONV_W, :]
    for j in range(CONV_HIST):
        acc = acc + st_ref[j] * w_ref[j:j + 1, :]
    y = _layer_norm(acc + cb_ref[...], g_ref[...], b_ref[...])
    o_ref[...] = _silu(y)
    u_ref[...] = u


def _conv_decode(glu, state_t, conv_w, conv_b, g, b):
    n = glu.shape[0]
    return pl.pallas_call(
        _conv_dec_kernel,
        out_shape=[jax.ShapeDtypeStruct((n, C_CONV), F32), jax.ShapeDtypeStruct((n, C_CONV), F32)],
        name="conv_decode",
    )(glu, state_t, conv_w, conv_b, g, b)


def _merge_kernel(x_ref, osb_ref, odf_ref, oc_ref, gate_ref, wsb_ref, wdf_ref, wc_ref, wo_ref,
                  g_ref, b_ref, o_ref, *, alpha):
    merged = (jax.nn.sigmoid(gate_ref[:, 0:D_MODEL]) * _dot(osb_ref[...].astype(BF16), wsb_ref[...])
              + jax.nn.sigmoid(gate_ref[:, D_MODEL:2 * D_MODEL]) * _dot(odf_ref[...].astype(BF16), wdf_ref[...])
              + jax.nn.sigmoid(gate_ref[:, 2 * D_MODEL:]) * _dot(oc_ref[...].astype(BF16), wc_ref[...]))
    m = _dot(merged.astype(BF16), wo_ref[...])
    o_ref[...] = _layer_norm(alpha * x_ref[...] + m, g_ref[...], b_ref[...])


def _merge(x, o_sb, o_df, o_c, gates, w_sb, w_df, w_c, w_o, g, b, tm, alpha):
    n = x.shape[0]
    rows = lambda w: pl.BlockSpec((tm, w), lambda i: (i, 0))
    full = lambda a: pl.BlockSpec(a.shape, lambda i: (0, 0))
    return pl.pallas_call(
        functools.partial(_merge_kernel, alpha=alpha),
        grid=(n // tm,),
        in_specs=[rows(D_MODEL), rows(W_SB), rows(W_DF), rows(C_CONV), rows(3 * D_MODEL),
                  full(w_sb), full(w_df), full(w_c), full(w_o), full(g), full(b)],
        out_specs=rows(D_MODEL),
        out_shape=jax.ShapeDtypeStruct((n, D_MODEL), F32),
        compiler_params=_params(("parallel",)),
        name="merge_ln",
    )(x, o_sb, o_df, o_c, gates, w_sb, w_df, w_c, w_o, g, b)


def _first_argmax(vals, ids, axis, sentinel):
    mx = jnp.max(vals, axis=axis, keepdims=True)
    ix = jnp.min(jnp.where(vals == mx, ids, sentinel), axis=axis, keepdims=True)
    return mx, ix


def _router_kernel(x_ref, rw_ref, rb_ref, before_ref, idx_ref, w_ref, rank_ref, count_ref, seen_ref, *, tm):
    @pl.when(pl.program_id(0) == 0)
    def _init():
        seen_ref[...] = jnp.zeros_like(seen_ref)

    logits = _dot_nt(rw_ref[...], x_ref[...].astype(BF16))
    scores = jax.nn.sigmoid(logits)
    biased = scores + rb_ref[...]
    sub = lax.broadcasted_iota(jnp.int32, (GROUP_SIZE, tm), 0).astype(F32)
    gscore = jnp.zeros((N_GROUPS, tm), F32)
    for g in range(N_GROUPS):
        blk = biased[g * GROUP_SIZE:(g + 1) * GROUP_SIZE, :]
        m1, i1 = _first_argmax(blk, sub, 0, float(GROUP_SIZE))
        m2 = jnp.max(jnp.where(sub == i1, NEG_INF, blk), axis=0, keepdims=True)
        gscore = jnp.where(sub == float(g), m1 + m2, gscore)
    gmask = jnp.zeros((N_GROUPS, tm), F32)
    for _ in range(TOPK_GROUPS):
        _, ig = _first_argmax(gscore, sub, 0, float(N_GROUPS))
        sel = sub == ig
        gmask = jnp.where(sel, 1.0, gmask)
        gscore = jnp.where(sel, NEG_INF, gscore)
    eid = lax.broadcasted_iota(jnp.int32, (N_EXPERTS, tm), 0).astype(F32)
    cand = jnp.concatenate(
        [jnp.where(gmask[g:g + 1, :] > 0.0, biased[g * GROUP_SIZE:(g + 1) * GROUP_SIZE, :], NEG_INF)
         for g in range(N_GROUPS)], axis=0)
    total = jnp.zeros((1, tm), F32)
    picked, chosen = [], []
    member = jnp.zeros((N_EXPERTS, tm), F32)
    for k in range(TOP_K):
        _, ie = _first_argmax(cand, eid, 0, float(N_EXPERTS))
        sel = eid == ie
        wk = jnp.sum(jnp.where(sel, scores, 0.0), axis=0, keepdims=True)
        cand = jnp.where(sel, NEG_INF, cand)
        member = jnp.where(sel, 1.0, member)
        idx_ref[k:k + 1, :] = ie.astype(jnp.int32)
        picked.append(wk)
        chosen.append(ie)
        total = total + wk
    for k in range(TOP_K):
        w_ref[k:k + 1, :] = picked[k] / total * ROUTE_SCALE
    earlier = _dot(member.astype(BF16), before_ref[...]) + seen_ref[...]
    for k in range(TOP_K):
        rank_ref[k:k + 1, :] = jnp.sum(jnp.where(eid == chosen[k], earlier, 0.0), axis=0,
                                       keepdims=True).astype(jnp.int32)
    seen_ref[...] = seen_ref[...] + jnp.sum(member, axis=1, keepdims=True)
    count_ref[...] = seen_ref[...].astype(jnp.int32)


def _router(x, rw_t, rb_col, tm):
    n = x.shape[0]
    r = lax.broadcasted_iota(jnp.int32, (tm, tm), 0)
    c = lax.broadcasted_iota(jnp.int32, (tm, tm), 1)
    before = (r < c).astype(BF16)
    pairs = lambda: pl.BlockSpec((TOP_K, tm), lambda i: (0, i))
    return pl.pallas_call(
        functools.partial(_router_kernel, tm=tm),
        grid=(n // tm,),
        in_specs=[pl.BlockSpec((tm, D_MODEL), lambda i: (i, 0)),
                  pl.BlockSpec((N_EXPERTS, D_MODEL), lambda i: (0, 0)),
                  pl.BlockSpec((N_EXPERTS, 1), lambda i: (0, 0)),
                  pl.BlockSpec((tm, tm), lambda i: (0, 0))],
        out_specs=[pairs(), pairs(), pairs(), pl.BlockSpec((N_EXPERTS, 1), lambda i: (0, 0))],
        out_shape=[jax.ShapeDtypeStruct((TOP_K, n), jnp.int32),
                   jax.ShapeDtypeStruct((TOP_K, n), F32),
                   jax.ShapeDtypeStruct((TOP_K, n), jnp.int32),
                   jax.ShapeDtypeStruct((N_EXPERTS, 1), jnp.int32)],
        scratch_shapes=[pltpu.VMEM((N_EXPERTS, 1), F32)],
        compiler_params=_params(("arbitrary",)),
        name="router",
    )(x, rw_t, rb_col, before)


def _wait_row_gather(src_hbm, dst_ref, sem):
    pltpu.make_async_copy(src_hbm.at[pl.ds(0, dst_ref.shape[0])], dst_ref, sem).wait()


def _dispatch_tables(top_idx_t, rank_t, counts, br):
    k, n = top_idx_t.shape
    counts = counts.reshape(N_EXPERTS)
    padded = (counts + br - 1) // br * br
    pend = jnp.cumsum(padded)
    pstart = pend - padded
    experts = jnp.arange(N_EXPERTS, dtype=jnp.int32)
    dest = rank_t + jnp.sum(jnp.where(top_idx_t[:, :, None] == experts, pstart, 0), axis=-1)
    n_blocks = -(-(k * n) // br) + N_EXPERTS
    first_row = jnp.arange(n_blocks, dtype=jnp.int32)[:, None] * br
    blk_expert = jnp.minimum(jnp.sum((pend[None, :] <= first_row).astype(jnp.int32), axis=1), N_EXPERTS - 1)
    n_used = (pend[-1] // br).astype(jnp.int32).reshape(1)
    return dest.astype(jnp.int32), blk_expert.astype(jnp.int32), n_used, pend.astype(jnp.int32), n_blocks


def _dispatch_kernel(dest_ref, pend_ref, nused_ref, x_ref, xs_hbm, zbuf, sem, zsem, *, tm, br, n_blocks):
    i = pl.program_id(0)
    rows = TOP_K * tm

    @pl.when(i == 0)
    def _zero_fill():
        zbuf[...] = jnp.zeros_like(zbuf)

        def fill(first_row):
            return pltpu.make_async_copy(zbuf, xs_hbm.at[pl.ds(pl.multiple_of(first_row, br), br)], zsem)

        def has_rows(e):
            return pend_ref[e] > (pend_ref[e - 1] if e else 0)

        for e in range(N_EXPERTS):
            @pl.when(has_rows(e))
            def _start(e=e):
                fill(pend_ref[e] - br).start()

        def start_tail(b, carry):
            fill(b * br).start()
            return carry
        lax.fori_loop(nused_ref[0], n_blocks, start_tail, 0)

        for e in range(N_EXPERTS):
            @pl.when(has_rows(e))
            def _wait(e=e):
                fill(pend_ref[e] - br).wait()

        def wait_tail(b, carry):
            fill(b * br).wait()
            return carry
        lax.fori_loop(nused_ref[0], n_blocks, wait_tail, 0)

    def wait_one_tile():
        pltpu.make_async_copy(xs_hbm.at[pl.ds(0, rows)], xs_hbm.at[pl.ds(0, rows)], sem).wait()

    def group(g, carry):
        base = pl.multiple_of(g * SUBLANES, SUBLANES)
        for u in range(SUBLANES):
            for k in range(TOP_K):
                row = dest_ref[k, i * tm + base + u]
                pltpu.make_async_copy(x_ref.at[pl.ds(base + u, 1)], xs_hbm.at[pl.ds(row, 1)], sem).start()
        return carry
    lax.fori_loop(0, tm // SUBLANES, group, 0)

    wait_one_tile()


def _dispatch(x, dest, pend, n_used, n_blocks, tm, br):
    n = x.shape[0]
    grid_spec = pltpu.PrefetchScalarGridSpec(
        num_scalar_prefetch=3,
        grid=(n // tm,),
        in_specs=[pl.BlockSpec((tm, D_MODEL), lambda i, d, pe, nu: (i, 0))],
        out_specs=pl.BlockSpec(memory_space=pl.ANY),
        scratch_shapes=[pltpu.VMEM((br, D_MODEL), F32), pltpu.SemaphoreType.DMA(()), pltpu.SemaphoreType.DMA(())],
    )
    return pl.pallas_call(
        functools.partial(_dispatch_kernel, tm=tm, br=br, n_blocks=n_blocks),
        grid_spec=grid_spec,
        out_shape=jax.ShapeDtypeStruct((n_blocks * br, D_MODEL), F32),
        compiler_params=_params(("arbitrary",)),
        name="dispatch",
    )(dest, pend, n_used, x)


def _expert_kernel(be_ref, nused_ref, xs_ref, wg_ref, wu_ref, wd_ref, o_ref, wg_b, wu_b, wd_b):
    b = pl.program_id(0)
    new_expert = (b == 0) | (be_ref[b] != be_ref[jnp.maximum(b - 1, 0)])

    @pl.when(new_expert & (b < nused_ref[0]))
    def _round_weights():
        wg_b[...] = wg_ref[...].astype(BF16)
        wu_b[...] = wu_ref[...].astype(BF16)
        wd_b[...] = wd_ref[...].astype(BF16)

    @pl.when(b < nused_ref[0])
    def _compute():
        x = xs_ref[...].astype(BF16)
        hdn = _silu(_dot(x, wg_b[...])) * _dot(x, wu_b[...])
        o_ref[...] = _dot(hdn.astype(BF16), wd_b[...])

    @pl.when(b >= nused_ref[0])
    def _unused():
        o_ref[...] = jnp.zeros_like(o_ref)


def _experts(xs, blk_expert, n_used, expert_weights, br):
    wg, wu, wd, layer = expert_weights
    n_blocks = blk_expert.shape[0]
    used = lambda b, nu: jnp.maximum(jnp.minimum(b, nu[0] - 1), 0)
    grid_spec = pltpu.PrefetchScalarGridSpec(
        num_scalar_prefetch=2,
        grid=(n_blocks,),
        in_specs=[pl.BlockSpec((br, D_MODEL), lambda b, be, nu: (used(b, nu), 0)),
                  pl.BlockSpec((None, None, D_MODEL, D_EXPERT), lambda b, be, nu: (layer, be[b], 0, 0)),
                  pl.BlockSpec((None, None, D_MODEL, D_EXPERT), lambda b, be, nu: (layer, be[b], 0, 0)),
                  pl.BlockSpec((None, None, D_EXPERT, D_MODEL), lambda b, be, nu: (layer, be[b], 0, 0))],
        out_specs=pl.BlockSpec((br, D_MODEL), lambda b, be, nu: (b, 0)),
        scratch_shapes=[pltpu.VMEM((D_MODEL, D_EXPERT), BF16), pltpu.VMEM((D_MODEL, D_EXPERT), BF16),
                        pltpu.VMEM((D_EXPERT, D_MODEL), BF16)],
    )
    return pl.pallas_call(
        _expert_kernel,
        grid_spec=grid_spec,
        out_shape=jax.ShapeDtypeStruct((n_blocks * br, D_MODEL), F32),
        compiler_params=_params(("arbitrary",)),
        name="experts",
    )(blk_expert, n_used, xs, wg, wu, wd)


def _ffn_out_kernel(dest_ref, x_ref, tw_ref, out_hbm, sg_ref, su_ref, sd_ref, g_ref, b_ref, o_ref, ybuf, sem,
                    *, alpha, tm):
    i = pl.program_id(0)
    slot = i % 2

    def gather(tile, into):
        def group(g, carry):
            base = pl.multiple_of(g * SUBLANES, SUBLANES)
            for u in range(SUBLANES):
                for k in range(TOP_K):
                    row = dest_ref[k, tile * tm + base + u]
                    pltpu.make_async_copy(out_hbm.at[pl.ds(row, 1)], ybuf.at[into, k, pl.ds(base + u, 1)],
                                          sem.at[into]).start()
            return carry
        lax.fori_loop(0, tm // SUBLANES, group, 0)

    @pl.when(i == 0)
    def _first():
        gather(0, 0)

    @pl.when(i + 1 < pl.num_programs(0))
    def _prefetch():
        gather(i + 1, 1 - slot)

    for k in range(TOP_K):
        _wait_row_gather(out_hbm, ybuf.at[slot, k], sem.at[slot])
    x = x_ref[...]
    xb = x.astype(BF16)
    hdn = _silu(_dot(xb, sg_ref[...])) * _dot(xb, su_ref[...])
    y = ybuf[slot, 0] * tw_ref[:, 0:1]
    for k in range(1, TOP_K):
        y = y + ybuf[slot, k] * tw_ref[:, k:k + 1]
    y = y + _dot(hdn.astype(BF16), sd_ref[...])
    o_ref[...] = _layer_norm(alpha * x + y, g_ref[...], b_ref[...])


def _ffn_out(x, expert_out, dest, top_w, sg, su, sd, g, b, tm, alpha):
    n = x.shape[0]
    rows = lambda w: pl.BlockSpec((tm, w), lambda i, d: (i, 0))
    full = lambda a: pl.BlockSpec(a.shape, lambda i, d: (0, 0))
    grid_spec = pltpu.PrefetchScalarGridSpec(
        num_scalar_prefetch=1,
        grid=(n // tm,),
        in_specs=[rows(D_MODEL), rows(TOP_K), pl.BlockSpec(memory_space=pl.ANY),
                  full(sg), full(su), full(sd), full(g), full(b)],
        out_specs=rows(D_MODEL),
        scratch_shapes=[pltpu.VMEM((2, TOP_K, tm, D_MODEL), F32), pltpu.SemaphoreType.DMA((2,))],
    )
    return pl.pallas_call(
        functools.partial(_ffn_out_kernel, alpha=alpha, tm=tm),
        grid_spec=grid_spec,
        out_shape=jax.ShapeDtypeStruct((n, D_MODEL), F32),
        compiler_params=_params(("arbitrary",)),
        name="ffn_out_ln",
    )(dest, x, top_w, expert_out, sg, su, sd, g, b)


def _moe(x1, f, tm, br, alpha):
    router_w, router_b, expert_weights, sg, su, sd, ln_g, ln_b = f
    top_idx_t, top_w_t, rank_t, counts = _router(x1, router_w, router_b, tm)
    dest, blk_expert, n_used, pend, n_blocks = _dispatch_tables(top_idx_t, rank_t, counts, br)
    xs = _dispatch(x1, dest, pend, n_used, n_blocks, tm, br)
    out = _experts(xs, blk_expert, n_used, expert_weights, br)
    return _ffn_out(x1, out, dest, top_w_t.T, sg, su, sd, ln_g, ln_b, min(tm, COMBINE_TM), alpha)


def _sb_dec_kernel(pt_ref, q_ref, *refs, pc):
    k_refs, v_refs = refs[:pc], refs[pc:2 * pc]
    tri_ref, o_ref, carry_ref, acc_ref = refs[2 * pc:]
    c = pl.program_id(1)

    @pl.when(c == 0)
    def _init():
        carry_ref[...] = jnp.zeros_like(carry_ref)
        acc_ref[...] = jnp.zeros_like(acc_ref)

    row = lax.broadcasted_iota(jnp.int32, (SUBLANES, W_SB), 0)
    col = lax.broadcasted_iota(jnp.int32, (SUBLANES, W_SB), 1)
    own = (col // D_SB) == row
    qbd = jnp.where(own, jnp.broadcast_to(q_ref[...], (SUBLANES, W_SB)), 0.0).astype(BF16)
    carry = carry_ref[...]
    acc = acc_ref[...]
    z = jnp.concatenate([_dot(qbd, k_refs[p][...].astype(BF16)) for p in range(pc)], axis=0) * QK_SCALE
    log_beta, log_keep = _log_sigmoid_pair(z)
    st = _split_dot(log_keep, tri_ref[...], terms=3)
    for p in range(pc):
        rows = slice(p * SUBLANES, (p + 1) * SUBLANES)
        w = jnp.exp(log_beta[rows] + st[rows, :LANES] + carry)
        acc = acc + _dot_nt(w.astype(BF16), v_refs[p][...].astype(BF16))
        carry = carry + st[rows, LANES:]
    carry_ref[...] = carry
    acc_ref[...] = acc

    @pl.when(c == pl.num_programs(1) - 1)
    def _fin():
        o_ref[...] = jnp.sum(jnp.where(own, acc, 0.0), axis=0, keepdims=True)


def _sb_decode(q, cache_kt, cache_vt, layer, page_table, pc):
    nb, n_pages = page_table.shape

    def page_spec(p):
        def index(b, c, pt):
            return (layer, pt[b * n_pages + n_pages - 1 - (c * pc + p)], 0, 0)
        return pl.BlockSpec((None, None, W_SB, PAGE_SIZE), index)

    row = lambda w: pl.BlockSpec((None, 1, w), lambda b, c, pt: (b, 0, 0))
    grid_spec = pltpu.PrefetchScalarGridSpec(
        num_scalar_prefetch=1,
        grid=(nb, n_pages // pc),
        in_specs=([row(W_SB)]
                  + [page_spec(p) for p in range(pc)]
                  + [page_spec(p) for p in range(pc)]
                  + [pl.BlockSpec((LANES, 2 * LANES), lambda b, c, pt: (0, 0))]),
        out_specs=row(W_SB),
        scratch_shapes=[pltpu.VMEM((SUBLANES, LANES), F32), pltpu.VMEM((SUBLANES, W_SB), F32)],
    )
    out = pl.pallas_call(
        functools.partial(_sb_dec_kernel, pc=pc),
        grid_spec=grid_spec,
        out_shape=jax.ShapeDtypeStruct((nb, 1, W_SB), F32),
        compiler_params=_params(("parallel", "arbitrary")),
        name="sb_decode",
    )(page_table.reshape(-1), q.reshape(nb, 1, W_SB), *([cache_kt] * pc), *([cache_vt] * pc), _suffix_matrix())
    return out.reshape(nb, W_SB)


def _df_dec_kernel(pt_ref, q_ref, kn_ref, vn_ref, bias_ref, bself_ref, lam_ref, g_ref, spread_ref, *refs,
                   pc, n_pages, lam_init):
    k_refs, v_refs = refs[:pc], refs[pc:2 * pc]
    o_ref, s_ref, m_ref, aself_ref, acc_ref = refs[2 * pc:]
    c = pl.program_id(1)
    nc = n_pages // pc

    @pl.when(c == 0)
    def _init():
        m_ref[...] = jnp.full_like(m_ref, NEG_INF)
        acc_ref[...] = jnp.zeros_like(acc_ref)

    def page_cols(page):
        return pl.ds(pl.multiple_of(page * PAGE_SIZE, PAGE_SIZE), PAGE_SIZE)

    @pl.when(c < nc)
    def _scores():
        row = lax.broadcasted_iota(jnp.int32, (SUBLANES, W_DF), 0)
        col = lax.broadcasted_iota(jnp.int32, (SUBLANES, W_DF), 1)
        own = (col // D_DF) == 2 * (row % H_DF) + row // H_DF
        qbd = jnp.where(own, jnp.broadcast_to(q_ref[...], (SUBLANES, W_DF)), 0.0).astype(BF16)
        m = m_ref[...]
        for p in range(pc):
            cols = page_cols(c * pc + p)
            s = _dot(qbd, k_refs[p][...].astype(BF16)) * (D_DF ** -0.5) + bias_ref[:, cols]
            s_ref[:, cols] = s
            m = jnp.maximum(m, jnp.max(s, axis=-1, keepdims=True))
        m_ref[...] = m

        @pl.when(c == nc - 1)
        def _weights():
            kn = jnp.broadcast_to(_bf16_round(kn_ref[...]), (SUBLANES, W_DF))
            s_self = jnp.sum(qbd.astype(F32) * kn, axis=-1, keepdims=True) * (D_DF ** -0.5) + bself_ref[...]
            m_fin = jnp.maximum(m, s_self)
            pr = jnp.exp(s_ref[...] - m_fin)
            p_self = jnp.exp(s_self - m_fin)
            total = jnp.sum(pr, axis=-1, keepdims=True) + p_self
            lam = _lambda(lam_ref[0:1, :], lam_ref[1:2, :], lam_ref[2:3, :], lam_ref[3:4, :], lam_init)
            pn = pr / total
            pn_self = jnp.broadcast_to(p_self / total, (SUBLANES, LANES))
            s_ref[...] = pn - lam * pltpu.roll(pn, shift=H_DF, axis=0)
            aself_ref[...] = pn_self - lam * pltpu.roll(pn_self, shift=H_DF, axis=0)

    @pl.when(c >= nc)
    def _values():
        rows = PAGE_SIZE * H_DF
        row = lax.broadcasted_iota(jnp.int32, (SUBLANES, rows), 0)
        col = lax.broadcasted_iota(jnp.int32, (SUBLANES, rows), 1)
        own = (col % H_DF) == row
        acc = acc_ref[...]
        weights = jnp.concatenate([s_ref[:, page_cols((c - nc) * pc + p)] for p in range(pc)], axis=0)
        spread = _dot(weights.astype(BF16), spread_ref[...])
        for p in range(pc):
            own_head = jnp.where(own, spread[p * SUBLANES:(p + 1) * SUBLANES], 0.0)
            acc = acc + _dot(own_head.astype(BF16), v_refs[p][...].astype(BF16))
        acc_ref[...] = acc

        @pl.when(c == 2 * nc - 1)
        def _fin():
            o = acc[0:H_DF] + _bf16_round(aself_ref[0:H_DF, 0:1]) * _bf16_round(vn_ref[...])
            o_ref[...] = _sub_norm(o, g_ref[...], 1.0 - lam_init)


def _df_decode(q, k_new, v_new, cache_kt, cache_v, layer, page_table, bias_past, bias_self, lam_vecs, subln_g,
               pc, lam_init):
    nb, n_pages = page_table.shape
    nc = n_pages // pc
    head_w = 2 * D_DF
    row = lambda w: pl.BlockSpec((None, 1, w), lambda b, c, pt: (b, 0, 0))
    heads = pl.BlockSpec((None, H_DF, head_w), lambda b, c, pt: (b, 0, 0))
    full = lambda a: pl.BlockSpec(a.shape, lambda b, c, pt: (0,) * a.ndim)

    def page_spec(p, second_pass):
        def index(b, c, pt):
            chunk = jnp.maximum(c - nc, 0) if second_pass else jnp.minimum(c, nc - 1)
            return (layer, pt[b * n_pages + chunk * pc + p], 0, 0)
        shape = (None, None, PAGE_SIZE * H_DF, head_w) if second_pass else (None, None, W_DF, PAGE_SIZE)
        return pl.BlockSpec(shape, index)

    r = lax.broadcasted_iota(jnp.int32, (PAGE_SIZE, PAGE_SIZE * H_DF), 0)
    c = lax.broadcasted_iota(jnp.int32, (PAGE_SIZE, PAGE_SIZE * H_DF), 1)
    spread = (c // H_DF == r).astype(BF16)
    grid_spec = pltpu.PrefetchScalarGridSpec(
        num_scalar_prefetch=1,
        grid=(nb, 2 * nc),
        in_specs=([row(W_DF), row(W_DF), heads, full(bias_past), full(bias_self), full(lam_vecs),
                   full(subln_g), full(spread)]
                  + [page_spec(p, False) for p in range(pc)]
                  + [page_spec(p, True) for p in range(pc)]),
        out_specs=heads,
        scratch_shapes=[pltpu.VMEM((SUBLANES, n_pages * PAGE_SIZE), F32), pltpu.VMEM((SUBLANES, 1), F32),
                        pltpu.VMEM((SUBLANES, LANES), F32), pltpu.VMEM((SUBLANES, head_w), F32)],
    )
    r3 = lambda a: a.reshape(nb, 1, W_DF)
    out = pl.pallas_call(
        functools.partial(_df_dec_kernel, pc=pc, n_pages=n_pages, lam_init=lam_init),
        grid_spec=grid_spec,
        out_shape=jax.ShapeDtypeStruct((nb, H_DF, head_w), F32),
        compiler_params=_params(("parallel", "arbitrary")),
        name="df_decode",
    )(page_table.reshape(-1), r3(q), r3(k_new), v_new.reshape(nb, H_DF, head_w), bias_past, bias_self, lam_vecs,
      subln_g, spread, *([cache_kt] * pc), *([cache_v] * pc))
    return out.reshape(nb, W_DF)


PROMPT_TM = 256
PROMPT_ATT_BLOCK = 256
PROMPT_CONV_TILE = 512
PROMPT_MOE_BLOCK = 256
COMBINE_TM = 128
SAMPLE_MOE_BLOCK = 32
DECODE_PAGES_PER_STEP = 16


def kernel(x_prompt, x_sample, cache_sb_k, cache_sb_v, cache_df_k, cache_df_v, state_conv, page_table,
           w_in, rel_bias_table, lam_q1, lam_k1, lam_q2, lam_k2, subln_g, conv_w, conv_b, conv_ln_g,
           conv_ln_b, w_sb_out, w_df_out, w_conv_out, w_o, ln1_g, ln1_b, router_w, router_bias, w_gate,
           w_up, w_down, sh_gate, sh_up, sh_down, ln2_g, ln2_b):
    depth = w_in.shape[0]
    batch, seq, _ = x_prompt.shape
    nb = x_sample.shape[0]
    n_pages = page_table.shape[1]
    past_len = n_pages * PAGE_SIZE
    alpha = (2 * depth) ** 0.25
    n_phys = cache_sb_k.shape[1]

    xp = x_prompt.reshape(batch * seq, D_MODEL)
    xs = x_sample.reshape(nb, D_MODEL)
    bias_tiles = _bias_tiles(rel_bias_table, PROMPT_ATT_BLOCK)
    dist = past_len - jnp.arange(past_len, dtype=jnp.int32)
    bias_past = jnp.tile(_bias_of_distance(rel_bias_table, dist).T, (2, 1))
    bias_self = jnp.tile(_bias_of_distance(rel_bias_table, jnp.zeros((1,), jnp.int32)).T, (2, 1))
    row = lambda a: a.reshape(1, -1)
    sb_kt = jnp.transpose(cache_sb_k, (0, 1, 3, 4, 2)).reshape(depth, n_phys, W_SB, PAGE_SIZE)
    sb_vt = jnp.transpose(cache_sb_v, (0, 1, 3, 4, 2)).reshape(depth, n_phys, W_SB, PAGE_SIZE)
    df_kt = jnp.transpose(cache_df_k, (0, 1, 3, 4, 5, 2)).reshape(depth, n_phys, W_DF, PAGE_SIZE)
    df_v = cache_df_v.reshape(depth, n_phys, PAGE_SIZE * H_DF, 2 * D_DF)

    new_p, new_s = [], []
    for l in range(depth):
        lam_init = 0.8 - 0.6 * math.exp(-0.3 * l)
        w_in_b = w_in[l].astype(BF16)
        lam_vecs = jnp.stack([lam_q1[l], lam_k1[l], lam_q2[l], lam_k2[l]])
        merge_w = (w_sb_out[l].astype(BF16), w_df_out[l].astype(BF16), w_conv_out[l].astype(BF16),
                   w_o[l].astype(BF16), row(ln1_g[l]), row(ln1_b[l]))
        ffn = (router_w[l].T.astype(BF16), router_bias[l].reshape(N_EXPERTS, 1),
               (w_gate, w_up, w_down, l),
               sh_gate[l].astype(BF16), sh_up[l].astype(BF16), sh_down[l].astype(BF16),
               row(ln2_g[l]), row(ln2_b[l]))
        conv_p = (conv_w[l], row(conv_b[l]), row(conv_ln_g[l]), row(conv_ln_b[l]))

        q_sb, k_sb, v_sb, q_df, k_df, v_df, glu, gates = _inproj(xp, w_in_b, PROMPT_TM)
        o_sb = _sb_attention(q_sb, k_sb, v_sb, batch, seq, PROMPT_ATT_BLOCK)
        o_df = _df_attention(q_df, k_df, v_df, bias_tiles, lam_vecs, subln_g[l].reshape(2 * D_DF, 1), batch, seq,
                             PROMPT_ATT_BLOCK, lam_init)
        o_c, p_conv = _conv_branch(glu, *conv_p, batch, seq, PROMPT_CONV_TILE)
        x1 = _merge(xp, o_sb, o_df, o_c, gates, *merge_w, PROMPT_TM, alpha)
        xp = _moe(x1, ffn, PROMPT_TM, PROMPT_MOE_BLOCK, alpha)
        new_p.append((k_sb, v_sb, k_df, v_df, p_conv))

        q_sb, k_sb, v_sb, q_df, k_df, v_df, glu, gates = _inproj(xs, w_in_b, nb)
        o_sb = _sb_decode(q_sb, sb_kt, sb_vt, l, page_table, DECODE_PAGES_PER_STEP)
        o_df = _df_decode(q_df, k_df, v_df, df_kt, df_v, l, page_table, bias_past, bias_self,
                          lam_vecs, row(subln_g[l]), DECODE_PAGES_PER_STEP, lam_init)
        o_c, u_new = _conv_decode(glu, jnp.transpose(state_conv[l], (1, 0, 2)), *conv_p)
        s_conv = jnp.concatenate([state_conv[l][:, 1:], u_new[:, None, :]], axis=1)
        x1 = _merge(xs, o_sb, o_df, o_c, gates, *merge_w, nb, alpha)
        xs = _moe(x1, ffn, nb, SAMPLE_MOE_BLOCK, alpha)
        new_s.append((k_sb, v_sb, k_df, v_df, s_conv))

    def stacked(rows, i, shape):
        return jnp.stack([r[i] for r in rows]).reshape((depth,) + shape)

    return (xp.reshape(batch, seq, D_MODEL),
            xs.reshape(nb, 1, D_MODEL),
            stacked(new_p, 0, (batch, seq, H_SB, D_SB)),
            stacked(new_p, 1, (batch, seq, H_SB, D_SB)),
            stacked(new_p, 2, (batch, seq, H_DF, 2, D_DF)),
            stacked(new_p, 3, (batch, seq, H_DF, 2 * D_DF)),
            stacked(new_p, 4, (batch, CONV_HIST, C_CONV)),
            stacked(new_s, 0, (nb, 1, H_SB, D_SB)),
            stacked(new_s, 1, (nb, 1, H_SB, D_SB)),
            stacked(new_s, 2, (nb, 1, H_DF, 2, D_DF)),
            stacked(new_s, 3, (nb, 1, H_DF, 2 * D_DF)),
            stacked(new_s, 4, (nb, CONV_HIST, C_CONV)))
```

```python
import functools
import math

import jax
import jax.numpy as jnp
from jax import lax
from jax.experimental import pallas as pl
from jax.experimental.pallas import tpu as pltpu

F32 = jnp.float32
BF16 = jnp.bfloat16

D_MODEL = 1024
H_SB, D_SB = 4, 64
H_DF, D_DF = 4, 64
C_CONV = D_MODEL // 4
CONV_W = 31
N_BUCKETS, MAX_DIST = 32, 128
N_EXPERTS, TOP_K, N_GROUPS, TOPK_GROUPS = 64, 8, 8, 4
GROUP_SIZE = N_EXPERTS // N_GROUPS
D_EXPERT = D_MODEL // 4
D_SHARED = D_MODEL // 4
ROUTE_SCALE = 2.5
LN_EPS = 1e-5
PAGE_SIZE = 128
W_SB = H_SB * D_SB
W_DF = H_DF * 2 * D_DF
IN_SIZES = (W_SB, W_SB, W_SB, W_DF, W_DF, W_DF, 2 * C_CONV, 3 * D_MODEL)
IN_WIDTH = sum(IN_SIZES)

LANES = 128
SUBLANES = 8
VMEM_LIMIT = 56 * 1024 * 1024
NEG_INF = float("-inf")
NT_DIMS = (((1,), (1,)), ((), ()))
QK_SCALE = D_SB ** -0.5
assert D_SB == D_DF and QK_SCALE == 0.125


def _params(semantics, vmem=VMEM_LIMIT):
    return pltpu.CompilerParams(dimension_semantics=semantics, vmem_limit_bytes=vmem)


def _dot(a, b):
    return jnp.dot(a, b, preferred_element_type=F32)


def _dot_nt(a, b):
    return lax.dot_general(a, b, NT_DIMS, preferred_element_type=F32)


def _layer_norm(y, g, b):
    mu = jnp.mean(y, axis=-1, keepdims=True)
    d = y - mu
    var = jnp.mean(d * d, axis=-1, keepdims=True)
    return d * lax.rsqrt(var + LN_EPS) * g + b


def _silu(x):
    return x * jax.nn.sigmoid(x)


def _inproj_kernel(x_ref, w_ref, *out_refs):
    x = x_ref[...].astype(BF16)
    off = 0
    for ref, width in zip(out_refs, IN_SIZES):
        for c in range(0, width, 512):
            cw = min(512, width - c)
            ref[:, c:c + cw] = _dot(x, w_ref[:, off + c:off + c + cw])
        off += width


def _inproj(x, w_bf16, tm):
    n = x.shape[0]
    return pl.pallas_call(
        _inproj_kernel,
        grid=(n // tm,),
        in_specs=[pl.BlockSpec((tm, D_MODEL), lambda i: (i, 0)),
                  pl.BlockSpec((D_MODEL, IN_WIDTH), lambda i: (0, 0), pipeline_mode=pl.Buffered(1))],
        out_specs=[pl.BlockSpec((tm, w), lambda i: (i, 0)) for w in IN_SIZES],
        out_shape=[jax.ShapeDtypeStruct((n, w), F32) for w in IN_SIZES],
        compiler_params=_params(("parallel",)),
        name="inproj",
    )(x, w_bf16)


def _causal_pairs(nq, descending):
    qi, kj = [], []
    for i in range(nq):
        ks = range(i, -1, -1) if descending else range(i + 1)
        for j in ks:
            qi.append(i)
            kj.append(j)
    return jnp.asarray(qi, jnp.int32), jnp.asarray(kj, jnp.int32)


def _suffix_matrix():
    r = lax.broadcasted_iota(jnp.int32, (LANES, LANES), 0)
    c = lax.broadcasted_iota(jnp.int32, (LANES, LANES), 1)
    u = (r > c).astype(BF16)
    return jnp.concatenate([u, jnp.ones((LANES, LANES), BF16)], axis=1)


def _split_dot(x, rhs_bf16, terms=2):
    out = None
    for _ in range(terms):
        part = x.astype(BF16)
        x = x - part.astype(F32)
        d = _dot(part, rhs_bf16)
        out = d if out is None else out + d
    return out


def _log_sigmoid_pair(z):
    sp = jnp.log1p(jnp.exp(-jnp.abs(z)))
    return jnp.minimum(z, 0.0) - sp, -jnp.maximum(z, 0.0) - sp


EXP_UNDERFLOW = -104.0


def _sb_kernel(qi_ref, kj_ref, q_ref, k_ref, v_ref, tri_ref, o_ref, carry_ref, acc_ref, live_ref,
               lb_ref, lk_ref, st_ref, w_ref, *, tq):
    p = pl.program_id(1)
    qi = qi_ref[p]
    kj = kj_ref[p]

    @pl.when(kj == qi)
    def _init():
        carry_ref[...] = jnp.zeros_like(carry_ref)
        acc_ref[...] = jnp.zeros_like(acc_ref)
        live_ref[0] = 1

    def live_step(diag):
        if diag:
            row = lax.broadcasted_iota(jnp.int32, (tq, tq), 0)
            col = lax.broadcasted_iota(jnp.int32, (tq, tq), 1)
            valid = col < row
        chunks = tq // LANES
        for h in range(H_SB):
            hs = slice(h * D_SB, (h + 1) * D_SB)
            z = _dot_nt((q_ref[:, hs] * QK_SCALE).astype(BF16), k_ref[:, hs].astype(BF16))
            log_beta, log_keep = _log_sigmoid_pair(z)
            if diag:
                log_keep = jnp.where(valid, log_keep, 0.0)
            lb_ref[h] = log_beta
            lk_ref[h] = log_keep
        for h in range(H_SB):
            for c in range(chunks):
                st_ref[h, c] = _split_dot(lk_ref[h, :, c * LANES:(c + 1) * LANES], tri_ref[...])
        worst = None
        for h in range(H_SB):
            carry = carry_ref[h]
            for c in reversed(range(chunks)):
                sl = slice(c * LANES, (c + 1) * LANES)
                w = jnp.exp(lb_ref[h, :, sl] + st_ref[h, c, :, :LANES] + carry)
                if diag:
                    w = jnp.where(valid[:, sl], w, 0.0)
                w_ref[h, :, sl] = w.astype(BF16)
                carry = carry + st_ref[h, c, :, LANES:]
            carry_ref[h] = carry
            top = jnp.max(carry)
            worst = top if worst is None else jnp.maximum(worst, top)
        for h in range(H_SB):
            hs = slice(h * D_SB, (h + 1) * D_SB)
            acc_ref[h] = acc_ref[h] + _dot(w_ref[h], v_ref[:, hs].astype(BF16))
        live_ref[0] = (worst >= EXP_UNDERFLOW).astype(jnp.int32)

    def step(diag):
        @pl.when(live_ref[0] == 1)
        def _live():
            live_step(diag)

    @pl.when(kj == qi)
    def _diag():
        step(True)

    @pl.when(kj < qi)
    def _off():
        step(False)

    @pl.when(kj == 0)
    def _fin():
        for h in range(H_SB):
            o_ref[:, h * D_SB:(h + 1) * D_SB] = acc_ref[h]


def _sb_attention(q, k, v, batch, seq, tq):
    nq = seq // tq
    qi, kj = _causal_pairs(nq, descending=True)
    grid_spec = pltpu.PrefetchScalarGridSpec(
        num_scalar_prefetch=2,
        grid=(batch, int(qi.shape[0])),
        in_specs=[pl.BlockSpec((tq, W_SB), lambda b, p, qi, kj: (b * nq + qi[p], 0)),
                  pl.BlockSpec((tq, W_SB), lambda b, p, qi, kj: (b * nq + kj[p], 0)),
                  pl.BlockSpec((tq, W_SB), lambda b, p, qi, kj: (b * nq + kj[p], 0)),
                  pl.BlockSpec((LANES, 2 * LANES), lambda b, p, qi, kj: (0, 0))],
        out_specs=pl.BlockSpec((tq, W_SB), lambda b, p, qi, kj: (b * nq + qi[p], 0)),
        scratch_shapes=[pltpu.VMEM((H_SB, tq, LANES), F32), pltpu.VMEM((H_SB, tq, D_SB), F32),
                        pltpu.SMEM((1,), jnp.int32),
                        pltpu.VMEM((H_SB, tq, tq), F32), pltpu.VMEM((H_SB, tq, tq), F32),
                        pltpu.VMEM((H_SB, tq // LANES, tq, 2 * LANES), F32), pltpu.VMEM((H_SB, tq, tq), BF16)],
    )
    return pl.pallas_call(
        functools.partial(_sb_kernel, tq=tq),
        grid_spec=grid_spec,
        out_shape=jax.ShapeDtypeStruct((batch * seq, W_SB), F32),
        compiler_params=_params(("parallel", "arbitrary")),
        name="sb_attention",
    )(qi, kj, q, k, v, _suffix_matrix())


def _bias_of_distance(table, n):
    max_exact = N_BUCKETS // 2
    nf = jnp.maximum(n, 1).astype(F32)
    large = max_exact + (jnp.log(nf / max_exact) / math.log(MAX_DIST / max_exact)
                         * (N_BUCKETS - max_exact)).astype(jnp.int32)
    large = jnp.minimum(large, N_BUCKETS - 1)
    bucket = jnp.where(n < max_exact, n, large)
    out = jnp.zeros(n.shape + (H_DF,), F32)
    for b in range(N_BUCKETS):
        out = jnp.where((bucket == b)[..., None], table[b].astype(F32), out)
    return out


def _bias_tiles(table, tq):
    assert tq >= MAX_DIST, "blocks two or more behind must all fall in the last bucket"
    key = jnp.arange(tq, dtype=jnp.int32)[:, None]
    query = jnp.arange(tq, dtype=jnp.int32)[None, :]
    tiles = [_bias_of_distance(table, jnp.maximum(t * tq + query - key, 0)) for t in range(3)]
    return jnp.transpose(jnp.stack(tiles), (0, 3, 1, 2))


def _lambda(lq1, lk1, lq2, lk2, lam_init):
    return (jnp.exp(jnp.sum(lq1 * lk1, axis=-1, keepdims=True))
            - jnp.exp(jnp.sum(lq2 * lk2, axis=-1, keepdims=True)) + lam_init)


def _sub_norm(o, g, post_scale):
    ms = jnp.mean(o * o, axis=-1, keepdims=True)
    return o * lax.rsqrt(ms + LN_EPS) * g * post_scale


def _df_kernel(qi_ref, kj_ref, q_ref, k_ref, v_ref, bias_ref, lam_ref, g_ref, o_ref,
               m_ref, l_ref, acc_ref, s_ref, p_ref, *, tq, lam_init):
    p = pl.program_id(1)
    qi = qi_ref[p]
    kj = kj_ref[p]

    @pl.when(kj == 0)
    def _init():
        m_ref[...] = jnp.full_like(m_ref, NEG_INF)
        l_ref[...] = jnp.zeros_like(l_ref)
        acc_ref[...] = jnp.zeros_like(acc_ref)

    def step(diag):
        tile = jnp.minimum(qi - kj, 2)
        if diag:
            key = lax.broadcasted_iota(jnp.int32, (tq, tq), 0)
            query = lax.broadcasted_iota(jnp.int32, (tq, tq), 1)
            causal = key <= query
        for h in range(H_DF):
            bias = bias_ref[tile, h]
            for mp in range(2):
                r = 2 * h + mp
                rs = slice(r * D_DF, (r + 1) * D_DF)
                s = _dot_nt(k_ref[:, rs].astype(BF16), (q_ref[:, rs] * QK_SCALE).astype(BF16)) + bias
                if diag:
                    s = jnp.where(causal, s, NEG_INF)
                s_ref[r] = s
        alphas = []
        for r in range(2 * H_DF):
            s = s_ref[r]
            m_old = m_ref[r]
            m_new = jnp.maximum(m_old, jnp.max(s, axis=0, keepdims=True))
            pr = jnp.exp(s - m_new)
            alphas.append(jnp.exp(m_old - m_new))
            l_ref[r] = alphas[r] * l_ref[r] + jnp.sum(pr, axis=0, keepdims=True)
            m_ref[r] = m_new
            p_ref[r] = pr.astype(BF16)
        for h in range(H_DF):
            vt = v_ref[:, h * 2 * D_DF:(h + 1) * 2 * D_DF].T.astype(BF16)
            for mp in range(2):
                r = 2 * h + mp
                acc_ref[r] = alphas[r] * acc_ref[r] + _dot(vt, p_ref[r])

    @pl.when(kj == qi)
    def _diag():
        step(True)

    @pl.when(kj < qi)
    def _off():
        step(False)

    @pl.when(kj == qi)
    def _fin():
        lam = _lambda(lam_ref[0:1, :], lam_ref[1:2, :], lam_ref[2:3, :], lam_ref[3:4, :], lam_init)
        for h in range(H_DF):
            o = acc_ref[2 * h] / l_ref[2 * h] - lam * (acc_ref[2 * h + 1] / l_ref[2 * h + 1])
            ms = jnp.mean(o * o, axis=0, keepdims=True)
            o = o * lax.rsqrt(ms + LN_EPS) * g_ref[...] * (1.0 - lam_init)
            o_ref[:, h * 2 * D_DF:(h + 1) * 2 * D_DF] = o.T


def _df_attention(q, k, v, bias_tiles, lam_vecs, subln_g, batch, seq, tq, lam_init):
    nq = seq // tq
    qi, kj = _causal_pairs(nq, descending=False)
    blk = lambda which: (lambda b, p, qi, kj: (b * nq + (qi if which == "q" else kj)[p], 0))
    grid_spec = pltpu.PrefetchScalarGridSpec(
        num_scalar_prefetch=2,
        grid=(batch, int(qi.shape[0])),
        in_specs=[pl.BlockSpec((tq, W_DF), blk("q")),
                  pl.BlockSpec((tq, W_DF), blk("k")),
                  pl.BlockSpec((tq, W_DF), blk("k")),
                  pl.BlockSpec((3, H_DF, tq, tq), lambda b, p, qi, kj: (0, 0, 0, 0)),
                  pl.BlockSpec((4, D_DF), lambda b, p, qi, kj: (0, 0)),
                  pl.BlockSpec((2 * D_DF, 1), lambda b, p, qi, kj: (0, 0))],
        out_specs=pl.BlockSpec((tq, W_DF), blk("q")),
        scratch_shapes=[pltpu.VMEM((2 * H_DF, 1, tq), F32), pltpu.VMEM((2 * H_DF, 1, tq), F32),
                        pltpu.VMEM((2 * H_DF, 2 * D_DF, tq), F32),
                        pltpu.VMEM((2 * H_DF, tq, tq), F32), pltpu.VMEM((2 * H_DF, tq, tq), BF16)],
    )
    return pl.pallas_call(
        functools.partial(_df_kernel, tq=tq, lam_init=lam_init),
        grid_spec=grid_spec,
        out_shape=jax.ShapeDtypeStruct((batch * seq, W_DF), F32),
        compiler_params=_params(("parallel", "arbitrary")),
        name="df_attention",
    )(qi, kj, q, k, v, bias_tiles, lam_vecs, subln_g)


CONV_HIST = CONV_W - 1
CONV_PAD = 32
CONV_CHUNK = 64


def _bf16_round(x):
    return x.astype(BF16).astype(F32)


def _conv_kernel(glu_ref, w_ref, cb_ref, g_ref, b_ref, o_ref, st_ref, ubuf_ref, tail_ref, *, tt):
    i = pl.program_id(1)

    @pl.when(i == 0)
    def _zero_history():
        ubuf_ref[0:CONV_PAD, :] = jnp.zeros((CONV_PAD, C_CONV), F32)

    @pl.when(i > 0)
    def _carry_history():
        ubuf_ref[0:CONV_PAD, :] = ubuf_ref[tt:tt + CONV_PAD, :]

    u = glu_ref[:, :C_CONV] * jax.nn.sigmoid(glu_ref[:, C_CONV:])
    ubuf_ref[CONV_PAD:CONV_PAD + tt, :] = _bf16_round(u)
    taps = [_bf16_round(w_ref[j:j + 1, :]) for j in range(CONV_W)]
    first = CONV_PAD - CONV_HIST
    for r0 in range(0, tt, CONV_CHUNK):
        acc = jnp.zeros((CONV_CHUNK, C_CONV), F32)
        for j in range(CONV_W):
            acc = acc + ubuf_ref[first + r0 + j:first + r0 + j + CONV_CHUNK, :] * taps[j]
        y = _layer_norm(acc + cb_ref[...], g_ref[...], b_ref[...])
        o_ref[r0:r0 + CONV_CHUNK, :] = _silu(y)

    @pl.when(i == pl.num_programs(1) - 1)
    def _final_state():
        tail_ref[...] = u[tt - CONV_PAD:, :]
        st_ref[...] = tail_ref[CONV_PAD - CONV_HIST:, :]


def _conv_branch(glu, conv_w, conv_b, g, b, batch, seq, tt):
    nt = seq // tt
    vec = lambda: pl.BlockSpec((1, C_CONV), lambda bb, i: (0, 0))
    return pl.pallas_call(
        functools.partial(_conv_kernel, tt=tt),
        grid=(batch, nt),
        in_specs=[pl.BlockSpec((tt, 2 * C_CONV), lambda bb, i: (bb * nt + i, 0)),
                  pl.BlockSpec((CONV_W, C_CONV), lambda bb, i: (0, 0)),
                  vec(), vec(), vec()],
        out_specs=[pl.BlockSpec((tt, C_CONV), lambda bb, i: (bb * nt + i, 0)),
                   pl.BlockSpec((None, CONV_HIST, C_CONV), lambda bb, i: (bb, 0, 0))],
        out_shape=[jax.ShapeDtypeStruct((batch * seq, C_CONV), F32),
                   jax.ShapeDtypeStruct((batch, CONV_HIST, C_CONV), F32)],
        scratch_shapes=[pltpu.VMEM((CONV_PAD + tt, C_CONV), F32), pltpu.VMEM((CONV_PAD, C_CONV), F32)],
        compiler_params=_params(("parallel", "arbitrary")),
        name="conv_branch",
    )(glu, conv_w, conv_b, g, b)


def _conv_dec_kernel(glu_ref, st_ref, w_ref, cb_ref, g_ref, b_ref, o_ref, u_ref):
    u = glu_ref[:, :C_CONV] * jax.nn.sigmoid(glu_ref[:, C_CONV:])
    acc = u * w_ref[CONV_HIST:CONV_W, :]
    for j in range(CONV_HIST):
        acc = acc + st_ref[j] * w_ref[j:j + 1, :]
    y = _layer_norm(acc + cb_ref[...], g_ref[...], b_ref[...])
    o_ref[...] = _silu(y)
    u_ref[...] = u


def _conv_decode(glu, state_t, conv_w, conv_b, g, b):
    n = glu.shape[0]
    return pl.pallas_call(
        _conv_dec_kernel,
        out_shape=[jax.ShapeDtypeStruct((n, C_CONV), F32), jax.ShapeDtypeStruct((n, C_CONV), F32)],
        name="conv_decode",
    )(glu, state_t, conv_w, conv_b, g, b)


def _merge_kernel(x_ref, osb_ref, odf_ref, oc_ref, gate_ref, wsb_ref, wdf_ref, wc_ref, wo_ref,
                  g_ref, b_ref, o_ref, *, alpha):
    merged = (jax.nn.sigmoid(gate_ref[:, 0:D_MODEL]) * _dot(osb_ref[...].astype(BF16), wsb_ref[...])
              + jax.nn.sigmoid(gate_ref[:, D_MODEL:2 * D_MODEL]) * _dot(odf_ref[...].astype(BF16), wdf_ref[...])
              + jax.nn.sigmoid(gate_ref[:, 2 * D_MODEL:]) * _dot(oc_ref[...].astype(BF16), wc_ref[...]))
    m = _dot(merged.astype(BF16), wo_ref[...])
    o_ref[...] = _layer_norm(alpha * x_ref[...] + m, g_ref[...], b_ref[...])


def _merge(x, o_sb, o_df, o_c, gates, w_sb, w_df, w_c, w_o, g, b, tm, alpha):
    n = x.shape[0]
    rows = lambda w: pl.BlockSpec((tm, w), lambda i: (i, 0))
    full = lambda a: pl.BlockSpec(a.shape, lambda i: (0, 0))
    return pl.pallas_call(
        functools.partial(_merge_kernel, alpha=alpha),
        grid=(n // tm,),
        in_specs=[rows(D_MODEL), rows(W_SB), rows(W_DF), rows(C_CONV), rows(3 * D_MODEL),
                  full(w_sb), full(w_df), full(w_c), full(w_o), full(g), full(b)],
        out_specs=rows(D_MODEL),
        out_shape=jax.ShapeDtypeStruct((n, D_MODEL), F32),
        compiler_params=_params(("parallel",)),
        name="merge_ln",
    )(x, o_sb, o_df, o_c, gates, w_sb, w_df, w_c, w_o, g, b)


def _first_argmax(vals, ids, axis, sentinel):
    mx = jnp.max(vals, axis=axis, keepdims=True)
    ix = jnp.min(jnp.where(vals == mx, ids, sentinel), axis=axis, keepdims=True)
    return mx, ix


def _router_kernel(x_ref, rw_ref, rb_ref, before_ref, idx_ref, w_ref, rank_ref, count_ref, seen_ref, *, tm):
    @pl.when(pl.program_id(0) == 0)
    def _init():
        seen_ref[...] = jnp.zeros_like(seen_ref)

    logits = _dot_nt(rw_ref[...], x_ref[...].astype(BF16))
    scores = jax.nn.sigmoid(logits)
    biased = scores + rb_ref[...]
    sub = lax.broadcasted_iota(jnp.int32, (GROUP_SIZE, tm), 0).astype(F32)
    gscore = jnp.zeros((N_GROUPS, tm), F32)
    for g in range(N_GROUPS):
        blk = biased[g * GROUP_SIZE:(g + 1) * GROUP_SIZE, :]
        m1, i1 = _first_argmax(blk, sub, 0, float(GROUP_SIZE))
        m2 = jnp.max(jnp.where(sub == i1, NEG_INF, blk), axis=0, keepdims=True)
        gscore = jnp.where(sub == float(g), m1 + m2, gscore)
    gmask = jnp.zeros((N_GROUPS, tm), F32)
    for _ in range(TOPK_GROUPS):
        _, ig = _first_argmax(gscore, sub, 0, float(N_GROUPS))
        sel = sub == ig
        gmask = jnp.where(sel, 1.0, gmask)
        gscore = jnp.where(sel, NEG_INF, gscore)
    eid = lax.broadcasted_iota(jnp.int32, (N_EXPERTS, tm), 0).astype(F32)
    cand = jnp.concatenate(
        [jnp.where(gmask[g:g + 1, :] > 0.0, biased[g * GROUP_SIZE:(g + 1) * GROUP_SIZE, :], NEG_INF)
         for g in range(N_GROUPS)], axis=0)
    total = jnp.zeros((1, tm), F32)
    picked, chosen = [], []
    member = jnp.zeros((N_EXPERTS, tm), F32)
    for k in range(TOP_K):
        _, ie = _first_argmax(cand, eid, 0, float(N_EXPERTS))
        sel = eid == ie
        wk = jnp.sum(jnp.where(sel, scores, 0.0), axis=0, keepdims=True)
        cand = jnp.where(sel, NEG_INF, cand)
        member = jnp.where(sel, 1.0, member)
        idx_ref[k:k + 1, :] = ie.astype(jnp.int32)
        picked.append(wk)
        chosen.append(ie)
        total = total + wk
    for k in range(TOP_K):
        w_ref[k:k + 1, :] = picked[k] / total * ROUTE_SCALE
    earlier = _dot(member.astype(BF16), before_ref[...]) + seen_ref[...]
    for k in range(TOP_K):
        rank_ref[k:k + 1, :] = jnp.sum(jnp.where(eid == chosen[k], earlier, 0.0), axis=0,
                                       keepdims=True).astype(jnp.int32)
    seen_ref[...] = seen_ref[...] + jnp.sum(member, axis=1, keepdims=True)
    count_ref[...] = seen_ref[...].astype(jnp.int32)


def _router(x, rw_t, rb_col, tm):
    n = x.shape[0]
    r = lax.broadcasted_iota(jnp.int32, (tm, tm), 0)
    c = lax.broadcasted_iota(jnp.int32, (tm, tm), 1)
    before = (r < c).astype(BF16)
    pairs = lambda: pl.BlockSpec((TOP_K, tm), lambda i: (0, i))
    return pl.pallas_call(
        functools.partial(_router_kernel, tm=tm),
        grid=(n // tm,),
        in_specs=[pl.BlockSpec((tm, D_MODEL), lambda i: (i, 0)),
                  pl.BlockSpec((N_EXPERTS, D_MODEL), lambda i: (0, 0)),
                  pl.BlockSpec((N_EXPERTS, 1), lambda i: (0, 0)),
                  pl.BlockSpec((tm, tm), lambda i: (0, 0))],
        out_specs=[pairs(), pairs(), pairs(), pl.BlockSpec((N_EXPERTS, 1), lambda i: (0, 0))],
        out_shape=[jax.ShapeDtypeStruct((TOP_K, n), jnp.int32),
                   jax.ShapeDtypeStruct((TOP_K, n), F32),
                   jax.ShapeDtypeStruct((TOP_K, n), jnp.int32),
                   jax.ShapeDtypeStruct((N_EXPERTS, 1), jnp.int32)],
        scratch_shapes=[pltpu.VMEM((N_EXPERTS, 1), F32)],
        compiler_params=_params(("arbitrary",)),
        name="router",
    )(x, rw_t, rb_col, before)


def _wait_row_gather(src_hbm, dst_ref, sem):
    pltpu.make_async_copy(src_hbm.at[pl.ds(0, dst_ref.shape[0])], dst_ref, sem).wait()


def _dispatch_tables(top_idx_t, rank_t, counts, br):
    k, n = top_idx_t.shape
    counts = counts.reshape(N_EXPERTS)
    padded = (counts + br - 1) // br * br
    pend = jnp.cumsum(padded)
    pstart = pend - padded
    experts = jnp.arange(N_EXPERTS, dtype=jnp.int32)
    dest = rank_t + jnp.sum(jnp.where(top_idx_t[:, :, None] == experts, pstart, 0), axis=-1)
    n_blocks = -(-(k * n) // br) + N_EXPERTS
    first_row = jnp.arange(n_blocks, dtype=jnp.int32)[:, None] * br
    blk_expert = jnp.minimum(jnp.sum((pend[None, :] <= first_row).astype(jnp.int32), axis=1), N_EXPERTS - 1)
    n_used = (pend[-1] // br).astype(jnp.int32).reshape(1)
    return dest.astype(jnp.int32), blk_expert.astype(jnp.int32), n_used, pend.astype(jnp.int32), n_blocks


def _dispatch_kernel(dest_ref, pend_ref, nused_ref, x_ref, xs_hbm, zbuf, sem, zsem, *, tm, br, n_blocks):
    i = pl.program_id(0)
    rows = TOP_K * tm

    @pl.when(i == 0)
    def _zero_fill():
        zbuf[...] = jnp.zeros_like(zbuf)

        def fill(first_row):
            return pltpu.make_async_copy(zbuf, xs_hbm.at[pl.ds(pl.multiple_of(first_row, br), br)], zsem)

        def has_rows(e):
            return pend_ref[e] > (pend_ref[e - 1] if e else 0)

        for e in range(N_EXPERTS):
            @pl.when(has_rows(e))
            def _start(e=e):
                fill(pend_ref[e] - br).start()

        def start_tail(b, carry):
            fill(b * br).start()
            return carry
        lax.fori_loop(nused_ref[0], n_blocks, start_tail, 0)

        for e in range(N_EXPERTS):
            @pl.when(has_rows(e))
            def _wait(e=e):
                fill(pend_ref[e] - br).wait()

        def wait_tail(b, carry):
            fill(b * br).wait()
            return carry
        lax.fori_loop(nused_ref[0], n_blocks, wait_tail, 0)

    def wait_one_tile():
        pltpu.make_async_copy(xs_hbm.at[pl.ds(0, rows)], xs_hbm.at[pl.ds(0, rows)], sem).wait()

    def group(g, carry):
        base = pl.multiple_of(g * SUBLANES, SUBLANES)
        for u in range(SUBLANES):
            for k in range(TOP_K):
                row = dest_ref[k, i * tm + base + u]
                pltpu.make_async_copy(x_ref.at[pl.ds(base + u, 1)], xs_hbm.at[pl.ds(row, 1)], sem).start()
        return carry
    lax.fori_loop(0, tm // SUBLANES, group, 0)

    wait_one_tile()


def _dispatch(x, dest, pend, n_used, n_blocks, tm, br):
    n = x.shape[0]
    grid_spec = pltpu.PrefetchScalarGridSpec(
        num_scalar_prefetch=3,
        grid=(n // tm,),
        in_specs=[pl.BlockSpec((tm, D_MODEL), lambda i, d, pe, nu: (i, 0))],
        out_specs=pl.BlockSpec(memory_space=pl.ANY),
        scratch_shapes=[pltpu.VMEM((br, D_MODEL), F32), pltpu.SemaphoreType.DMA(()), pltpu.SemaphoreType.DMA(())],
    )
    return pl.pallas_call(
        functools.partial(_dispatch_kernel, tm=tm, br=br, n_blocks=n_blocks),
        grid_spec=grid_spec,
        out_shape=jax.ShapeDtypeStruct((n_blocks * br, D_MODEL), F32),
        compiler_params=_params(("arbitrary",)),
        name="dispatch",
    )(dest, pend, n_used, x)


def _expert_kernel(be_ref, nused_ref, xs_ref, wg_ref, wu_ref, wd_ref, o_ref, wg_b, wu_b, wd_b):
    b = pl.program_id(0)
    new_expert = (b == 0) | (be_ref[b] != be_ref[jnp.maximum(b - 1, 0)])

    @pl.when(new_expert & (b < nused_ref[0]))
    def _round_weights():
        wg_b[...] = wg_ref[...].astype(BF16)
        wu_b[...] = wu_ref[...].astype(BF16)
        wd_b[...] = wd_ref[...].astype(BF16)

    @pl.when(b < nused_ref[0])
    def _compute():
        x = xs_ref[...].astype(BF16)
        hdn = _silu(_dot(x, wg_b[...])) * _dot(x, wu_b[...])
        o_ref[...] = _dot(hdn.astype(BF16), wd_b[...])

    @pl.when(b >= nused_ref[0])
    def _unused():
        o_ref[...] = jnp.zeros_like(o_ref)


def _experts(xs, blk_expert, n_used, expert_weights, br):
    wg, wu, wd, layer = expert_weights
    n_blocks = blk_expert.shape[0]
    used = lambda b, nu: jnp.maximum(jnp.minimum(b, nu[0] - 1), 0)
    grid_spec = pltpu.PrefetchScalarGridSpec(
        num_scalar_prefetch=2,
        grid=(n_blocks,),
        in_specs=[pl.BlockSpec((br, D_MODEL), lambda b, be, nu: (used(b, nu), 0)),
                  pl.BlockSpec((None, None, D_MODEL, D_EXPERT), lambda b, be, nu: (layer, be[b], 0, 0)),
                  pl.BlockSpec((None, None, D_MODEL, D_EXPERT), lambda b, be, nu: (layer, be[b], 0, 0)),
                  pl.BlockSpec((None, None, D_EXPERT, D_MODEL), lambda b, be, nu: (layer, be[b], 0, 0))],
        out_specs=pl.BlockSpec((br, D_MODEL), lambda b, be, nu: (b, 0)),
        scratch_shapes=[pltpu.VMEM((D_MODEL, D_EXPERT), BF16), pltpu.VMEM((D_MODEL, D_EXPERT), BF16),
                        pltpu.VMEM((D_EXPERT, D_MODEL), BF16)],
    )
    return pl.pallas_call(
        _expert_kernel,
        grid_spec=grid_spec,
        out_shape=jax.ShapeDtypeStruct((n_blocks * br, D_MODEL), F32),
        compiler_params=_params(("arbitrary",)),
        name="experts",
    )(blk_expert, n_used, xs, wg, wu, wd)


def _ffn_out_kernel(dest_ref, x_ref, tw_ref, out_hbm, sg_ref, su_ref, sd_ref, g_ref, b_ref, o_ref, ybuf, sem,
                    *, alpha, tm):
    i = pl.program_id(0)
    slot = i % 2

    def gather(tile, into):
        def group(g, carry):
            base = pl.multiple_of(g * SUBLANES, SUBLANES)
            for u in range(SUBLANES):
                for k in range(TOP_K):
                    row = dest_ref[k, tile * tm + base + u]
                    pltpu.make_async_copy(out_hbm.at[pl.ds(row, 1)], ybuf.at[into, k, pl.ds(base + u, 1)],
                                          sem.at[into]).start()
            return carry
        lax.fori_loop(0, tm // SUBLANES, group, 0)

    @pl.when(i == 0)
    def _first():
        gather(0, 0)

    @pl.when(i + 1 < pl.num_programs(0))
    def _prefetch():
        gather(i + 1, 1 - slot)

    for k in range(TOP_K):
        _wait_row_gather(out_hbm, ybuf.at[slot, k], sem.at[slot])
    x = x_ref[...]
    xb = x.astype(BF16)
    hdn = _silu(_dot(xb, sg_ref[...])) * _dot(xb, su_ref[...])
    y = ybuf[slot, 0] * tw_ref[:, 0:1]
    for k in range(1, TOP_K):
        y = y + ybuf[slot, k] * tw_ref[:, k:k + 1]
    y = y + _dot(hdn.astype(BF16), sd_ref[...])
    o_ref[...] = _layer_norm(alpha * x + y, g_ref[...], b_ref[...])


def _ffn_out(x, expert_out, dest, top_w, sg, su, sd, g, b, tm, alpha):
    n = x.shape[0]
    rows = lambda w: pl.BlockSpec((tm, w), lambda i, d: (i, 0))
    full = lambda a: pl.BlockSpec(a.shape, lambda i, d: (0, 0))
    grid_spec = pltpu.PrefetchScalarGridSpec(
        num_scalar_prefetch=1,
        grid=(n // tm,),
        in_specs=[rows(D_MODEL), rows(TOP_K), pl.BlockSpec(memory_space=pl.ANY),
                  full(sg), full(su), full(sd), full(g), full(b)],
        out_specs=rows(D_MODEL),
        scratch_shapes=[pltpu.VMEM((2, TOP_K, tm, D_MODEL), F32), pltpu.SemaphoreType.DMA((2,))],
    )
    return pl.pallas_call(
        functools.partial(_ffn_out_kernel, alpha=alpha, tm=tm),
        grid_spec=grid_spec,
        out_shape=jax.ShapeDtypeStruct((n, D_MODEL), F32),
        compiler_params=_params(("arbitrary",)),
        name="ffn_out_ln",
    )(dest, x, top_w, expert_out, sg, su, sd, g, b)


def _moe(x1, f, tm, br, alpha):
    router_w, router_b, expert_weights, sg, su, sd, ln_g, ln_b = f
    top_idx_t, top_w_t, rank_t, counts = _router(x1, router_w, router_b, tm)
    dest, blk_expert, n_used, pend, n_blocks = _dispatch_tables(top_idx_t, rank_t, counts, br)
    xs = _dispatch(x1, dest, pend, n_used, n_blocks, min(x1.shape[0], DISPATCH_TM), br)
    out = _experts(xs, blk_expert, n_used, expert_weights, br)
    return _ffn_out(x1, out, dest, top_w_t.T, sg, su, sd, ln_g, ln_b, min(tm, COMBINE_TM), alpha)


def _sb_dec_kernel(pt_ref, q_ref, *refs, pc):
    k_refs, v_refs = refs[:pc], refs[pc:2 * pc]
    tri_ref, o_ref, carry_ref, acc_ref = refs[2 * pc:]
    c = pl.program_id(1)

    @pl.when(c == 0)
    def _init():
        carry_ref[...] = jnp.zeros_like(carry_ref)
        acc_ref[...] = jnp.zeros_like(acc_ref)

    row = lax.broadcasted_iota(jnp.int32, (SUBLANES, W_SB), 0)
    col = lax.broadcasted_iota(jnp.int32, (SUBLANES, W_SB), 1)
    own = (col // D_SB) == row
    qbd = jnp.where(own, jnp.broadcast_to(q_ref[...], (SUBLANES, W_SB)), 0.0).astype(BF16)
    carry = carry_ref[...]
    acc = acc_ref[...]
    z = jnp.concatenate([_dot(qbd, k_refs[p][...].astype(BF16)) for p in range(pc)], axis=0) * QK_SCALE
    log_beta, log_keep = _log_sigmoid_pair(z)
    st = _split_dot(log_keep, tri_ref[...], terms=3)
    for p in range(pc):
        rows = slice(p * SUBLANES, (p + 1) * SUBLANES)
        w = jnp.exp(log_beta[rows] + st[rows, :LANES] + carry)
        acc = acc + _dot_nt(w.astype(BF16), v_refs[p][...].astype(BF16))
        carry = carry + st[rows, LANES:]
    carry_ref[...] = carry
    acc_ref[...] = acc

    @pl.when(c == pl.num_programs(1) - 1)
    def _fin():
        o_ref[...] = jnp.sum(jnp.where(own, acc, 0.0), axis=0, keepdims=True)


def _sb_decode(q, cache_kt, cache_vt, layer, page_table, pc):
    nb, n_pages = page_table.shape

    def page_spec(p):
        def index(b, c, pt):
            return (layer, pt[b * n_pages + n_pages - 1 - (c * pc + p)], 0, 0)
        return pl.BlockSpec((None, None, W_SB, PAGE_SIZE), index)

    row = lambda w: pl.BlockSpec((None, 1, w), lambda b, c, pt: (b, 0, 0))
    grid_spec = pltpu.PrefetchScalarGridSpec(
        num_scalar_prefetch=1,
        grid=(nb, n_pages // pc),
        in_specs=([row(W_SB)]
                  + [page_spec(p) for p in range(pc)]
                  + [page_spec(p) for p in range(pc)]
                  + [pl.BlockSpec((LANES, 2 * LANES), lambda b, c, pt: (0, 0))]),
        out_specs=row(W_SB),
        scratch_shapes=[pltpu.VMEM((SUBLANES, LANES), F32), pltpu.VMEM((SUBLANES, W_SB), F32)],
    )
    out = pl.pallas_call(
        functools.partial(_sb_dec_kernel, pc=pc),
        grid_spec=grid_spec,
        out_shape=jax.ShapeDtypeStruct((nb, 1, W_SB), F32),
        compiler_params=_params(("parallel", "arbitrary")),
        name="sb_decode",
    )(page_table.reshape(-1), q.reshape(nb, 1, W_SB), *([cache_kt] * pc), *([cache_vt] * pc), _suffix_matrix())
    return out.reshape(nb, W_SB)


def _df_dec_kernel(pt_ref, q_ref, kn_ref, vn_ref, bias_ref, bself_ref, lam_ref, g_ref, spread_ref, *refs,
                   pc, n_pages, lam_init):
    k_refs, v_refs = refs[:pc], refs[pc:2 * pc]
    o_ref, s_ref, m_ref, aself_ref, acc_ref = refs[2 * pc:]
    c = pl.program_id(1)
    nc = n_pages // pc

    @pl.when(c == 0)
    def _init():
        m_ref[...] = jnp.full_like(m_ref, NEG_INF)
        acc_ref[...] = jnp.zeros_like(acc_ref)

    def page_cols(page):
        return pl.ds(pl.multiple_of(page * PAGE_SIZE, PAGE_SIZE), PAGE_SIZE)

    @pl.when(c < nc)
    def _scores():
        row = lax.broadcasted_iota(jnp.int32, (SUBLANES, W_DF), 0)
        col = lax.broadcasted_iota(jnp.int32, (SUBLANES, W_DF), 1)
        own = (col // D_DF) == 2 * (row % H_DF) + row // H_DF
        qbd = jnp.where(own, jnp.broadcast_to(q_ref[...], (SUBLANES, W_DF)), 0.0).astype(BF16)
        m = m_ref[...]
        for p in range(pc):
            cols = page_cols(c * pc + p)
            s = _dot(qbd, k_refs[p][...].astype(BF16)) * (D_DF ** -0.5) + bias_ref[:, cols]
            s_ref[:, cols] = s
            m = jnp.maximum(m, jnp.max(s, axis=-1, keepdims=True))
        m_ref[...] = m

        @pl.when(c == nc - 1)
        def _weights():
            kn = jnp.broadcast_to(_bf16_round(kn_ref[...]), (SUBLANES, W_DF))
            s_self = jnp.sum(qbd.astype(F32) * kn, axis=-1, keepdims=True) * (D_DF ** -0.5) + bself_ref[...]
            m_fin = jnp.maximum(m, s_self)
            pr = jnp.exp(s_ref[...] - m_fin)
            p_self = jnp.exp(s_self - m_fin)
            total = jnp.sum(pr, axis=-1, keepdims=True) + p_self
            lam = _lambda(lam_ref[0:1, :], lam_ref[1:2, :], lam_ref[2:3, :], lam_ref[3:4, :], lam_init)
            pn = pr / total
            pn_self = jnp.broadcast_to(p_self / total, (SUBLANES, LANES))
            s_ref[...] = pn - lam * pltpu.roll(pn, shift=H_DF, axis=0)
            aself_ref[...] = pn_self - lam * pltpu.roll(pn_self, shift=H_DF, axis=0)

    @pl.when(c >= nc)
    def _values():
        rows = PAGE_SIZE * H_DF
        row = lax.broadcasted_iota(jnp.int32, (SUBLANES, rows), 0)
        col = lax.broadcasted_iota(jnp.int32, (SUBLANES, rows), 1)
        own = (col % H_DF) == row
        acc = acc_ref[...]
        weights = jnp.concatenate([s_ref[:, page_cols((c - nc) * pc + p)] for p in range(pc)], axis=0)
        spread = _dot(weights.astype(BF16), spread_ref[...])
        for p in range(pc):
            own_head = jnp.where(own, spread[p * SUBLANES:(p + 1) * SUBLANES], 0.0)
            acc = acc + _dot(own_head.astype(BF16), v_refs[p][...].astype(BF16))
        acc_ref[...] = acc

        @pl.when(c == 2 * nc - 1)
        def _fin():
            o = acc[0:H_DF] + _bf16_round(aself_ref[0:H_DF, 0:1]) * _bf16_round(vn_ref[...])
            o_ref[...] = _sub_norm(o, g_ref[...], 1.0 - lam_init)


def _df_decode(q, k_new, v_new, cache_kt, cache_v, layer, page_table, bias_past, bias_self, lam_vecs, subln_g,
               pc, lam_init):
    nb, n_pages = page_table.shape
    nc = n_pages // pc
    head_w = 2 * D_DF
    row = lambda w: pl.BlockSpec((None, 1, w), lambda b, c, pt: (b, 0, 0))
    heads = pl.BlockSpec((None, H_DF, head_w), lambda b, c, pt: (b, 0, 0))
    full = lambda a: pl.BlockSpec(a.shape, lambda b, c, pt: (0,) * a.ndim)

    def page_spec(p, second_pass):
        def index(b, c, pt):
            chunk = jnp.maximum(c - nc, 0) if second_pass else jnp.minimum(c, nc - 1)
            return (layer, pt[b * n_pages + chunk * pc + p], 0, 0)
        shape = (None, None, PAGE_SIZE * H_DF, head_w) if second_pass else (None, None, W_DF, PAGE_SIZE)
        return pl.BlockSpec(shape, index)

    r = lax.broadcasted_iota(jnp.int32, (PAGE_SIZE, PAGE_SIZE * H_DF), 0)
    c = lax.broadcasted_iota(jnp.int32, (PAGE_SIZE, PAGE_SIZE * H_DF), 1)
    spread = (c // H_DF == r).astype(BF16)
    grid_spec = pltpu.PrefetchScalarGridSpec(
        num_scalar_prefetch=1,
        grid=(nb, 2 * nc),
        in_specs=([row(W_DF), row(W_DF), heads, full(bias_past), full(bias_self), full(lam_vecs),
                   full(subln_g), full(spread)]
                  + [page_spec(p, False) for p in range(pc)]
                  + [page_spec(p, True) for p in range(pc)]),
        out_specs=heads,
        scratch_shapes=[pltpu.VMEM((SUBLANES, n_pages * PAGE_SIZE), F32), pltpu.VMEM((SUBLANES, 1), F32),
                        pltpu.VMEM((SUBLANES, LANES), F32), pltpu.VMEM((SUBLANES, head_w), F32)],
    )
    r3 = lambda a: a.reshape(nb, 1, W_DF)
    out = pl.pallas_call(
        functools.partial(_df_dec_kernel, pc=pc, n_pages=n_pages, lam_init=lam_init),
        grid_spec=grid_spec,
        out_shape=jax.ShapeDtypeStruct((nb, H_DF, head_w), F32),
        compiler_params=_params(("parallel", "arbitrary")),
        name="df_decode",
    )(page_table.reshape(-1), r3(q), r3(k_new), v_new.reshape(nb, H_DF, head_w), bias_past, bias_self, lam_vecs,
      subln_g, spread, *([cache_kt] * pc), *([cache_v] * pc))
    return out.reshape(nb, W_DF)


PROMPT_TM = 256
PROMPT_ATT_BLOCK = 256
PROMPT_CONV_TILE = 512
PROMPT_MOE_BLOCK = 256
COMBINE_TM = 256
DISPATCH_TM = 512
SAMPLE_MOE_BLOCK = 32
DECODE_PAGES_PER_STEP = 16


def kernel(x_prompt, x_sample, cache_sb_k, cache_sb_v, cache_df_k, cache_df_v, state_conv, page_table,
           w_in, rel_bias_table, lam_q1, lam_k1, lam_q2, lam_k2, subln_g, conv_w, conv_b, conv_ln_g,
           conv_ln_b, w_sb_out, w_df_out, w_conv_out, w_o, ln1_g, ln1_b, router_w, router_bias, w_gate,
           w_up, w_down, sh_gate, sh_up, sh_down, ln2_g, ln2_b):
    depth = w_in.shape[0]
    batch, seq, _ = x_prompt.shape
    nb = x_sample.shape[0]
    n_pages = page_table.shape[1]
    past_len = n_pages * PAGE_SIZE
    alpha = (2 * depth) ** 0.25
    n_phys = cache_sb_k.shape[1]

    xp = x_prompt.reshape(batch * seq, D_MODEL)
    xs = x_sample.reshape(nb, D_MODEL)
    bias_tiles = _bias_tiles(rel_bias_table, PROMPT_ATT_BLOCK)
    dist = past_len - jnp.arange(past_len, dtype=jnp.int32)
    bias_past = jnp.tile(_bias_of_distance(rel_bias_table, dist).T, (2, 1))
    bias_self = jnp.tile(_bias_of_distance(rel_bias_table, jnp.zeros((1,), jnp.int32)).T, (2, 1))
    row = lambda a: a.reshape(1, -1)
    sb_kt = jnp.transpose(cache_sb_k, (0, 1, 3, 4, 2)).reshape(depth, n_phys, W_SB, PAGE_SIZE)
    sb_vt = jnp.transpose(cache_sb_v, (0, 1, 3, 4, 2)).reshape(depth, n_phys, W_SB, PAGE_SIZE)
    df_kt = jnp.transpose(cache_df_k, (0, 1, 3, 4, 5, 2)).reshape(depth, n_phys, W_DF, PAGE_SIZE)
    df_v = cache_df_v.reshape(depth, n_phys, PAGE_SIZE * H_DF, 2 * D_DF)

    new_p, new_s = [], []
    for l in range(depth):
        lam_init = 0.8 - 0.6 * math.exp(-0.3 * l)
        w_in_b = w_in[l].astype(BF16)
        lam_vecs = jnp.stack([lam_q1[l], lam_k1[l], lam_q2[l], lam_k2[l]])
        merge_w = (w_sb_out[l].astype(BF16), w_df_out[l].astype(BF16), w_conv_out[l].astype(BF16),
                   w_o[l].astype(BF16), row(ln1_g[l]), row(ln1_b[l]))
        ffn = (router_w[l].T.astype(BF16), router_bias[l].reshape(N_EXPERTS, 1),
               (w_gate, w_up, w_down, l),
               sh_gate[l].astype(BF16), sh_up[l].astype(BF16), sh_down[l].astype(BF16),
               row(ln2_g[l]), row(ln2_b[l]))
        conv_p = (conv_w[l], row(conv_b[l]), row(conv_ln_g[l]), row(conv_ln_b[l]))

        q_sb, k_sb, v_sb, q_df, k_df, v_df, glu, gates = _inproj(xp, w_in_b, PROMPT_TM)
        o_sb = _sb_attention(q_sb, k_sb, v_sb, batch, seq, PROMPT_ATT_BLOCK)
        o_df = _df_attention(q_df, k_df, v_df, bias_tiles, lam_vecs, subln_g[l].reshape(2 * D_DF, 1), batch, seq,
                             PROMPT_ATT_BLOCK, lam_init)
        o_c, p_conv = _conv_branch(glu, *conv_p, batch, seq, PROMPT_CONV_TILE)
        x1 = _merge(xp, o_sb, o_df, o_c, gates, *merge_w, PROMPT_TM, alpha)
        xp = _moe(x1, ffn, PROMPT_TM, PROMPT_MOE_BLOCK, alpha)
        new_p.append((k_sb, v_sb, k_df, v_df, p_conv))

        q_sb, k_sb, v_sb, q_df, k_df, v_df, glu, gates = _inproj(xs, w_in_b, nb)
        o_sb = _sb_decode(q_sb, sb_kt, sb_vt, l, page_table, DECODE_PAGES_PER_STEP)
        o_df = _df_decode(q_df, k_df, v_df, df_kt, df_v, l, page_table, bias_past, bias_self,
                          lam_vecs, row(subln_g[l]), DECODE_PAGES_PER_STEP, lam_init)
        o_c, u_new = _conv_decode(glu, jnp.transpose(state_conv[l], (1, 0, 2)), *conv_p)
        s_conv = jnp.concatenate([state_conv[l][:, 1:], u_new[:, None, :]], axis=1)
        x1 = _merge(xs, o_sb, o_df, o_c, gates, *merge_w, nb, alpha)
        xs = _moe(x1, ffn, nb, SAMPLE_MOE_BLOCK, alpha)
        new_s.append((k_sb, v_sb, k_df, v_df, s_conv))

    def stacked(rows, i, shape):
        return jnp.stack([r[i] for r in rows]).reshape((depth,) + shape)

    return (xp.reshape(batch, seq, D_MODEL),
            xs.reshape(nb, 1, D_MODEL),
            stacked(new_p, 0, (batch, seq, H_SB, D_SB)),
            stacked(new_p, 1, (batch, seq, H_SB, D_SB)),
            stacked(new_p, 2, (batch, seq, H_DF, 2, D_DF)),
            stacked(new_p, 3, (batch, seq, H_DF, 2 * D_DF)),
            stacked(new_p, 4, (batch, CONV_HIST, C_CONV)),
            stacked(new_s, 0, (nb, 1, H_SB, D_SB)),
            stacked(new_s, 1, (nb, 1, H_SB, D_SB)),
            stacked(new_s, 2, (nb, 1, H_DF, 2, D_DF)),
            stacked(new_s, 3, (nb, 1, H_DF, 2 * D_DF)),
            stacked(new_s, 4, (nb, CONV_HIST, C_CONV)))
```

```python
import functools
import math

import jax
import jax.numpy as jnp
from jax import lax
from jax.experimental import pallas as pl
from jax.experimental.pallas import tpu as pltpu

F32 = jnp.float32
BF16 = jnp.bfloat16

D_MODEL = 1024
H_SB, D_SB = 4, 64
H_DF, D_DF = 4, 64
C_CONV = D_MODEL // 4
CONV_W = 31
N_BUCKETS, MAX_DIST = 32, 128
N_EXPERTS, TOP_K, N_GROUPS, TOPK_GROUPS = 64, 8, 8, 4
GROUP_SIZE = N_EXPERTS // N_GROUPS
D_EXPERT = D_MODEL // 4
D_SHARED = D_MODEL // 4
ROUTE_SCALE = 2.5
LN_EPS = 1e-5
PAGE_SIZE = 128
W_SB = H_SB * D_SB
W_DF = H_DF * 2 * D_DF
IN_SIZES = (W_SB, W_SB, W_SB, W_DF, W_DF, W_DF, 2 * C_CONV, 3 * D_MODEL)
IN_WIDTH = sum(IN_SIZES)

LANES = 128
SUBLANES = 8
VMEM_LIMIT = 56 * 1024 * 1024
NEG_INF = float("-inf")
NT_DIMS = (((1,), (1,)), ((), ()))
QK_SCALE = D_SB ** -0.5
assert D_SB == D_DF and QK_SCALE == 0.125


def _params(semantics, vmem=VMEM_LIMIT):
    return pltpu.CompilerParams(dimension_semantics=semantics, vmem_limit_bytes=vmem)


def _dot(a, b):
    return jnp.dot(a, b, preferred_element_type=F32)


def _dot_nt(a, b):
    return lax.dot_general(a, b, NT_DIMS, preferred_element_type=F32)


def _layer_norm(y, g, b):
    mu = jnp.mean(y, axis=-1, keepdims=True)
    d = y - mu
    var = jnp.mean(d * d, axis=-1, keepdims=True)
    return d * lax.rsqrt(var + LN_EPS) * g + b


def _silu(x):
    return x * jax.nn.sigmoid(x)


def _inproj_kernel(x_ref, w_ref, *out_refs):
    x = x_ref[...].astype(BF16)
    off = 0
    for ref, width in zip(out_refs, IN_SIZES):
        for c in range(0, width, 512):
            cw = min(512, width - c)
            ref[:, c:c + cw] = _dot(x, w_ref[:, off + c:off + c + cw])
        off += width


def _inproj(x, w_bf16, tm):
    n = x.shape[0]
    return pl.pallas_call(
        _inproj_kernel,
        grid=(n // tm,),
        in_specs=[pl.BlockSpec((tm, D_MODEL), lambda i: (i, 0)),
                  pl.BlockSpec((D_MODEL, IN_WIDTH), lambda i: (0, 0), pipeline_mode=pl.Buffered(1))],
        out_specs=[pl.BlockSpec((tm, w), lambda i: (i, 0)) for w in IN_SIZES],
        out_shape=[jax.ShapeDtypeStruct((n, w), F32) for w in IN_SIZES],
        compiler_params=_params(("parallel",)),
        name="inproj",
    )(x, w_bf16)


def _causal_pairs(nq, descending):
    qi, kj = [], []
    for i in range(nq):
        ks = range(i, -1, -1) if descending else range(i + 1)
        for j in ks:
            qi.append(i)
            kj.append(j)
    return jnp.asarray(qi, jnp.int32), jnp.asarray(kj, jnp.int32)


def _suffix_matrix():
    r = lax.broadcasted_iota(jnp.int32, (LANES, LANES), 0)
    c = lax.broadcasted_iota(jnp.int32, (LANES, LANES), 1)
    u = (r > c).astype(BF16)
    return jnp.concatenate([u, jnp.ones((LANES, LANES), BF16)], axis=1)


def _split_dot(x, rhs_bf16, terms=2):
    out = None
    for _ in range(terms):
        part = x.astype(BF16)
        x = x - part.astype(F32)
        d = _dot(part, rhs_bf16)
        out = d if out is None else out + d
    return out


def _log_sigmoid_pair(z):
    sp = jnp.log1p(jnp.exp(-jnp.abs(z)))
    return jnp.minimum(z, 0.0) - sp, -jnp.maximum(z, 0.0) - sp


EXP_UNDERFLOW = -104.0


def _sb_kernel(qi_ref, kj_ref, q_ref, k_ref, v_ref, tri_ref, o_ref, carry_ref, acc_ref, live_ref,
               lb_ref, lk_ref, st_ref, w_ref, *, tq):
    p = pl.program_id(1)
    qi = qi_ref[p]
    kj = kj_ref[p]

    @pl.when(kj == qi)
    def _init():
        carry_ref[...] = jnp.zeros_like(carry_ref)
        acc_ref[...] = jnp.zeros_like(acc_ref)
        live_ref[0] = 1

    def live_step(diag):
        if diag:
            row = lax.broadcasted_iota(jnp.int32, (tq, tq), 0)
            col = lax.broadcasted_iota(jnp.int32, (tq, tq), 1)
            valid = col < row
        chunks = tq // LANES
        for h in range(H_SB):
            hs = slice(h * D_SB, (h + 1) * D_SB)
            z = _dot_nt((q_ref[:, hs] * QK_SCALE).astype(BF16), k_ref[:, hs].astype(BF16))
            log_beta, log_keep = _log_sigmoid_pair(z)
            if diag:
                log_keep = jnp.where(valid, log_keep, 0.0)
            lb_ref[h] = log_beta
            lk_ref[h] = log_keep
        for h in range(H_SB):
            for c in range(chunks):
                st_ref[h, c] = _split_dot(lk_ref[h, :, c * LANES:(c + 1) * LANES], tri_ref[...])
        worst = None
        for h in range(H_SB):
            carry = carry_ref[h]
            for c in reversed(range(chunks)):
                sl = slice(c * LANES, (c + 1) * LANES)
                w = jnp.exp(lb_ref[h, :, sl] + st_ref[h, c, :, :LANES] + carry)
                if diag:
                    w = jnp.where(valid[:, sl], w, 0.0)
                w_ref[h, :, sl] = w.astype(BF16)
                carry = carry + st_ref[h, c, :, LANES:]
            carry_ref[h] = carry
            top = jnp.max(carry)
            worst = top if worst is None else jnp.maximum(worst, top)
        for h in range(H_SB):
            hs = slice(h * D_SB, (h + 1) * D_SB)
            acc_ref[h] = acc_ref[h] + _dot(w_ref[h], v_ref[:, hs].astype(BF16))
        live_ref[0] = (worst >= EXP_UNDERFLOW).astype(jnp.int32)

    def step(diag):
        @pl.when(live_ref[0] == 1)
        def _live():
            live_step(diag)

    @pl.when(kj == qi)
    def _diag():
        step(True)

    @pl.when(kj < qi)
    def _off():
        step(False)

    @pl.when(kj == 0)
    def _fin():
        for h in range(H_SB):
            o_ref[:, h * D_SB:(h + 1) * D_SB] = acc_ref[h]


def _sb_attention(q, k, v, batch, seq, tq):
    nq = seq // tq
    qi, kj = _causal_pairs(nq, descending=True)
    grid_spec = pltpu.PrefetchScalarGridSpec(
        num_scalar_prefetch=2,
        grid=(batch, int(qi.shape[0])),
        in_specs=[pl.BlockSpec((tq, W_SB), lambda b, p, qi, kj: (b * nq + qi[p], 0)),
                  pl.BlockSpec((tq, W_SB), lambda b, p, qi, kj: (b * nq + kj[p], 0)),
                  pl.BlockSpec((tq, W_SB), lambda b, p, qi, kj: (b * nq + kj[p], 0)),
                  pl.BlockSpec((LANES, 2 * LANES), lambda b, p, qi, kj: (0, 0))],
        out_specs=pl.BlockSpec((tq, W_SB), lambda b, p, qi, kj: (b * nq + qi[p], 0)),
        scratch_shapes=[pltpu.VMEM((H_SB, tq, LANES), F32), pltpu.VMEM((H_SB, tq, D_SB), F32),
                        pltpu.SMEM((1,), jnp.int32),
                        pltpu.VMEM((H_SB, tq, tq), F32), pltpu.VMEM((H_SB, tq, tq), F32),
                        pltpu.VMEM((H_SB, tq // LANES, tq, 2 * LANES), F32), pltpu.VMEM((H_SB, tq, tq), BF16)],
    )
    return pl.pallas_call(
        functools.partial(_sb_kernel, tq=tq),
        grid_spec=grid_spec,
        out_shape=jax.ShapeDtypeStruct((batch * seq, W_SB), F32),
        compiler_params=_params(("parallel", "arbitrary")),
        name="sb_attention",
    )(qi, kj, q, k, v, _suffix_matrix())


def _bias_of_distance(table, n):
    max_exact = N_BUCKETS // 2
    nf = jnp.maximum(n, 1).astype(F32)
    large = max_exact + (jnp.log(nf / max_exact) / math.log(MAX_DIST / max_exact)
                         * (N_BUCKETS - max_exact)).astype(jnp.int32)
    large = jnp.minimum(large, N_BUCKETS - 1)
    bucket = jnp.where(n < max_exact, n, large)
    out = jnp.zeros(n.shape + (H_DF,), F32)
    for b in range(N_BUCKETS):
        out = jnp.where((bucket == b)[..., None], table[b].astype(F32), out)
    return out


def _bias_tiles(table, tq):
    assert tq >= MAX_DIST, "blocks two or more behind must all fall in the last bucket"
    key = jnp.arange(tq, dtype=jnp.int32)[:, None]
    query = jnp.arange(tq, dtype=jnp.int32)[None, :]
    tiles = [_bias_of_distance(table, jnp.maximum(t * tq + query - key, 0)) for t in range(3)]
    return jnp.transpose(jnp.stack(tiles), (0, 3, 1, 2))


def _lambda(lq1, lk1, lq2, lk2, lam_init):
    return (jnp.exp(jnp.sum(lq1 * lk1, axis=-1, keepdims=True))
            - jnp.exp(jnp.sum(lq2 * lk2, axis=-1, keepdims=True)) + lam_init)


def _sub_norm(o, g, post_scale):
    ms = jnp.mean(o * o, axis=-1, keepdims=True)
    return o * lax.rsqrt(ms + LN_EPS) * g * post_scale


def _df_kernel(qi_ref, kj_ref, q_ref, k_ref, v_ref, bias_ref, lam_ref, g_ref, o_ref,
               m_ref, l_ref, acc_ref, s_ref, p_ref, *, tq, lam_init):
    p = pl.program_id(1)
    qi = qi_ref[p]
    kj = kj_ref[p]

    @pl.when(kj == 0)
    def _init():
        m_ref[...] = jnp.full_like(m_ref, NEG_INF)
        l_ref[...] = jnp.zeros_like(l_ref)
        acc_ref[...] = jnp.zeros_like(acc_ref)

    def step(diag):
        tile = jnp.minimum(qi - kj, 2)
        if diag:
            key = lax.broadcasted_iota(jnp.int32, (tq, tq), 0)
            query = lax.broadcasted_iota(jnp.int32, (tq, tq), 1)
            causal = key <= query
        for h in range(H_DF):
            bias = bias_ref[tile, h]
            for mp in range(2):
                r = 2 * h + mp
                rs = slice(r * D_DF, (r + 1) * D_DF)
                s = _dot_nt(k_ref[:, rs].astype(BF16), (q_ref[:, rs] * QK_SCALE).astype(BF16)) + bias
                if diag:
                    s = jnp.where(causal, s, NEG_INF)
                s_ref[r] = s
        alphas = []
        for r in range(2 * H_DF):
            s = s_ref[r]
            m_old = m_ref[r]
            m_new = jnp.maximum(m_old, jnp.max(s, axis=0, keepdims=True))
            pr = jnp.exp(s - m_new)
            alphas.append(jnp.exp(m_old - m_new))
            l_ref[r] = alphas[r] * l_ref[r] + jnp.sum(pr, axis=0, keepdims=True)
            m_ref[r] = m_new
            p_ref[r] = pr.astype(BF16)
        for h in range(H_DF):
            vt = v_ref[:, h * 2 * D_DF:(h + 1) * 2 * D_DF].T.astype(BF16)
            for mp in range(2):
                r = 2 * h + mp
                acc_ref[r] = alphas[r] * acc_ref[r] + _dot(vt, p_ref[r])

    @pl.when(kj == qi)
    def _diag():
        step(True)

    @pl.when(kj < qi)
    def _off():
        step(False)

    @pl.when(kj == qi)
    def _fin():
        lam = _lambda(lam_ref[0:1, :], lam_ref[1:2, :], lam_ref[2:3, :], lam_ref[3:4, :], lam_init)
        for h in range(H_DF):
            o = acc_ref[2 * h] / l_ref[2 * h] - lam * (acc_ref[2 * h + 1] / l_ref[2 * h + 1])
            ms = jnp.mean(o * o, axis=0, keepdims=True)
            o = o * lax.rsqrt(ms + LN_EPS) * g_ref[...] * (1.0 - lam_init)
            o_ref[:, h * 2 * D_DF:(h + 1) * 2 * D_DF] = o.T


def _df_attention(q, k, v, bias_tiles, lam_vecs, subln_g, batch, seq, tq, lam_init):
    nq = seq // tq
    qi, kj = _causal_pairs(nq, descending=False)
    blk = lambda which: (lambda b, p, qi, kj: (b * nq + (qi if which == "q" else kj)[p], 0))
    grid_spec = pltpu.PrefetchScalarGridSpec(
        num_scalar_prefetch=2,
        grid=(batch, int(qi.shape[0])),
        in_specs=[pl.BlockSpec((tq, W_DF), blk("q")),
                  pl.BlockSpec((tq, W_DF), blk("k")),
                  pl.BlockSpec((tq, W_DF), blk("k")),
                  pl.BlockSpec((3, H_DF, tq, tq), lambda b, p, qi, kj: (0, 0, 0, 0)),
                  pl.BlockSpec((4, D_DF), lambda b, p, qi, kj: (0, 0)),
                  pl.BlockSpec((2 * D_DF, 1), lambda b, p, qi, kj: (0, 0))],
        out_specs=pl.BlockSpec((tq, W_DF), blk("q")),
        scratch_shapes=[pltpu.VMEM((2 * H_DF, 1, tq), F32), pltpu.VMEM((2 * H_DF, 1, tq), F32),
                        pltpu.VMEM((2 * H_DF, 2 * D_DF, tq), F32),
                        pltpu.VMEM((2 * H_DF, tq, tq), F32), pltpu.VMEM((2 * H_DF, tq, tq), BF16)],
    )
    return pl.pallas_call(
        functools.partial(_df_kernel, tq=tq, lam_init=lam_init),
        grid_spec=grid_spec,
        out_shape=jax.ShapeDtypeStruct((batch * seq, W_DF), F32),
        compiler_params=_params(("parallel", "arbitrary")),
        name="df_attention",
    )(qi, kj, q, k, v, bias_tiles, lam_vecs, subln_g)


CONV_HIST = CONV_W - 1
CONV_PAD = 32
CONV_CHUNK = 64


def _bf16_round(x):
    return x.astype(BF16).astype(F32)


def _conv_kernel(glu_ref, w_ref, cb_ref, g_ref, b_ref, o_ref, st_ref, ubuf_ref, tail_ref, *, tt):
    i = pl.program_id(1)

    @pl.when(i == 0)
    def _zero_history():
        ubuf_ref[0:CONV_PAD, :] = jnp.zeros((CONV_PAD, C_CONV), F32)

    @pl.when(i > 0)
    def _carry_history():
        ubuf_ref[0:CONV_PAD, :] = ubuf_ref[tt:tt + CONV_PAD, :]

    u = glu_ref[:, :C_CONV] * jax.nn.sigmoid(glu_ref[:, C_CONV:])
    ubuf_ref[CONV_PAD:CONV_PAD + tt, :] = _bf16_round(u)
    taps = [_bf16_round(w_ref[j:j + 1, :]) for j in range(CONV_W)]
    first = CONV_PAD - CONV_HIST
    for r0 in range(0, tt, CONV_CHUNK):
        acc = jnp.zeros((CONV_CHUNK, C_CONV), F32)
        for j in range(CONV_W):
            acc = acc + ubuf_ref[first + r0 + j:first + r0 + j + CONV_CHUNK, :] * taps[j]
        y = _layer_norm(acc + cb_ref[...], g_ref[...], b_ref[...])
        o_ref[r0:r0 + CONV_CHUNK, :] = _silu(y)

    @pl.when(i == pl.num_programs(1) - 1)
    def _final_state():
        tail_ref[...] = u[tt - CONV_PAD:, :]
        st_ref[...] = tail_ref[CONV_PAD - CONV_HIST:, :]


def _conv_branch(glu, conv_w, conv_b, g, b, batch, seq, tt):
    nt = seq // tt
    vec = lambda: pl.BlockSpec((1, C_CONV), lambda bb, i: (0, 0))
    return pl.pallas_call(
        functools.partial(_conv_kernel, tt=tt),
        grid=(batch, nt),
        in_specs=[pl.BlockSpec((tt, 2 * C_CONV), lambda bb, i: (bb * nt + i, 0)),
                  pl.BlockSpec((CONV_W, C_CONV), lambda bb, i: (0, 0)),
                  vec(), vec(), vec()],
        out_specs=[pl.BlockSpec((tt, C_CONV), lambda bb, i: (bb * nt + i, 0)),
                   pl.BlockSpec((None, CONV_HIST, C_CONV), lambda bb, i: (bb, 0, 0))],
        out_shape=[jax.ShapeDtypeStruct((batch * seq, C_CONV), F32),
                   jax.ShapeDtypeStruct((batch, CONV_HIST, C_CONV), F32)],
        scratch_shapes=[pltpu.VMEM((CONV_PAD + tt, C_CONV), F32), pltpu.VMEM((CONV_PAD, C_CONV), F32)],
        compiler_params=_params(("parallel", "arbitrary")),
        name="conv_branch",
    )(glu, conv_w, conv_b, g, b)


def _conv_dec_kernel(glu_ref, st_ref, w_ref, cb_ref, g_ref, b_ref, o_ref, u_ref):
    u = glu_ref[:, :C_CONV] * jax.nn.sigmoid(glu_ref[:, C_CONV:])
    acc = u * w_ref[CONV_HIST:CONV_W, :]
    for j in range(CONV_HIST):
        acc = acc + st_ref[j] * w_ref[j:j + 1, :]
    y = _layer_norm(acc + cb_ref[...], g_ref[...], b_ref[...])
    o_ref[...] = _silu(y)
    u_ref[...] = u


def _conv_decode(glu, state_t, conv_w, conv_b, g, b):
    n = glu.shape[0]
    return pl.pallas_call(
        _conv_dec_kernel,
        out_shape=[jax.ShapeDtypeStruct((n, C_CONV), F32), jax.ShapeDtypeStruct((n, C_CONV), F32)],
        name="conv_decode",
    )(glu, state_t, conv_w, conv_b, g, b)


def _merge_kernel(x_ref, osb_ref, odf_ref, oc_ref, gate_ref, wsb_ref, wdf_ref, wc_ref, wo_ref,
                  g_ref, b_ref, o_ref, *, alpha):
    merged = (jax.nn.sigmoid(gate_ref[:, 0:D_MODEL]) * _dot(osb_ref[...].astype(BF16), wsb_ref[...])
              + jax.nn.sigmoid(gate_ref[:, D_MODEL:2 * D_MODEL]) * _dot(odf_ref[...].astype(BF16), wdf_ref[...])
              + jax.nn.sigmoid(gate_ref[:, 2 * D_MODEL:]) * _dot(oc_ref[...].astype(BF16), wc_ref[...]))
    m = _dot(merged.astype(BF16), wo_ref[...])
    o_ref[...] = _layer_norm(alpha * x_ref[...] + m, g_ref[...], b_ref[...])


def _merge(x, o_sb, o_df, o_c, gates, w_sb, w_df, w_c, w_o, g, b, tm, alpha):
    n = x.shape[0]
    rows = lambda w: pl.BlockSpec((tm, w), lambda i: (i, 0))
    full = lambda a: pl.BlockSpec(a.shape, lambda i: (0, 0))
    return pl.pallas_call(
        functools.partial(_merge_kernel, alpha=alpha),
        grid=(n // tm,),
        in_specs=[rows(D_MODEL), rows(W_SB), rows(W_DF), rows(C_CONV), rows(3 * D_MODEL),
                  full(w_sb), full(w_df), full(w_c), full(w_o), full(g), full(b)],
        out_specs=rows(D_MODEL),
        out_shape=jax.ShapeDtypeStruct((n, D_MODEL), F32),
        compiler_params=_params(("parallel",)),
        name="merge_ln",
    )(x, o_sb, o_df, o_c, gates, w_sb, w_df, w_c, w_o, g, b)


def _first_argmax(vals, ids, axis, sentinel):
    mx = jnp.max(vals, axis=axis, keepdims=True)
    ix = jnp.min(jnp.where(vals == mx, ids, sentinel), axis=axis, keepdims=True)
    return mx, ix


def _router_kernel(x_ref, rw_ref, rb_ref, before_ref, idx_ref, w_ref, rank_ref, count_ref, seen_ref, *, tm):
    @pl.when(pl.program_id(0) == 0)
    def _init():
        seen_ref[...] = jnp.zeros_like(seen_ref)

    logits = _dot_nt(rw_ref[...], x_ref[...].astype(BF16))
    scores = jax.nn.sigmoid(logits)
    biased = scores + rb_ref[...]
    sub = lax.broadcasted_iota(jnp.int32, (GROUP_SIZE, tm), 0).astype(F32)
    gscore = jnp.zeros((N_GROUPS, tm), F32)
    for g in range(N_GROUPS):
        blk = biased[g * GROUP_SIZE:(g + 1) * GROUP_SIZE, :]
        m1, i1 = _first_argmax(blk, sub, 0, float(GROUP_SIZE))
        m2 = jnp.max(jnp.where(sub == i1, NEG_INF, blk), axis=0, keepdims=True)
        gscore = jnp.where(sub == float(g), m1 + m2, gscore)
    gmask = jnp.zeros((N_GROUPS, tm), F32)
    for _ in range(TOPK_GROUPS):
        _, ig = _first_argmax(gscore, sub, 0, float(N_GROUPS))
        sel = sub == ig
        gmask = jnp.where(sel, 1.0, gmask)
        gscore = jnp.where(sel, NEG_INF, gscore)
    eid = lax.broadcasted_iota(jnp.int32, (N_EXPERTS, tm), 0).astype(F32)
    cand = jnp.concatenate(
        [jnp.where(gmask[g:g + 1, :] > 0.0, biased[g * GROUP_SIZE:(g + 1) * GROUP_SIZE, :], NEG_INF)
         for g in range(N_GROUPS)], axis=0)
    total = jnp.zeros((1, tm), F32)
    picked, chosen = [], []
    member = jnp.zeros((N_EXPERTS, tm), F32)
    for k in range(TOP_K):
        _, ie = _first_argmax(cand, eid, 0, float(N_EXPERTS))
        sel = eid == ie
        wk = jnp.sum(jnp.where(sel, scores, 0.0), axis=0, keepdims=True)
        cand = jnp.where(sel, NEG_INF, cand)
        member = jnp.where(sel, 1.0, member)
        idx_ref[k:k + 1, :] = ie.astype(jnp.int32)
        picked.append(wk)
        chosen.append(ie)
        total = total + wk
    for k in range(TOP_K):
        w_ref[k:k + 1, :] = picked[k] / total * ROUTE_SCALE
    earlier = _dot(member.astype(BF16), before_ref[...]) + seen_ref[...]
    for k in range(TOP_K):
        rank_ref[k:k + 1, :] = jnp.sum(jnp.where(eid == chosen[k], earlier, 0.0), axis=0,
                                       keepdims=True).astype(jnp.int32)
    seen_ref[...] = seen_ref[...] + jnp.sum(member, axis=1, keepdims=True)
    count_ref[...] = seen_ref[...].astype(jnp.int32)


def _router(x, rw_t, rb_col, tm):
    n = x.shape[0]
    r = lax.broadcasted_iota(jnp.int32, (tm, tm), 0)
    c = lax.broadcasted_iota(jnp.int32, (tm, tm), 1)
    before = (r < c).astype(BF16)
    pairs = lambda: pl.BlockSpec((TOP_K, tm), lambda i: (0, i))
    return pl.pallas_call(
        functools.partial(_router_kernel, tm=tm),
        grid=(n // tm,),
        in_specs=[pl.BlockSpec((tm, D_MODEL), lambda i: (i, 0)),
                  pl.BlockSpec((N_EXPERTS, D_MODEL), lambda i: (0, 0)),
                  pl.BlockSpec((N_EXPERTS, 1), lambda i: (0, 0)),
                  pl.BlockSpec((tm, tm), lambda i: (0, 0))],
        out_specs=[pairs(), pairs(), pairs(), pl.BlockSpec((N_EXPERTS, 1), lambda i: (0, 0))],
        out_shape=[jax.ShapeDtypeStruct((TOP_K, n), jnp.int32),
                   jax.ShapeDtypeStruct((TOP_K, n), F32),
                   jax.ShapeDtypeStruct((TOP_K, n), jnp.int32),
                   jax.ShapeDtypeStruct((N_EXPERTS, 1), jnp.int32)],
        scratch_shapes=[pltpu.VMEM((N_EXPERTS, 1), F32)],
        compiler_params=_params(("arbitrary",)),
        name="router",
    )(x, rw_t, rb_col, before)


def _wait_row_gather(src_hbm, dst_ref, sem):
    pltpu.make_async_copy(src_hbm.at[pl.ds(0, dst_ref.shape[0])], dst_ref, sem).wait()


def _dispatch_tables(top_idx_t, rank_t, counts, br):
    k, n = top_idx_t.shape
    counts = counts.reshape(N_EXPERTS)
    padded = (counts + br - 1) // br * br
    pend = jnp.cumsum(padded)
    pstart = pend - padded
    experts = jnp.arange(N_EXPERTS, dtype=jnp.int32)
    dest = rank_t + jnp.sum(jnp.where(top_idx_t[:, :, None] == experts, pstart, 0), axis=-1)
    n_blocks = -(-(k * n) // br) + N_EXPERTS
    first_row = jnp.arange(n_blocks, dtype=jnp.int32)[:, None] * br
    blk_expert = jnp.minimum(jnp.sum((pend[None, :] <= first_row).astype(jnp.int32), axis=1), N_EXPERTS - 1)
    n_used = (pend[-1] // br).astype(jnp.int32).reshape(1)
    return dest.astype(jnp.int32), blk_expert.astype(jnp.int32), n_used, pend.astype(jnp.int32), n_blocks


def _dispatch_kernel(dest_ref, pend_ref, nused_ref, x_ref, xs_hbm, zbuf, sem, zsem, *, tm, br, n_blocks):
    i = pl.program_id(0)
    rows = TOP_K * tm

    @pl.when(i == 0)
    def _zero_fill():
        zbuf[...] = jnp.zeros_like(zbuf)

        def fill(first_row):
            return pltpu.make_async_copy(zbuf, xs_hbm.at[pl.ds(pl.multiple_of(first_row, br), br)], zsem)

        def has_rows(e):
            return pend_ref[e] > (pend_ref[e - 1] if e else 0)

        for e in range(N_EXPERTS):
            @pl.when(has_rows(e))
            def _start(e=e):
                fill(pend_ref[e] - br).start()

        def start_tail(b, carry):
            fill(b * br).start()
            return carry
        lax.fori_loop(nused_ref[0], n_blocks, start_tail, 0)

        for e in range(N_EXPERTS):
            @pl.when(has_rows(e))
            def _wait(e=e):
                fill(pend_ref[e] - br).wait()

        def wait_tail(b, carry):
            fill(b * br).wait()
            return carry
        lax.fori_loop(nused_ref[0], n_blocks, wait_tail, 0)

    def wait_one_tile():
        pltpu.make_async_copy(xs_hbm.at[pl.ds(0, rows)], xs_hbm.at[pl.ds(0, rows)], sem).wait()

    def group(g, carry):
        base = pl.multiple_of(g * SUBLANES, SUBLANES)
        for u in range(SUBLANES):
            for k in range(TOP_K):
                row = dest_ref[k, i * tm + base + u]
                pltpu.make_async_copy(x_ref.at[pl.ds(base + u, 1)], xs_hbm.at[pl.ds(row, 1)], sem).start()
        return carry
    lax.fori_loop(0, tm // SUBLANES, group, 0)

    wait_one_tile()


def _dispatch(x, dest, pend, n_used, n_blocks, tm, br):
    n = x.shape[0]
    grid_spec = pltpu.PrefetchScalarGridSpec(
        num_scalar_prefetch=3,
        grid=(n // tm,),
        in_specs=[pl.BlockSpec((tm, D_MODEL), lambda i, d, pe, nu: (i, 0))],
        out_specs=pl.BlockSpec(memory_space=pl.ANY),
        scratch_shapes=[pltpu.VMEM((br, D_MODEL), F32), pltpu.SemaphoreType.DMA(()), pltpu.SemaphoreType.DMA(())],
    )
    return pl.pallas_call(
        functools.partial(_dispatch_kernel, tm=tm, br=br, n_blocks=n_blocks),
        grid_spec=grid_spec,
        out_shape=jax.ShapeDtypeStruct((n_blocks * br, D_MODEL), F32),
        compiler_params=_params(("arbitrary",)),
        name="dispatch",
    )(dest, pend, n_used, x)


def _expert_kernel(be_ref, nused_ref, xs_ref, wg_ref, wu_ref, wd_ref, o_ref, wg_b, wu_b, wd_b):
    b = pl.program_id(0)
    new_expert = (b == 0) | (be_ref[b] != be_ref[jnp.maximum(b - 1, 0)])

    @pl.when(new_expert & (b < nused_ref[0]))
    def _round_weights():
        wg_b[...] = wg_ref[...].astype(BF16)
        wu_b[...] = wu_ref[...].astype(BF16)
        wd_b[...] = wd_ref[...].astype(BF16)

    @pl.when(b < nused_ref[0])
    def _compute():
        x = xs_ref[...].astype(BF16)
        hdn = _silu(_dot(x, wg_b[...])) * _dot(x, wu_b[...])
        o_ref[...] = _dot(hdn.astype(BF16), wd_b[...])

    @pl.when(b >= nused_ref[0])
    def _unused():
        o_ref[...] = jnp.zeros_like(o_ref)


def _experts(xs, blk_expert, n_used, expert_weights, br):
    wg, wu, wd, layer = expert_weights
    n_blocks = blk_expert.shape[0]
    used = lambda b, nu: jnp.maximum(jnp.minimum(b, nu[0] - 1), 0)
    grid_spec = pltpu.PrefetchScalarGridSpec(
        num_scalar_prefetch=2,
        grid=(n_blocks,),
        in_specs=[pl.BlockSpec((br, D_MODEL), lambda b, be, nu: (used(b, nu), 0)),
                  pl.BlockSpec((None, None, D_MODEL, D_EXPERT), lambda b, be, nu: (layer, be[b], 0, 0)),
                  pl.BlockSpec((None, None, D_MODEL, D_EXPERT), lambda b, be, nu: (layer, be[b], 0, 0)),
                  pl.BlockSpec((None, None, D_EXPERT, D_MODEL), lambda b, be, nu: (layer, be[b], 0, 0))],
        out_specs=pl.BlockSpec((br, D_MODEL), lambda b, be, nu: (b, 0)),
        scratch_shapes=[pltpu.VMEM((D_MODEL, D_EXPERT), BF16), pltpu.VMEM((D_MODEL, D_EXPERT), BF16),
                        pltpu.VMEM((D_EXPERT, D_MODEL), BF16)],
    )
    return pl.pallas_call(
        _expert_kernel,
        grid_spec=grid_spec,
        out_shape=jax.ShapeDtypeStruct((n_blocks * br, D_MODEL), F32),
        compiler_params=_params(("arbitrary",)),
        name="experts",
    )(blk_expert, n_used, xs, wg, wu, wd)


def _ffn_out_kernel(dest_ref, x_ref, tw_ref, out_hbm, sg_ref, su_ref, sd_ref, g_ref, b_ref, o_ref, ybuf, sem,
                    *, alpha, tm):
    i = pl.program_id(0)
    slot = i % 2

    def gather(tile, into):
        def group(g, carry):
            base = pl.multiple_of(g * SUBLANES, SUBLANES)
            for u in range(SUBLANES):
                for k in range(TOP_K):
                    row = dest_ref[k, tile * tm + base + u]
                    pltpu.make_async_copy(out_hbm.at[pl.ds(row, 1)], ybuf.at[into, k, pl.ds(base + u, 1)],
                                          sem.at[into]).start()
            return carry
        lax.fori_loop(0, tm // SUBLANES, group, 0)

    @pl.when(i == 0)
    def _first():
        gather(0, 0)

    @pl.when(i + 1 < pl.num_programs(0))
    def _prefetch():
        gather(i + 1, 1 - slot)

    for k in range(TOP_K):
        _wait_row_gather(out_hbm, ybuf.at[slot, k], sem.at[slot])
    x = x_ref[...]
    xb = x.astype(BF16)
    hdn = _silu(_dot(xb, sg_ref[...])) * _dot(xb, su_ref[...])
    y = ybuf[slot, 0] * tw_ref[:, 0:1]
    for k in range(1, TOP_K):
        y = y + ybuf[slot, k] * tw_ref[:, k:k + 1]
    y = y + _dot(hdn.astype(BF16), sd_ref[...])
    o_ref[...] = _layer_norm(alpha * x + y, g_ref[...], b_ref[...])


def _ffn_out(x, expert_out, dest, top_w, sg, su, sd, g, b, tm, alpha):
    n = x.shape[0]
    rows = lambda w: pl.BlockSpec((tm, w), lambda i, d: (i, 0))
    full = lambda a: pl.BlockSpec(a.shape, lambda i, d: (0, 0))
    grid_spec = pltpu.PrefetchScalarGridSpec(
        num_scalar_prefetch=1,
        grid=(n // tm,),
        in_specs=[rows(D_MODEL), rows(TOP_K), pl.BlockSpec(memory_space=pl.ANY),
                  full(sg), full(su), full(sd), full(g), full(b)],
        out_specs=rows(D_MODEL),
        scratch_shapes=[pltpu.VMEM((2, TOP_K, tm, D_MODEL), F32), pltpu.SemaphoreType.DMA((2,))],
    )
    return pl.pallas_call(
        functools.partial(_ffn_out_kernel, alpha=alpha, tm=tm),
        grid_spec=grid_spec,
        out_shape=jax.ShapeDtypeStruct((n, D_MODEL), F32),
        compiler_params=_params(("arbitrary",)),
        name="ffn_out_ln",
    )(dest, x, top_w, expert_out, sg, su, sd, g, b)


def _moe(x1, f, tm, br, alpha):
    router_w, router_b, expert_weights, sg, su, sd, ln_g, ln_b = f
    top_idx_t, top_w_t, rank_t, counts = _router(x1, router_w, router_b, tm)
    dest, blk_expert, n_used, pend, n_blocks = _dispatch_tables(top_idx_t, rank_t, counts, br)
    xs = _dispatch(x1, dest, pend, n_used, n_blocks, min(x1.shape[0], DISPATCH_TM), br)
    out = _experts(xs, blk_expert, n_used, expert_weights, br)
    return _ffn_out(x1, out, dest, top_w_t.T, sg, su, sd, ln_g, ln_b, min(tm, COMBINE_TM), alpha)


def _sb_dec_kernel(pt_ref, q_ref, *refs, pc):
    k_refs, v_refs = refs[:pc], refs[pc:2 * pc]
    tri_ref, o_ref, carry_ref, acc_ref = refs[2 * pc:]
    c = pl.program_id(1)

    @pl.when(c == 0)
    def _init():
        carry_ref[...] = jnp.zeros_like(carry_ref)
        acc_ref[...] = jnp.zeros_like(acc_ref)

    row = lax.broadcasted_iota(jnp.int32, (SUBLANES, W_SB), 0)
    col = lax.broadcasted_iota(jnp.int32, (SUBLANES, W_SB), 1)
    own = (col // D_SB) == row
    qbd = jnp.where(own, jnp.broadcast_to(q_ref[...], (SUBLANES, W_SB)), 0.0).astype(BF16)
    carry = carry_ref[...]
    acc = acc_ref[...]
    z = jnp.concatenate([_dot(qbd, k_refs[p][...].astype(BF16)) for p in range(pc)], axis=0) * QK_SCALE
    log_beta, log_keep = _log_sigmoid_pair(z)
    st = _split_dot(log_keep, tri_ref[...], terms=3)
    for p in range(pc):
        rows = slice(p * SUBLANES, (p + 1) * SUBLANES)
        w = jnp.exp(log_beta[rows] + st[rows, :LANES] + carry)
        acc = acc + _dot_nt(w.astype(BF16), v_refs[p][...].astype(BF16))
        carry = carry + st[rows, LANES:]
    carry_ref[...] = carry
    acc_ref[...] = acc

    @pl.when(c == pl.num_programs(1) - 1)
    def _fin():
        o_ref[...] = jnp.sum(jnp.where(own, acc, 0.0), axis=0, keepdims=True)


def _sb_decode(q, cache_kt, cache_vt, layer, page_table, pc):
    nb, n_pages = page_table.shape

    def page_spec(p):
        def index(b, c, pt):
            return (layer, pt[b * n_pages + n_pages - 1 - (c * pc + p)], 0, 0)
        return pl.BlockSpec((None, None, W_SB, PAGE_SIZE), index)

    row = lambda w: pl.BlockSpec((None, 1, w), lambda b, c, pt: (b, 0, 0))
    grid_spec = pltpu.PrefetchScalarGridSpec(
        num_scalar_prefetch=1,
        grid=(nb, n_pages // pc),
        in_specs=([row(W_SB)]
                  + [page_spec(p) for p in range(pc)]
                  + [page_spec(p) for p in range(pc)]
                  + [pl.BlockSpec((LANES, 2 * LANES), lambda b, c, pt: (0, 0))]),
        out_specs=row(W_SB),
        scratch_shapes=[pltpu.VMEM((SUBLANES, LANES), F32), pltpu.VMEM((SUBLANES, W_SB), F32)],
    )
    out = pl.pallas_call(
        functools.partial(_sb_dec_kernel, pc=pc),
        grid_spec=grid_spec,
        out_shape=jax.ShapeDtypeStruct((nb, 1, W_SB), F32),
        compiler_params=_params(("parallel", "arbitrary")),
        name="sb_decode",
    )(page_table.reshape(-1), q.reshape(nb, 1, W_SB), *([cache_kt] * pc), *([cache_vt] * pc), _suffix_matrix())
    return out.reshape(nb, W_SB)


def _df_dec_kernel(pt_ref, q_ref, kn_ref, vn_ref, bias_ref, bself_ref, lam_ref, g_ref, spread_ref, *refs,
                   pc, n_pages, lam_init):
    k_refs, v_refs = refs[:pc], refs[pc:2 * pc]
    o_ref, s_ref, m_ref, aself_ref, acc_ref = refs[2 * pc:]
    c = pl.program_id(1)
    nc = n_pages // pc

    @pl.when(c == 0)
    def _init():
        m_ref[...] = jnp.full_like(m_ref, NEG_INF)
        acc_ref[...] = jnp.zeros_like(acc_ref)

    def page_cols(page):
        return pl.ds(pl.multiple_of(page * PAGE_SIZE, PAGE_SIZE), PAGE_SIZE)

    @pl.when(c < nc)
    def _scores():
        row = lax.broadcasted_iota(jnp.int32, (SUBLANES, W_DF), 0)
        col = lax.broadcasted_iota(jnp.int32, (SUBLANES, W_DF), 1)
        own = (col // D_DF) == 2 * (row % H_DF) + row // H_DF
        qbd = jnp.where(own, jnp.broadcast_to(q_ref[...], (SUBLANES, W_DF)), 0.0).astype(BF16)
        m = m_ref[...]
        for p in range(pc):
            cols = page_cols(c * pc + p)
            s = _dot(qbd, k_refs[p][...].astype(BF16)) * (D_DF ** -0.5) + bias_ref[:, cols]
            s_ref[:, cols] = s
            m = jnp.maximum(m, jnp.max(s, axis=-1, keepdims=True))
        m_ref[...] = m

        @pl.when(c == nc - 1)
        def _weights():
            kn = jnp.broadcast_to(_bf16_round(kn_ref[...]), (SUBLANES, W_DF))
            s_self = jnp.sum(qbd.astype(F32) * kn, axis=-1, keepdims=True) * (D_DF ** -0.5) + bself_ref[...]
            m_fin = jnp.maximum(m, s_self)
            pr = jnp.exp(s_ref[...] - m_fin)
            p_self = jnp.exp(s_self - m_fin)
            total = jnp.sum(pr, axis=-1, keepdims=True) + p_self
            lam = _lambda(lam_ref[0:1, :], lam_ref[1:2, :], lam_ref[2:3, :], lam_ref[3:4, :], lam_init)
            pn = pr / total
            pn_self = jnp.broadcast_to(p_self / total, (SUBLANES, LANES))
            s_ref[...] = pn - lam * pltpu.roll(pn, shift=H_DF, axis=0)
            aself_ref[...] = pn_self - lam * pltpu.roll(pn_self, shift=H_DF, axis=0)

    @pl.when(c >= nc)
    def _values():
        rows = PAGE_SIZE * H_DF
        row = lax.broadcasted_iota(jnp.int32, (SUBLANES, rows), 0)
        col = lax.broadcasted_iota(jnp.int32, (SUBLANES, rows), 1)
        own = (col % H_DF) == row
        acc = acc_ref[...]
        weights = jnp.concatenate([s_ref[:, page_cols((c - nc) * pc + p)] for p in range(pc)], axis=0)
        spread = _dot(weights.astype(BF16), spread_ref[...])
        for p in range(pc):
            own_head = jnp.where(own, spread[p * SUBLANES:(p + 1) * SUBLANES], 0.0)
            acc = acc + _dot(own_head.astype(BF16), v_refs[p][...].astype(BF16))
        acc_ref[...] = acc

        @pl.when(c == 2 * nc - 1)
        def _fin():
            o = acc[0:H_DF] + _bf16_round(aself_ref[0:H_DF, 0:1]) * _bf16_round(vn_ref[...])
            o_ref[...] = _sub_norm(o, g_ref[...], 1.0 - lam_init)


def _df_decode(q, k_new, v_new, cache_kt, cache_v, layer, page_table, bias_past, bias_self, lam_vecs, subln_g,
               pc, lam_init):
    nb, n_pages = page_table.shape
    nc = n_pages // pc
    head_w = 2 * D_DF
    row = lambda w: pl.BlockSpec((None, 1, w), lambda b, c, pt: (b, 0, 0))
    heads = pl.BlockSpec((None, H_DF, head_w), lambda b, c, pt: (b, 0, 0))
    full = lambda a: pl.BlockSpec(a.shape, lambda b, c, pt: (0,) * a.ndim)

    def page_spec(p, second_pass):
        def index(b, c, pt):
            chunk = jnp.maximum(c - nc, 0) if second_pass else jnp.minimum(c, nc - 1)
            return (layer, pt[b * n_pages + chunk * pc + p], 0, 0)
        shape = (None, None, PAGE_SIZE * H_DF, head_w) if second_pass else (None, None, W_DF, PAGE_SIZE)
        return pl.BlockSpec(shape, index)

    r = lax.broadcasted_iota(jnp.int32, (PAGE_SIZE, PAGE_SIZE * H_DF), 0)
    c = lax.broadcasted_iota(jnp.int32, (PAGE_SIZE, PAGE_SIZE * H_DF), 1)
    spread = (c // H_DF == r).astype(BF16)
    grid_spec = pltpu.PrefetchScalarGridSpec(
        num_scalar_prefetch=1,
        grid=(nb, 2 * nc),
        in_specs=([row(W_DF), row(W_DF), heads, full(bias_past), full(bias_self), full(lam_vecs),
                   full(subln_g), full(spread)]
                  + [page_spec(p, False) for p in range(pc)]
                  + [page_spec(p, True) for p in range(pc)]),
        out_specs=heads,
        scratch_shapes=[pltpu.VMEM((SUBLANES, n_pages * PAGE_SIZE), F32), pltpu.VMEM((SUBLANES, 1), F32),
                        pltpu.VMEM((SUBLANES, LANES), F32), pltpu.VMEM((SUBLANES, head_w), F32)],
    )
    r3 = lambda a: a.reshape(nb, 1, W_DF)
    out = pl.pallas_call(
        functools.partial(_df_dec_kernel, pc=pc, n_pages=n_pages, lam_init=lam_init),
        grid_spec=grid_spec,
        out_shape=jax.ShapeDtypeStruct((nb, H_DF, head_w), F32),
        compiler_params=_params(("parallel", "arbitrary")),
        name="df_decode",
    )(page_table.reshape(-1), r3(q), r3(k_new), v_new.reshape(nb, H_DF, head_w), bias_past, bias_self, lam_vecs,
      subln_g, spread, *([cache_kt] * pc), *([cache_v] * pc))
    return out.reshape(nb, W_DF)


PROMPT_TM = 256
PROMPT_ATT_BLOCK = 256
PROMPT_CONV_TILE = 512
PROMPT_MOE_BLOCK = 256
COMBINE_TM = 256
DISPATCH_TM = 512
SAMPLE_MOE_BLOCK = 32
DECODE_PAGES_PER_STEP = 32


def kernel(x_prompt, x_sample, cache_sb_k, cache_sb_v, cache_df_k, cache_df_v, state_conv, page_table,
           w_in, rel_bias_table, lam_q1, lam_k1, lam_q2, lam_k2, subln_g, conv_w, conv_b, conv_ln_g,
           conv_ln_b, w_sb_out, w_df_out, w_conv_out, w_o, ln1_g, ln1_b, router_w, router_bias, w_gate,
           w_up, w_down, sh_gate, sh_up, sh_down, ln2_g, ln2_b):
    depth = w_in.shape[0]
    batch, seq, _ = x_prompt.shape
    nb = x_sample.shape[0]
    n_pages = page_table.shape[1]
    past_len = n_pages * PAGE_SIZE
    pages_per_step = math.gcd(n_pages, DECODE_PAGES_PER_STEP)
    alpha = (2 * depth) ** 0.25
    n_phys = cache_sb_k.shape[1]

    xp = x_prompt.reshape(batch * seq, D_MODEL)
    xs = x_sample.reshape(nb, D_MODEL)
    bias_tiles = _bias_tiles(rel_bias_table, PROMPT_ATT_BLOCK)
    dist = past_len - jnp.arange(past_len, dtype=jnp.int32)
    bias_past = jnp.tile(_bias_of_distance(rel_bias_table, dist).T, (2, 1))
    bias_self = jnp.tile(_bias_of_distance(rel_bias_table, jnp.zeros((1,), jnp.int32)).T, (2, 1))
    row = lambda a: a.reshape(1, -1)
    sb_kt = jnp.transpose(cache_sb_k, (0, 1, 3, 4, 2)).reshape(depth, n_phys, W_SB, PAGE_SIZE)
    sb_vt = jnp.transpose(cache_sb_v, (0, 1, 3, 4, 2)).reshape(depth, n_phys, W_SB, PAGE_SIZE)
    df_kt = jnp.transpose(cache_df_k, (0, 1, 3, 4, 5, 2)).reshape(depth, n_phys, W_DF, PAGE_SIZE)
    df_v = cache_df_v.reshape(depth, n_phys, PAGE_SIZE * H_DF, 2 * D_DF)

    new_p, new_s = [], []
    for l in range(depth):
        lam_init = 0.8 - 0.6 * math.exp(-0.3 * l)
        w_in_b = w_in[l].astype(BF16)
        lam_vecs = jnp.stack([lam_q1[l], lam_k1[l], lam_q2[l], lam_k2[l]])
        merge_w = (w_sb_out[l].astype(BF16), w_df_out[l].astype(BF16), w_conv_out[l].astype(BF16),
                   w_o[l].astype(BF16), row(ln1_g[l]), row(ln1_b[l]))
        ffn = (router_w[l].T.astype(BF16), router_bias[l].reshape(N_EXPERTS, 1),
               (w_gate, w_up, w_down, l),
               sh_gate[l].astype(BF16), sh_up[l].astype(BF16), sh_down[l].astype(BF16),
               row(ln2_g[l]), row(ln2_b[l]))
        conv_p = (conv_w[l], row(conv_b[l]), row(conv_ln_g[l]), row(conv_ln_b[l]))

        q_sb, k_sb, v_sb, q_df, k_df, v_df, glu, gates = _inproj(xp, w_in_b, PROMPT_TM)
        o_sb = _sb_attention(q_sb, k_sb, v_sb, batch, seq, PROMPT_ATT_BLOCK)
        o_df = _df_attention(q_df, k_df, v_df, bias_tiles, lam_vecs, subln_g[l].reshape(2 * D_DF, 1), batch, seq,
                             PROMPT_ATT_BLOCK, lam_init)
        o_c, p_conv = _conv_branch(glu, *conv_p, batch, seq, PROMPT_CONV_TILE)
        x1 = _merge(xp, o_sb, o_df, o_c, gates, *merge_w, PROMPT_TM, alpha)
        xp = _moe(x1, ffn, PROMPT_TM, PROMPT_MOE_BLOCK, alpha)
        new_p.append((k_sb, v_sb, k_df, v_df, p_conv))

        q_sb, k_sb, v_sb, q_df, k_df, v_df, glu, gates = _inproj(xs, w_in_b, nb)
        o_sb = _sb_decode(q_sb, sb_kt, sb_vt, l, page_table, pages_per_step)
        o_df = _df_decode(q_df, k_df, v_df, df_kt, df_v, l, page_table, bias_past, bias_self,
                          lam_vecs, row(subln_g[l]), pages_per_step, lam_init)
        o_c, u_new = _conv_decode(glu, jnp.transpose(state_conv[l], (1, 0, 2)), *conv_p)
        s_conv = jnp.concatenate([state_conv[l][:, 1:], u_new[:, None, :]], axis=1)
        x1 = _merge(xs, o_sb, o_df, o_c, gates, *merge_w, nb, alpha)
        xs = _moe(x1, ffn, nb, SAMPLE_MOE_BLOCK, alpha)
        new_s.append((k_sb, v_sb, k_df, v_df, s_conv))

    def stacked(rows, i, shape):
        return jnp.stack([r[i] for r in rows]).reshape((depth,) + shape)

    return (xp.reshape(batch, seq, D_MODEL),
            xs.reshape(nb, 1, D_MODEL),
            stacked(new_p, 0, (batch, seq, H_SB, D_SB)),
            stacked(new_p, 1, (batch, seq, H_SB, D_SB)),
            stacked(new_p, 2, (batch, seq, H_DF, 2, D_DF)),
            stacked(new_p, 3, (batch, seq, H_DF, 2 * D_DF)),
            stacked(new_p, 4, (batch, CONV_HIST, C_CONV)),
            stacked(new_s, 0, (nb, 1, H_SB, D_SB)),
            stacked(new_s, 1, (nb, 1, H_SB, D_SB)),
            stacked(new_s, 2, (nb, 1, H_DF, 2, D_DF)),
            stacked(new_s, 3, (nb, 1, H_DF, 2 * D_DF)),
            stacked(new_s, 4, (nb, CONV_HIST, C_CONV)))
```
